```python
import math
import jax, jax.numpy as jnp
from jax import lax
import numpy as np

D_MODEL = 1024
BATCH = 8
SEQ = 2048
DEPTH = 1
DEC_BATCH = 128
DEC_SEQ = 4
PAST_LEN = 16384
PAGE_SIZE = 128

GLA_WIDTH = D_MODEL // 2
S5_WIDTH = D_MODEL - GLA_WIDTH
GLA_HEADS = 4
GLA_DV = GLA_WIDTH // GLA_HEADS
GLA_DK = GLA_DV // 2
GLA_RANK = 16
GLA_TAU = 16.0
GLA_CHUNK = 64
S5_GROUP_CH = 16
S5_GROUPS = S5_WIDTH // S5_GROUP_CH
S5_STATE = 64
PEER_KEYS = 128
PEER_EXPERTS = PEER_KEYS * PEER_KEYS
PEER_HEADS = 8
PEER_DQ = 256
PEER_TOPK = 16
PEER_BLOCK = 128
EPS = 1e-6

SPLIT_SIZES = (GLA_HEADS * GLA_DK, GLA_HEADS * GLA_DK, GLA_WIDTH, GLA_WIDTH, GLA_RANK, S5_WIDTH)
IN_COLS = sum(SPLIT_SIZES)
SPLIT_POINTS = tuple(int(v) for v in np.cumsum(SPLIT_SIZES)[:-1])

kernel_name = 'hymba_gla_s5_peer_step'


def rmsnorm(x, gain):
    xf = x.astype(jnp.float32)
    y = xf * lax.rsqrt(jnp.mean(xf * xf, axis=-1, keepdims=True) + EPS)
    return (y * gain.astype(jnp.float32)).astype(x.dtype)


def gla_mixer(q, k, v, log_a, s0):
    f32 = jnp.float32
    B, L, H, _ = q.shape
    C = math.gcd(L, GLA_CHUNK)
    n = L // C

    def to_chunks(t):
        return t.astype(f32).reshape(B, n, C, H, t.shape[-1]).transpose(1, 0, 3, 2, 4)

    qc, kc, vc, ac = map(to_chunks, (q * (GLA_DK ** -0.5), k, v, log_a))
    causal = jnp.tril(jnp.ones((C, C), dtype=bool))[:, :, None]

    def step(S, inp):
        qb, kb, vb, ab = inp
        b = jnp.cumsum(ab, axis=2)
        o_inter = jnp.einsum('bhcd,bhde->bhce', qb * jnp.exp(b), S)
        diff = b[:, :, :, None, :] - b[:, :, None, :, :]
        decay = jnp.exp(jnp.where(causal, diff, -jnp.inf))
        scores = jnp.einsum('bhid,bhijd,bhjd->bhij', qb, decay, kb)
        o = o_inter + jnp.einsum('bhij,bhje->bhie', scores, vb)
        b_last = b[:, :, -1:, :]
        S_new = jnp.exp(b_last[:, :, 0, :])[..., None] * S + jnp.einsum(
            'bhcd,bhce->bhde', kb * jnp.exp(b_last - b), vb)
        return S_new, o

    S_fin, o = lax.scan(step, s0.astype(f32), (qc, kc, vc, ac))
    o = o.transpose(1, 0, 3, 2, 4).reshape(B, L, H, vc.shape[-1])
    return o, S_fin


def s5_mixer(u, s_re, s_im, lam_re, lam_im, log_dt, b_re, b_im, c_re, c_im, d_skip):
    f32 = jnp.float32
    Bsz, L, _ = u.shape
    ug = u.astype(f32).reshape(Bsz, L, S5_GROUPS, S5_GROUP_CH)
    dt = jnp.exp(log_dt.astype(f32))[:, None]
    lr, li = lam_re.astype(f32), lam_im.astype(f32)
    mag = jnp.exp(lr * dt)
    ab_re, ab_im = mag * jnp.cos(li * dt), mag * jnp.sin(li * dt)
    den = lr * lr + li * li
    nr, ni = ab_re - 1.0, ab_im
    f_re = (nr * lr + ni * li) / den
    f_im = (ni * lr - nr * li) / den
    br, bi = b_re.astype(f32), b_im.astype(f32)
    bb_re = f_re[..., None] * br - f_im[..., None] * bi
    bb_im = f_re[..., None] * bi + f_im[..., None] * br
    bu_re = jnp.einsum('blgc,gpc->blgp', ug, bb_re)
    bu_im = jnp.einsum('blgc,gpc->blgp', ug, bb_im)
    sr, si = s_re.astype(f32), s_im.astype(f32)
    bu_re = bu_re.at[:, 0].add(ab_re * sr - ab_im * si)
    bu_im = bu_im.at[:, 0].add(ab_re * si + ab_im * sr)
    a_re = jnp.broadcast_to(ab_re, bu_re.shape)
    a_im = jnp.broadcast_to(ab_im, bu_im.shape)

    def combine(e1, e2):
        a1r, a1i, b1r, b1i = e1
        a2r, a2i, b2r, b2i = e2
        return (a1r * a2r - a1i * a2i,
                a1r * a2i + a1i * a2r,
                a2r * b1r - a2i * b1i + b2r,
                a2r * b1i + a2i * b1r + b2i)

    _, _, h_re, h_im = lax.associative_scan(combine, (a_re, a_im, bu_re, bu_im), axis=1)
    y = (jnp.einsum('blgp,gcp->blgc', h_re, c_re.astype(f32))
         - jnp.einsum('blgp,gcp->blgc', h_im, c_im.astype(f32)))
    y = y + d_skip.astype(f32).reshape(S5_GROUPS, S5_GROUP_CH) * ug
    return y.reshape(Bsz, L, S5_WIDTH), h_re[:, -1], h_im[:, -1]


def peer_ffn(h, w_query, sub_keys, expert_u, expert_v):
    f32 = jnp.float32
    Bsz, L, D = h.shape
    T = Bsz * L
    ht = h.reshape(T, D)
    q = (ht @ w_query).reshape(T, PEER_HEADS, 2, PEER_DQ // 2)
    s = jnp.einsum('thpd,hpnd->thpn', q, sub_keys).astype(f32)
    top_s, top_i = lax.top_k(s, PEER_TOPK)
    cand_s = top_s[:, :, 0, :, None] + top_s[:, :, 1, None, :]
    cand_i = top_i[:, :, 0, :, None] * PEER_KEYS + top_i[:, :, 1, None, :]
    cand_s = cand_s.reshape(T, PEER_HEADS, PEER_TOPK * PEER_TOPK)
    cand_i = cand_i.reshape(T, PEER_HEADS, PEER_TOPK * PEER_TOPK)
    fin_s, pos = lax.top_k(cand_s, PEER_TOPK)
    fin_i = jnp.take_along_axis(cand_i, pos, axis=-1)
    gate = jax.nn.softmax(fin_s, axis=-1)
    idx = fin_i.reshape(T, PEER_HEADS * PEER_TOPK)
    gate = gate.reshape(T, PEER_HEADS * PEER_TOPK).astype(h.dtype)
    pad = (-T) % PEER_BLOCK
    nb = (T + pad) // PEER_BLOCK
    xb = jnp.pad(ht, ((0, pad), (0, 0))).reshape(nb, PEER_BLOCK, D)
    ib = jnp.pad(idx, ((0, pad), (0, 0))).reshape(nb, PEER_BLOCK, -1)
    gb = jnp.pad(gate, ((0, pad), (0, 0))).reshape(nb, PEER_BLOCK, -1)

    def block(args):
        xk, ik, gk = args
        act = jax.nn.gelu(jnp.einsum('tkd,td->tk', expert_u[ik], xk))
        return jnp.einsum('tk,tkd->td', gk * act, expert_v[ik])

    out = lax.map(block, (xb, ib, gb))
    return out.reshape(nb * PEER_BLOCK, D)[:T].reshape(Bsz, L, D)


def decoder_layer(x, s_gla, s_re, s_im, norm1, w_in, w_a2, b_a2, gla_norm,
                  lam_re, lam_im, log_dt, b_re, b_im, c_re, c_im, d_skip,
                  w_glu, b_glu, w_out, norm2, peer_wq, peer_keys, peer_u, peer_v):
    Bsz, L, _ = x.shape
    h = rmsnorm(x, norm1)
    proj = h @ w_in
    q, k, v, g, a_lr, u = jnp.split(proj, SPLIT_POINTS, axis=-1)
    log_a = jax.nn.log_sigmoid((a_lr @ w_a2 + b_a2).astype(jnp.float32)) / GLA_TAU
    q = q.reshape(Bsz, L, GLA_HEADS, GLA_DK)
    k = k.reshape(Bsz, L, GLA_HEADS, GLA_DK)
    v = v.reshape(Bsz, L, GLA_HEADS, GLA_DV)
    log_a = log_a.reshape(Bsz, L, GLA_HEADS, GLA_DK)
    o_gla, s_gla_new = gla_mixer(q, k, v, log_a, s_gla)
    o_gla = rmsnorm(o_gla.astype(x.dtype), gla_norm).reshape(Bsz, L, GLA_WIDTH) * jax.nn.silu(g)
    y_s5, sre_new, sim_new = s5_mixer(u, s_re, s_im, lam_re, lam_im, log_dt,
                                      b_re, b_im, c_re, c_im, d_skip)
    z = jax.nn.gelu(y_s5.astype(x.dtype))
    o_s5 = z * jax.nn.sigmoid(z @ w_glu + b_glu)
    x = x + jnp.concatenate([o_gla, o_s5], axis=-1) @ w_out
    x = x + peer_ffn(rmsnorm(x, norm2), peer_wq, peer_keys, peer_u, peer_v)
    return (x, s_gla_new.astype(s_gla.dtype), sre_new.astype(s_re.dtype), sim_new.astype(s_im.dtype))


def setup_inputs(seed: int = 0) -> dict:
    key = jax.random.key(seed)
    ks = iter(jax.random.split(key, 40))
    f32 = jnp.float32

    def nrm(shape, scale):
        return scale * jax.random.normal(next(ks), shape, f32)

    Ld = DEPTH
    n_idx = jnp.arange(S5_STATE, dtype=f32)
    return {
        'x_prompt': nrm((BATCH, SEQ, D_MODEL), 1.0),
        'x_sample': nrm((DEC_BATCH, DEC_SEQ, D_MODEL), 1.0),
        'state_gla': nrm((Ld, DEC_BATCH, GLA_HEADS, GLA_DK, GLA_DV), 4.0),
        'state_s5_re': nrm((Ld, DEC_BATCH, S5_GROUPS, S5_STATE), 1.0),
        'state_s5_im': nrm((Ld, DEC_BATCH, S5_GROUPS, S5_STATE), 1.0),
        'norm1': 1.0 + nrm((Ld, D_MODEL), 0.02),
        'w_in': nrm((Ld, D_MODEL, IN_COLS), D_MODEL ** -0.5),
        'w_a2': nrm((Ld, GLA_RANK, GLA_HEADS * GLA_DK), GLA_RANK ** -0.5),
        'b_a2': 2.0 + nrm((Ld, GLA_HEADS * GLA_DK), 0.1),
        'gla_norm': 1.0 + nrm((Ld, GLA_HEADS, GLA_DV), 0.02),
        's5_lam_re': -0.5 + nrm((Ld, S5_GROUPS, S5_STATE), 0.01),
        's5_lam_im': math.pi * n_idx + nrm((Ld, S5_GROUPS, S5_STATE), 0.01),
        's5_log_dt': jax.random.uniform(next(ks), (Ld, S5_GROUPS), f32, math.log(1e-3), math.log(1e-1)),
        's5_b_re': nrm((Ld, S5_GROUPS, S5_STATE, S5_GROUP_CH), (2 * S5_GROUP_CH) ** -0.5),
        's5_b_im': nrm((Ld, S5_GROUPS, S5_STATE, S5_GROUP_CH), (2 * S5_GROUP_CH) ** -0.5),
        's5_c_re': nrm((Ld, S5_GROUPS, S5_GROUP_CH, S5_STATE), (2 * S5_STATE) ** -0.5),
        's5_c_im': nrm((Ld, S5_GROUPS, S5_GROUP_CH, S5_STATE), (2 * S5_STATE) ** -0.5),
        's5_d': nrm((Ld, S5_WIDTH), 1.0),
        'w_glu': nrm((Ld, S5_WIDTH, S5_WIDTH), S5_WIDTH ** -0.5),
        'b_glu': nrm((Ld, S5_WIDTH), 0.02),
        'w_out': nrm((Ld, D_MODEL, D_MODEL), D_MODEL ** -0.5),
        'norm2': 1.0 + nrm((Ld, D_MODEL), 0.02),
        'peer_wq': nrm((Ld, D_MODEL, PEER_HEADS * PEER_DQ), D_MODEL ** -0.5),
        'peer_keys': nrm((Ld, PEER_HEADS, 2, PEER_KEYS, PEER_DQ // 2), (PEER_DQ // 2) ** -0.5),
        'peer_u': nrm((Ld, PEER_EXPERTS, D_MODEL), D_MODEL ** -0.5),
        'peer_v': nrm((Ld, PEER_EXPERTS, D_MODEL), PEER_HEADS ** -0.5),
        'final_norm': 1.0 + nrm((D_MODEL,), 0.02),
    }


def reference(x_prompt, x_sample, state_gla, state_s5_re, state_s5_im, norm1, w_in, w_a2, b_a2,
              gla_norm, s5_lam_re, s5_lam_im, s5_log_dt, s5_b_re, s5_b_im, s5_c_re, s5_c_im, s5_d,
              w_glu, b_glu, w_out, norm2, peer_wq, peer_keys, peer_u, peer_v, final_norm):
    xp, xs = x_prompt, x_sample
    gp, rp, ip, gs, rs, is_ = [], [], [], [], [], []
    for l in range(DEPTH):
        params = (norm1[l], w_in[l], w_a2[l], b_a2[l], gla_norm[l], s5_lam_re[l], s5_lam_im[l],
                  s5_log_dt[l], s5_b_re[l], s5_b_im[l], s5_c_re[l], s5_c_im[l], s5_d[l],
                  w_glu[l], b_glu[l], w_out[l], norm2[l], peer_wq[l], peer_keys[l], peer_u[l], peer_v[l])
        zg = jnp.zeros((xp.shape[0], GLA_HEADS, GLA_DK, GLA_DV), xp.dtype)
        zs = jnp.zeros((xp.shape[0], S5_GROUPS, S5_STATE), xp.dtype)
        xp, g_new, r_new, i_new = decoder_layer(xp, zg, zs, zs, *params)
        gp.append(g_new); rp.append(r_new); ip.append(i_new)
        xs, g_new, r_new, i_new = decoder_layer(xs, state_gla[l], state_s5_re[l], state_s5_im[l], *params)
        gs.append(g_new); rs.append(r_new); is_.append(i_new)
    y_prompt = rmsnorm(xp, final_norm)
    y_sample = rmsnorm(xs, final_norm)
    return (y_prompt, y_sample, jnp.stack(gp), jnp.stack(rp), jnp.stack(ip),
            jnp.stack(gs), jnp.stack(rs), jnp.stack(is_))
```

```python
import functools

import jax
import jax.numpy as jnp
from jax import lax
from jax.experimental import pallas as pl
from jax.experimental.pallas import tpu as pltpu

F32 = jnp.float32
BF16 = jnp.bfloat16

D_MODEL = 1024
GLA_HEADS = 4
GLA_DK = 64
GLA_DV = 128
GLA_QK = GLA_HEADS * GLA_DK
GLA_WIDTH = GLA_HEADS * GLA_DV
GLA_RANK = 16
GLA_TAU = 16.0
GLA_CHUNK = 64
S5_WIDTH = 512
S5_GROUP_CH = 16
S5_GROUPS = 32
S5_STATE = 64
S5_LANES = S5_GROUPS * S5_STATE
S5_NBLK = 4
S5_BLK_CH = S5_WIDTH // S5_NBLK
S5_BLK_ST = S5_LANES // S5_NBLK
PEER_KEYS = 128
PEER_EXPERTS = PEER_KEYS * PEER_KEYS
PEER_HEADS = 8
PEER_DQ = 256
PEER_TOPK = 16
EPS = 1e-6

P_Q, P_K, P_V, P_G, P_U, P_LA = 0, 256, 512, 1024, 1536, 2048
P_COLS = 2304
W_IN_COLS = 2176

TOK_TILE = 256
PEER_TOK_TILE = 512
PEER_EXP_TILE = 512
SAMPLE_SEQ_BLOCK = 16
VMEM_LIMIT = 56 * 1024 * 1024


def _split(x):
    hi = x.astype(BF16)
    lo = (x - hi.astype(F32)).astype(BF16)
    return hi, lo


def _split3(x):
    a = x.astype(BF16)
    r = x - a.astype(F32)
    b = r.astype(BF16)
    c = (r - b.astype(F32)).astype(BF16)
    return a, b, c


def _dot(a, b):
    return jnp.dot(a, b, preferred_element_type=F32)


def _dot_nt(a, b):
    return lax.dot_general(a, b, (((1,), (1,)), ((), ())), preferred_element_type=F32)


def _dot_tn(a, b):
    return lax.dot_general(a, b, (((0,), (0,)), ((), ())), preferred_element_type=F32)


def _dot3(a, b_hi, b_lo):
    a_hi, a_lo = _split(a)
    return _dot(a_hi, b_hi) + _dot(a_lo, b_hi) + _dot(a_hi, b_lo)


def _dot_exact01(m01, x):
    a, b, c = _split3(x)
    return _dot(m01, a) + _dot(m01, b) + _dot(m01, c)


def _rms(x):
    return x * lax.rsqrt(jnp.mean(x * x, axis=-1, keepdims=True) + EPS)


def _params(sem):
    return pltpu.CompilerParams(dimension_semantics=sem, vmem_limit_bytes=VMEM_LIMIT)


def _const_spec(shape):
    n = len(shape)
    return pl.BlockSpec(shape, lambda *_: (0,) * n)


def _s5prep_kernel(lr_ref, li_ref, ldt_ref, bret_ref, bimt_ref,
                   pwre_ref, pwim_ref, bbre_ref, bbim_ref):
    lr = lr_ref[...]
    li = li_ref[...]
    dt = jnp.exp(ldt_ref[...])
    mag = jnp.exp(lr * dt)
    abr = mag * jnp.cos(li * dt)
    abi = mag * jnp.sin(li * dt)
    den = lr * lr + li * li
    nr = abr - 1.0
    ni = abi
    fr = (nr * lr + ni * li) / den
    fi = (ni * lr - nr * li) / den
    bret = bret_ref[...]
    bimt = bimt_ref[...]
    bbre_ref[...] = fr[:, None, :] * bret - fi[:, None, :] * bimt
    bbim_ref[...] = fr[:, None, :] * bimt + fi[:, None, :] * bret
    pr, pi = abr, abi
    for i in range(8):
        pwre_ref[i] = pr
        pwim_ref[i] = pi
        pr, pi = pr * abr - pi * abi, pr * abi + pi * abr


def _s5_prep(lam_re, lam_im, log_dt, b_re, b_im):
    g, p = lam_re.shape
    ch = b_re.shape[-1]
    bret = jnp.transpose(b_re, (0, 2, 1))
    bimt = jnp.transpose(b_im, (0, 2, 1))
    out = pl.pallas_call(
        _s5prep_kernel,
        out_shape=(jax.ShapeDtypeStruct((8, g, p), F32), jax.ShapeDtypeStruct((8, g, p), F32),
                   jax.ShapeDtypeStruct((g, ch, p), F32), jax.ShapeDtypeStruct((g, ch, p), F32)),
        name="s5_prep",
    )(lam_re, lam_im, log_dt.reshape(g, 1), bret, bimt)
    return out


def _inproj_kernel(x_ref, n1_ref, whi_ref, wlo_ref, a2hi_ref, a2lo_ref, ba2_ref, p_ref):
    h = _rms(x_ref[...]) * n1_ref[...]
    p = _dot3(h, whi_ref[...], wlo_ref[...])
    alr = p[:, P_LA:W_IN_COLS]
    z = _dot3(alr, a2hi_ref[...], a2lo_ref[...]) + ba2_ref[...]
    log_sig = jnp.minimum(z, 0.0) - jnp.log1p(jnp.exp(-jnp.abs(z)))
    p_ref[:, 0:P_LA] = p[:, 0:P_LA]
    p_ref[:, P_LA:P_COLS] = log_sig * (1.0 / GLA_TAU)


def _in_proj(x2, n1, whi, wlo, a2hi, a2lo, ba2):
    rows = x2.shape[0]
    return pl.pallas_call(
        _inproj_kernel,
        grid=(rows // TOK_TILE,),
        in_specs=[pl.BlockSpec((TOK_TILE, D_MODEL), lambda i: (i, 0)),
                  _const_spec(n1.shape), _const_spec(whi.shape), _const_spec(wlo.shape),
                  _const_spec(a2hi.shape), _const_spec(a2lo.shape), _const_spec(ba2.shape)],
        out_specs=pl.BlockSpec((TOK_TILE, P_COLS), lambda i: (i, 0)),
        out_shape=jax.ShapeDtypeStruct((rows, P_COLS), F32),
        compiler_params=_params(("parallel",)),
        name="in_proj",
    )(x2, n1, whi, wlo, a2hi, a2lo, ba2)


def _cumsum_rows(mask01, la):
    return _dot_exact01(mask01.astype(BF16), la)


def _gla_prompt_kernel(q_ref, k_ref, v_ref, g_ref, la_ref, e_ref, gn_ref,
                       o_ref, sfin_ref, st_ref, b_s, k_s, *, n_chunks):
    c = pl.program_id(1)
    C = q_ref.shape[0]

    @pl.when(c == 0)
    def _():
        st_ref[...] = jnp.zeros_like(st_ref)

    la = la_ref[...]
    row = lax.broadcasted_iota(jnp.int32, (C, C), 0)
    col = lax.broadcasted_iota(jnp.int32, (C, C), 1)
    b = _cumsum_rows(col <= row, la)
    q = q_ref[...] * (GLA_DK ** -0.5)
    k = k_ref[...]
    v = v_ref[...]
    b_s[...] = b
    k_s[...] = k
    blast = b[C - 1:C, :]
    qe = (q * jnp.exp(b)).astype(BF16)
    vb = v.astype(BF16)

    def body(j, acc):
        bj = b_s[pl.ds(j, 1), :]
        kj = k_s[pl.ds(j, 1), :]
        t = q * kj * jnp.exp(jnp.minimum(b - bj, 0.0))
        return acc + _dot(t.astype(BF16), e_ref[j])

    acc = lax.fori_loop(0, C, body, jnp.zeros((C, GLA_HEADS * C), F32))
    rr = lax.broadcasted_iota(jnp.int32, acc.shape, 0)
    cc = lax.broadcasted_iota(jnp.int32, acc.shape, 1)
    scores = jnp.where((cc % C) <= rr, acc, 0.0).astype(BF16)

    kd = k * jnp.exp(blast - b)
    outs = []
    for h in range(GLA_HEADS):
        ks = slice(h * GLA_DK, (h + 1) * GLA_DK)
        vs = slice(h * GLA_DV, (h + 1) * GLA_DV)
        st = st_ref[h]
        o_h = _dot_nt(qe[:, ks], st.astype(BF16)) + _dot(scores[:, h * C:(h + 1) * C], vb[:, vs])
        outs.append(_rms(o_h))
        v_hi, v_lo = _split(v[:, vs])
        k_hi, k_lo = _split(kd[:, ks])
        upd = _dot_tn(v_hi, k_hi) + _dot_tn(v_lo, k_hi) + _dot_tn(v_hi, k_lo)
        st_ref[h] = jnp.exp(blast[:, ks]) * st + upd
    g = g_ref[...]
    o_ref[...] = jnp.concatenate(outs, axis=-1) * gn_ref[...] * (g * jax.nn.sigmoid(g))

    @pl.when(c == n_chunks - 1)
    def _():
        for h in range(GLA_HEADS):
            sfin_ref[0, h] = st_ref[h].T


def _gla_prompt(p2, n_seq, seq_len, e_mat, gn):
    C = GLA_CHUNK
    nch = seq_len // C
    rows = n_seq * seq_len

    def tok(width, colblk):
        return pl.BlockSpec((C, width), lambda b, c: (b * nch + c, colblk))

    return pl.pallas_call(
        functools.partial(_gla_prompt_kernel, n_chunks=nch),
        grid=(n_seq, nch),
        in_specs=[tok(GLA_QK, P_Q // GLA_QK), tok(GLA_QK, P_K // GLA_QK),
                  tok(GLA_WIDTH, P_V // GLA_WIDTH), tok(GLA_WIDTH, P_G // GLA_WIDTH),
                  tok(GLA_QK, P_LA // GLA_QK),
                  _const_spec(e_mat.shape), _const_spec(gn.shape)],
        out_specs=[pl.BlockSpec((C, GLA_WIDTH), lambda b, c: (b * nch + c, 0)),
                   pl.BlockSpec((1, GLA_HEADS, GLA_DK, GLA_DV), lambda b, c: (b, 0, 0, 0))],
        out_shape=(jax.ShapeDtypeStruct((rows, GLA_WIDTH), F32),
                   jax.ShapeDtypeStruct((n_seq, GLA_HEADS, GLA_DK, GLA_DV), F32)),
        scratch_shapes=[pltpu.VMEM((GLA_HEADS, GLA_DV, GLA_DK), F32),
                        pltpu.VMEM((C, GLA_QK), F32), pltpu.VMEM((C, GLA_QK), F32)],
        compiler_params=_params(("parallel", "arbitrary")),
        name="gla_prompt",
    )(p2, p2, p2, p2, p2, e_mat, gn)


def _gla_sample_kernel(q_ref, k_ref, v_ref, g_ref, la_ref, s0_ref, e2_ref, rep_ref, gn_ref,
                       o_ref, snew_ref, *, seq_len):
    R = q_ref.shape[0]
    nseq = R // seq_len
    SD = nseq * GLA_DK
    la = la_ref[...]
    row = lax.broadcasted_iota(jnp.int32, (R, R), 0)
    col = lax.broadcasted_iota(jnp.int32, (R, R), 1)
    same = (col // seq_len) == (row // seq_len)
    b = _cumsum_rows(same & (col <= row), la)
    btot = _cumsum_rows(same, la)
    q = q_ref[...] * (GLA_DK ** -0.5)
    k = k_ref[...]
    v = v_ref[...]
    rmod = lax.broadcasted_iota(jnp.int32, (R, GLA_QK), 0) % seq_len

    o = jnp.zeros((R, GLA_WIDTH), F32)
    for d in range(seq_len):
        ks_, bs_, vs_ = (k, b, v) if d == 0 else (pltpu.roll(k, d, 0), pltpu.roll(b, d, 0),
                                                  pltpu.roll(v, d, 0))
        m = q * ks_ * jnp.exp(jnp.minimum(b - bs_, 0.0))
        m = jnp.where(rmod >= d, m, 0.0)
        o = o + _dot(m.astype(BF16), e2_ref[...]) * vs_

    xr = lax.broadcasted_iota(jnp.int32, (R, GLA_HEADS * SD), 0) // seq_len
    xc = (lax.broadcasted_iota(jnp.int32, (R, GLA_HEADS * SD), 1) % SD) // GLA_DK
    own = xr == xc
    rep = rep_ref[...]

    def expand(x_bf16):
        return jnp.where(own, _dot(x_bf16, rep), 0.0).astype(BF16)

    qx = expand((q * jnp.exp(b)).astype(BF16))
    kd_hi, kd_lo = _split(k * jnp.exp(btot - b))
    kx_hi, kx_lo = expand(kd_hi), expand(kd_lo)
    ea, eb, ec = _split3(jnp.exp(btot))
    ax = (expand(ea), expand(eb), expand(ec))
    last = (lax.broadcasted_iota(jnp.int32, (R, GLA_DV), 0) % seq_len == seq_len - 1).astype(BF16)

    outs = []
    for h in range(GLA_HEADS):
        xs = slice(h * SD, (h + 1) * SD)
        vs = slice(h * GLA_DV, (h + 1) * GLA_DV)
        s0 = s0_ref[:, h].reshape(SD, GLA_DV)
        o_h = o[:, vs] + _dot(qx[:, xs], s0.astype(BF16))
        outs.append(_rms(o_h))
        v_hi, v_lo = _split(v[:, vs])
        upd = _dot_tn(kx_hi[:, xs], v_hi) + _dot_tn(kx_lo[:, xs], v_hi) + _dot_tn(kx_hi[:, xs], v_lo)
        decay = _dot_tn(ax[0][:, xs], last) + _dot_tn(ax[1][:, xs], last) + _dot_tn(ax[2][:, xs], last)
        snew_ref[:, h] = (decay * s0 + upd).reshape(nseq, GLA_DK, GLA_DV)
    g = g_ref[...]
    o_ref[...] = jnp.concatenate(outs, axis=-1) * gn_ref[...] * (g * jax.nn.sigmoid(g))


def _gla_sample(p2, s0, seq_len, e2, rep, gn):
    n_seq = s0.shape[0]
    R = SAMPLE_SEQ_BLOCK * seq_len
    nblk = n_seq // SAMPLE_SEQ_BLOCK

    def tok(width, colblk):
        return pl.BlockSpec((R, width), lambda i: (i, colblk))

    st_spec = pl.BlockSpec((SAMPLE_SEQ_BLOCK, GLA_HEADS, GLA_DK, GLA_DV), lambda i: (i, 0, 0, 0))
    return pl.pallas_call(
        functools.partial(_gla_sample_kernel, seq_len=seq_len),
        grid=(nblk,),
        in_specs=[tok(GLA_QK, P_Q // GLA_QK), tok(GLA_QK, P_K // GLA_QK),
                  tok(GLA_WIDTH, P_V // GLA_WIDTH), tok(GLA_WIDTH, P_G // GLA_WIDTH),
                  tok(GLA_QK, P_LA // GLA_QK), st_spec,
                  _const_spec(e2.shape), _const_spec(rep.shape), _const_spec(gn.shape)],
        out_specs=[pl.BlockSpec((R, GLA_WIDTH), lambda i: (i, 0)), st_spec],
        out_shape=(jax.ShapeDtypeStruct((n_seq * seq_len, GLA_WIDTH), F32),
                   jax.ShapeDtypeStruct(s0.shape, F32)),
        compiler_params=_params(("parallel",)),
        name="gla_sample",
    )(p2, p2, p2, p2, p2, s0, e2, rep, gn)


def _s5_local_scan(bur, bui, pwre_ref, pwim_ref, lanes, group):
    rmod = lax.broadcasted_iota(jnp.int32, bur.shape, 0) % group
    s = 1
    while s < group:
        ar = pwre_ref[s - 1:s, lanes]
        ai = pwim_ref[s - 1:s, lanes]
        sr = jnp.where(rmod >= s, pltpu.roll(bur, s, 0), 0.0)
        si = jnp.where(rmod >= s, pltpu.roll(bui, s, 0), 0.0)
        bur, bui = bur + ar * sr - ai * si, bui + ar * si + ai * sr
        s *= 2
    return bur, bui


def _s5_tail(ys, u, d_ref, wglu_ref, bglu_ref):
    y = jnp.concatenate(ys, axis=-1) + d_ref[...] * u
    z = jax.nn.gelu(y)
    return z * jax.nn.sigmoid(_dot(z.astype(BF16), wglu_ref[...]) + bglu_ref[...])


def _s5_prompt_kernel(u_ref, brh_ref, brl_ref, bih_ref, bil_ref, cre_ref, cim_ref,
                      pwre_ref, pwim_ref, d_ref, wglu_ref, bglu_ref,
                      o_ref, stre_ref, stim_ref, hre_s, him_s, car_re, car_im):
    t = pl.program_id(1)
    TT = u_ref.shape[0]

    @pl.when(t == 0)
    def _():
        car_re[...] = jnp.zeros_like(car_re)
        car_im[...] = jnp.zeros_like(car_im)

    u = u_ref[...]
    ys = []
    for l in range(S5_NBLK):
        lanes = slice(l * S5_BLK_ST, (l + 1) * S5_BLK_ST)
        ul = u[:, l * S5_BLK_CH:(l + 1) * S5_BLK_CH]
        bur = _dot3(ul, brh_ref[l], brl_ref[l])
        bui = _dot3(ul, bih_ref[l], bil_ref[l])
        bur, bui = _s5_local_scan(bur, bui, pwre_ref, pwim_ref, lanes, 8)
        hre_s[...] = bur
        him_s[...] = bui
        p8r = pwre_ref[:, lanes]
        p8i = pwim_ref[:, lanes]

        def grp(r, carry):
            cr, ci = carry
            off = pl.multiple_of(r * 8, 8)
            xr = hre_s[pl.ds(off, 8), :] + p8r * cr - p8i * ci
            xi = him_s[pl.ds(off, 8), :] + p8r * ci + p8i * cr
            hre_s[pl.ds(off, 8), :] = xr
            him_s[pl.ds(off, 8), :] = xi
            return (jnp.broadcast_to(xr[7:8], xr.shape), jnp.broadcast_to(xi[7:8], xi.shape))

        cr0 = jnp.broadcast_to(car_re[:, lanes], (8, S5_BLK_ST))
        ci0 = jnp.broadcast_to(car_im[:, lanes], (8, S5_BLK_ST))
        cr, ci = lax.fori_loop(0, TT // 8, grp, (cr0, ci0))
        car_re[:, lanes] = cr[0:1]
        car_im[:, lanes] = ci[0:1]
        ys.append(_dot(hre_s[...].astype(BF16), cre_ref[l]) - _dot(him_s[...].astype(BF16), cim_ref[l]))
    o_ref[...] = _s5_tail(ys, u, d_ref, wglu_ref, bglu_ref)
    stre_ref[0] = car_re[...]
    stim_ref[0] = car_im[...]


def _s5_sample_kernel(u_ref, s0re_ref, s0im_ref, brh_ref, brl_ref, bih_ref, bil_ref, cre_ref, cim_ref,
                      pwre_ref, pwim_ref, d_ref, wglu_ref, bglu_ref,
                      o_ref, hre_ref, him_ref, *, seq_len):
    u = u_ref[...]
    ys = []
    for l in range(S5_NBLK):
        lanes = slice(l * S5_BLK_ST, (l + 1) * S5_BLK_ST)
        ul = u[:, l * S5_BLK_CH:(l + 1) * S5_BLK_CH]
        ar = pwre_ref[0:1, lanes]
        ai = pwim_ref[0:1, lanes]
        sr = s0re_ref[:, lanes]
        si = s0im_ref[:, lanes]
        bur = _dot3(ul, brh_ref[l], brl_ref[l]) + (ar * sr - ai * si)
        bui = _dot3(ul, bih_ref[l], bil_ref[l]) + (ar * si + ai * sr)
        bur, bui = _s5_local_scan(bur, bui, pwre_ref, pwim_ref, lanes, seq_len)
        hre_ref[:, lanes] = bur
        him_ref[:, lanes] = bui
        ys.append(_dot(bur.astype(BF16), cre_ref[l]) - _dot(bui.astype(BF16), cim_ref[l]))
    o_ref[...] = _s5_tail(ys, u, d_ref, wglu_ref, bglu_ref)


def _s5_weight_specs(ws):
    return [_const_spec(w.shape) for w in ws]


def _s5_prompt(p2, n_seq, seq_len, ws):
    TT = TOK_TILE
    nt = seq_len // TT
    rows = n_seq * seq_len
    st_spec = pl.BlockSpec((1, 1, S5_LANES), lambda b, t: (b, 0, 0))
    return pl.pallas_call(
        _s5_prompt_kernel,
        grid=(n_seq, nt),
        in_specs=[pl.BlockSpec((TT, S5_WIDTH), lambda b, t: (b * nt + t, P_U // S5_WIDTH))]
        + _s5_weight_specs(ws),
        out_specs=[pl.BlockSpec((TT, S5_WIDTH), lambda b, t: (b * nt + t, 0)), st_spec, st_spec],
        out_shape=(jax.ShapeDtypeStruct((rows, S5_WIDTH), F32),
                   jax.ShapeDtypeStruct((n_seq, 1, S5_LANES), F32),
                   jax.ShapeDtypeStruct((n_seq, 1, S5_LANES), F32)),
        scratch_shapes=[pltpu.VMEM((TT, S5_BLK_ST), F32), pltpu.VMEM((TT, S5_BLK_ST), F32),
                        pltpu.VMEM((1, S5_LANES), F32), pltpu.VMEM((1, S5_LANES), F32)],
        compiler_params=_params(("parallel", "arbitrary")),
        name="s5_prompt",
    )(p2, *ws)


def _s5_sample(p2, s0re_rows, s0im_rows, seq_len, ws):
    rows = s0re_rows.shape[0]
    TT = TOK_TILE
    row_spec = pl.BlockSpec((TT, S5_LANES), lambda i: (i, 0))
    return pl.pallas_call(
        functools.partial(_s5_sample_kernel, seq_len=seq_len),
        grid=(rows // TT,),
        in_specs=[pl.BlockSpec((TT, S5_WIDTH), lambda i: (i, P_U // S5_WIDTH)), row_spec, row_spec]
        + _s5_weight_specs(ws),
        out_specs=[pl.BlockSpec((TT, S5_WIDTH), lambda i: (i, 0)), row_spec, row_spec],
        out_shape=(jax.ShapeDtypeStruct((rows, S5_WIDTH), F32),
                   jax.ShapeDtypeStruct((rows, S5_LANES), F32),
                   jax.ShapeDtypeStruct((rows, S5_LANES), F32)),
        compiler_params=_params(("parallel",)),
        name="s5_sample",
    )(p2, s0re_rows, s0im_rows, *ws)


def _outproj_kernel(x_ref, og_ref, os_ref, wth_ref, wtl_ref, wbh_ref, wbl_ref, n2_ref,
                    x1_ref, h2_ref):
    x1 = (x_ref[...] + _dot3(og_ref[...], wth_ref[...], wtl_ref[...])
          + _dot3(os_ref[...], wbh_ref[...], wbl_ref[...]))
    x1_ref[...] = x1
    h2_ref[...] = _rms(x1) * n2_ref[...]


def _out_proj(x2, og, osx, wth, wtl, wbh, wbl, n2):
    rows = x2.shape[0]
    big = pl.BlockSpec((TOK_TILE, D_MODEL), lambda i: (i, 0))
    half = pl.BlockSpec((TOK_TILE, GLA_WIDTH), lambda i: (i, 0))
    return pl.pallas_call(
        _outproj_kernel,
        grid=(rows // TOK_TILE,),
        in_specs=[big, half, half] + [_const_spec(w.shape) for w in (wth, wtl, wbh, wbl, n2)],
        out_specs=[big, big],
        out_shape=(jax.ShapeDtypeStruct((rows, D_MODEL), F32),) * 2,
        compiler_params=_params(("parallel",)),
        name="out_proj",
    )(x2, og, osx, wth, wtl, wbh, wbl, n2)


def _extract_top(work, n, store=None):
    first = None
    m = None
    for r in range(n):
        m = jnp.max(work, axis=0, keepdims=True)
        if r == 0:
            first = m
        if store is not None:
            store(r, m)
        if r < n - 1:
            work = jnp.where(work == m, -jnp.inf, work)
    return first, m


def _route_kernel(h2_ref, wqh_ref, wql_ref, kh_ref, kl_ref, st_ref, stats_ref, top_s):
    TM = h2_ref.shape[0]
    qp = _dot3(h2_ref[...], wqh_ref[...], wql_ref[...])
    for hp in range(2 * PEER_HEADS):
        q_hi, q_lo = _split(qp[:, hp * PEER_KEYS:(hp + 1) * PEER_KEYS])
        s = _dot_nt(kh_ref[hp], q_hi) + _dot_nt(kl_ref[hp], q_hi) + _dot_nt(kh_ref[hp], q_lo)
        st_ref[hp] = s

        def store(r, m, hp=hp):
            top_s[hp, r:r + 1, :] = m

        _extract_top(s, PEER_TOPK, store)
    K = PEER_TOPK
    for h in range(PEER_HEADS):
        v1 = top_s[2 * h]
        v2 = top_s[2 * h + 1]
        cands = [v1 + v2[0:1]] + [v1[0:8] + v2[b:b + 1] for b in range(1, 8)] + [v2[8:K] + v1[0:1]]
        cand = jnp.concatenate(cands, axis=0)
        mx, thr = _extract_top(cand, PEER_TOPK)
        z = jnp.sum(jnp.where(cand >= thr, jnp.exp(cand - mx), 0.0), axis=0, keepdims=True)
        stats_ref[h] = jnp.concatenate([v1[0:1], v2[0:1], thr, 1.0 / z, jnp.zeros((4, TM), F32)], axis=0)


def _route(h2, wqh, wql, kh, kl):
    rows = h2.shape[0]
    TM = TOK_TILE
    return pl.pallas_call(
        _route_kernel,
        grid=(rows // TM,),
        in_specs=[pl.BlockSpec((TM, D_MODEL), lambda i: (i, 0)),
                  _const_spec(wqh.shape), _const_spec(wql.shape),
                  _const_spec(kh.shape), _const_spec(kl.shape)],
        out_specs=[pl.BlockSpec((2 * PEER_HEADS, PEER_KEYS, TM), lambda i: (0, 0, i)),
                   pl.BlockSpec((PEER_HEADS, 8, TM), lambda i: (0, 0, i))],
        out_shape=(jax.ShapeDtypeStruct((2 * PEER_HEADS, PEER_KEYS, rows), F32),
                   jax.ShapeDtypeStruct((PEER_HEADS, 8, rows), F32)),
        scratch_shapes=[pltpu.VMEM((2 * PEER_HEADS, PEER_TOPK, TM), F32)],
        compiler_params=_params(("parallel",)),
        name="peer_route",
    )(h2, wqh, wql, kh, kl)


def _peer_kernel(h2_ref, x1_ref, st_ref, stats_ref, u_ref, vt_ref, fn_ref, y_ref,
                 acc, h2b, e1, e2, wact, *, n_eblk):
    e = pl.program_id(1)
    EB = u_ref.shape[0]
    sub = EB // PEER_KEYS

    @pl.when(e == 0)
    def _():
        acc[...] = jnp.zeros_like(acc)
        h2b[...] = h2_ref[...].astype(BF16)
        for h in range(PEER_HEADS):
            st = stats_ref[h]
            e1[h] = jnp.exp(st_ref[2 * h] - st[0:1])
            e2[h] = jnp.exp(st_ref[2 * h + 1] - st[1:2]) * st[3:4]

    act = _dot_nt(u_ref[...], h2b[...])
    for a in range(sub):
        i1 = e * sub + a
        w = jnp.zeros((PEER_KEYS, act.shape[1]), F32)
        for h in range(PEER_HEADS):
            s1 = st_ref[2 * h, pl.ds(i1, 1), :]
            g1 = e1[h, pl.ds(i1, 1), :]
            thr = stats_ref[h, 2:3, :]
            w = w + jnp.where((s1 + st_ref[2 * h + 1]) >= thr, g1 * e2[h], 0.0)
        ga = jax.nn.gelu(act[a * PEER_KEYS:(a + 1) * PEER_KEYS])
        wact[a * PEER_KEYS:(a + 1) * PEER_KEYS, :] = (w * ga).astype(BF16)
    acc[...] += _dot(vt_ref[...], wact[...])

    @pl.when(e == n_eblk - 1)
    def _():
        out = x1_ref[...] + acc[...].T
        y_ref[...] = _rms(out) * fn_ref[...]


def _peer(h2, x1, s_t, stats, u_bf, vt_bf, fn):
    rows = h2.shape[0]
    TM = PEER_TOK_TILE
    EB = PEER_EXP_TILE
    n_eblk = PEER_EXPERTS // EB
    big = pl.BlockSpec((TM, D_MODEL), lambda i, e: (i, 0))
    return pl.pallas_call(
        functools.partial(_peer_kernel, n_eblk=n_eblk),
        grid=(rows // TM, n_eblk),
        in_specs=[big, big,
                  pl.BlockSpec((2 * PEER_HEADS, PEER_KEYS, TM), lambda i, e: (0, 0, i)),
                  pl.BlockSpec((PEER_HEADS, 8, TM), lambda i, e: (0, 0, i)),
                  pl.BlockSpec((EB, D_MODEL), lambda i, e: (e, 0)),
                  pl.BlockSpec((D_MODEL, EB), lambda i, e: (0, e)),
                  _const_spec(fn.shape)],
        out_specs=big,
        out_shape=jax.ShapeDtypeStruct((rows, D_MODEL), F32),
        scratch_shapes=[pltpu.VMEM((D_MODEL, TM), F32), pltpu.VMEM((TM, D_MODEL), BF16),
                        pltpu.VMEM((PEER_HEADS, PEER_KEYS, TM), F32),
                        pltpu.VMEM((PEER_HEADS, PEER_KEYS, TM), F32),
                        pltpu.VMEM((EB, TM), BF16)],
        compiler_params=_params(("parallel", "arbitrary")),
        name="peer_experts",
    )(h2, x1, s_t, stats, u_bf, vt_bf, fn)


def _block_diag(w, eye):
    n, g, a, b = w.shape
    return jnp.einsum('lgab,gh->lgahb', w, eye).reshape(n, g * a, g * b)


def _gla_constants(seq_len_sample):
    lane_head = jnp.arange(GLA_QK) // GLA_DK
    col = jnp.arange(GLA_HEADS * GLA_CHUNK)
    e_mat = ((lane_head[None, :, None] == (col // GLA_CHUNK)[None, None, :])
             & ((col % GLA_CHUNK)[None, None, :] == jnp.arange(GLA_CHUNK)[:, None, None])).astype(BF16)
    e2 = (lane_head[:, None] == (jnp.arange(GLA_WIDTH) // GLA_DV)[None, :]).astype(BF16)
    sd = SAMPLE_SEQ_BLOCK * GLA_DK
    xc = jnp.arange(GLA_HEADS * sd)
    rep = ((lane_head[:, None] == (xc // sd)[None, :])
           & ((jnp.arange(GLA_QK) % GLA_DK)[:, None] == (xc % GLA_DK)[None, :])).astype(BF16)
    return e_mat, e2, rep


def kernel(x_prompt, x_sample, state_gla, state_s5_re, state_s5_im, norm1, w_in, w_a2, b_a2, gla_norm, s5_lam_re, s5_lam_im, s5_log_dt, s5_b_re, s5_b_im, s5_c_re, s5_c_im, s5_d, w_glu, b_glu, w_out, norm2, peer_wq, peer_keys, peer_u, peer_v, final_norm):
    depth = norm1.shape[0]
    assert depth == 1, "single-layer trunk"
    n_p, len_p, _ = x_prompt.shape
    n_s, len_s, _ = x_sample.shape
    l = 0

    w = w_in[l]
    w_re = jnp.concatenate([w[:, 0:1536], w[:, 1552:2064], w[:, 1536:1552],
                            jnp.zeros((D_MODEL, W_IN_COLS - 2064), F32)], axis=1)
    whi, wlo = _split(w_re)
    a2 = jnp.concatenate([w_a2[l], jnp.zeros((W_IN_COLS - P_LA - GLA_RANK, GLA_QK), F32)], axis=0)
    a2hi, a2lo = _split(a2)
    ba2 = b_a2[l].reshape(1, GLA_QK)
    n1 = norm1[l].reshape(1, D_MODEL)
    gn = gla_norm[l].reshape(1, GLA_WIDTH)
    e_mat, e2, rep = _gla_constants(len_s)

    pwre, pwim, bbre, bbim = _s5_prep(s5_lam_re[l], s5_lam_im[l], s5_log_dt[l], s5_b_re[l], s5_b_im[l])
    eye = jnp.eye(8, dtype=F32)
    blk = lambda t: t.reshape(S5_NBLK, 8, S5_GROUP_CH, S5_STATE)
    brh, brl = _split(_block_diag(blk(bbre), eye))
    bih, bil = _split(_block_diag(blk(bbim), eye))
    cre = _block_diag(jnp.swapaxes(blk(s5_c_re[l]), 2, 3), eye).astype(BF16)
    cim = _block_diag(jnp.swapaxes(blk(s5_c_im[l]), 2, 3), eye).astype(BF16)
    s5w = (brh, brl, bih, bil, cre, cim, pwre.reshape(8, S5_LANES), pwim.reshape(8, S5_LANES),
           s5_d[l].reshape(1, S5_WIDTH), w_glu[l].astype(BF16), b_glu[l].reshape(1, S5_WIDTH))

    wth, wtl = _split(w_out[l][:GLA_WIDTH])
    wbh, wbl = _split(w_out[l][GLA_WIDTH:])
    n2 = norm2[l].reshape(1, D_MODEL)
    wqh, wql = _split(peer_wq[l])
    kh, kl = _split(peer_keys[l].reshape(2 * PEER_HEADS, PEER_KEYS, PEER_DQ // 2))
    u_bf = peer_u[l].astype(BF16)
    vt_bf = peer_v[l].astype(BF16).T
    fn = final_norm.reshape(1, D_MODEL)

    def tail(x2, og, osx):
        x1, h2 = _out_proj(x2, og, osx, wth, wtl, wbh, wbl, n2)
        s_t, stats = _route(h2, wqh, wql, kh, kl)
        return _peer(h2, x1, s_t, stats, u_bf, vt_bf, fn)

    xp = x_prompt.reshape(n_p * len_p, D_MODEL)
    pp = _in_proj(xp, n1, whi, wlo, a2hi, a2lo, ba2)
    og_p, gla_p = _gla_prompt(pp, n_p, len_p, e_mat, gn)
    os_p, sre_p, sim_p = _s5_prompt(pp, n_p, len_p, s5w)
    y_p = tail(xp, og_p, os_p).reshape(n_p, len_p, D_MODEL)

    xs = x_sample.reshape(n_s * len_s, D_MODEL)
    ps = _in_proj(xs, n1, whi, wlo, a2hi, a2lo, ba2)
    og_s, gla_s = _gla_sample(ps, state_gla[l], len_s, e2, rep, gn)
    first_row = lambda s: jnp.pad(s.reshape(n_s, 1, S5_LANES), ((0, 0), (0, len_s - 1), (0, 0))
                                  ).reshape(n_s * len_s, S5_LANES)
    os_s, hre_s, him_s = _s5_sample(ps, first_row(state_s5_re[l]), first_row(state_s5_im[l]), len_s, s5w)
    y_s = tail(xs, og_s, os_s).reshape(n_s, len_s, D_MODEL)
    last_row = lambda hs: hs.reshape(n_s, len_s, S5_GROUPS, S5_STATE)[:, len_s - 1]

    st = lambda a: a.reshape(1, n_p, S5_GROUPS, S5_STATE)
    return (y_p, y_s, gla_p[None], st(sre_p), st(sim_p),
            gla_s[None], last_row(hre_s)[None], last_row(him_s)[None])
```

```python
import functools

import jax
import jax.numpy as jnp
from jax import lax
from jax.experimental import pallas as pl
from jax.experimental.pallas import tpu as pltpu

F32 = jnp.float32
BF16 = jnp.bfloat16

D_MODEL = 1024
GLA_HEADS = 4
GLA_DK = 64
GLA_DV = 128
GLA_QK = GLA_HEADS * GLA_DK
GLA_WIDTH = GLA_HEADS * GLA_DV
GLA_RANK = 16
GLA_TAU = 16.0
GLA_CHUNK = 64
S5_WIDTH = 512
S5_GROUP_CH = 16
S5_GROUPS = 32
S5_STATE = 64
S5_LANES = S5_GROUPS * S5_STATE
S5_NBLK = 4
S5_BLK_CH = S5_WIDTH // S5_NBLK
S5_BLK_ST = S5_LANES // S5_NBLK
PEER_KEYS = 128
PEER_EXPERTS = PEER_KEYS * PEER_KEYS
PEER_HEADS = 8
PEER_DQ = 256
PEER_TOPK = 16
EPS = 1e-6
LANES = 128
GELU_C = 0.7978845608028654
GELU_A = 0.044715

P_Q, P_K, P_V, P_G, P_U, P_LA = 0, 256, 512, 1024, 1536, 2048
P_COLS = 2304
W_IN_COLS = 2176

TOK_TILE = 256
PEER_TOK_TILE = 512
PEER_EXP_TILE = 512
SAMPLE_SEQ_BLOCK = 16
VMEM_LIMIT = 56 * 1024 * 1024


def _split(x):
    hi = x.astype(BF16)
    lo = (x - hi.astype(F32)).astype(BF16)
    return hi, lo


def _split3(x):
    a = x.astype(BF16)
    r = x - a.astype(F32)
    b = r.astype(BF16)
    c = (r - b.astype(F32)).astype(BF16)
    return a, b, c


def _dot(a, b):
    return jnp.dot(a, b, preferred_element_type=F32)


def _dot_nt(a, b):
    return lax.dot_general(a, b, (((1,), (1,)), ((), ())), preferred_element_type=F32)


def _dot_tn(a, b):
    return lax.dot_general(a, b, (((0,), (0,)), ((), ())), preferred_element_type=F32)


def _dot3(a, b_hi, b_lo):
    a_hi, a_lo = _split(a)
    return _dot(a_hi, b_hi) + _dot(a_lo, b_hi) + _dot(a_hi, b_lo)


def _dot_exact01(m01, x):
    a, b, c = _split3(x)
    return _dot(m01, a) + _dot(m01, b) + _dot(m01, c)


def _rms(x):
    return x * lax.rsqrt(jnp.mean(x * x, axis=-1, keepdims=True) + EPS)


def _params(sem):
    return pltpu.CompilerParams(dimension_semantics=sem, vmem_limit_bytes=VMEM_LIMIT)


def _const_spec(shape):
    n = len(shape)
    return pl.BlockSpec(shape, lambda *_: (0,) * n)


def _s5prep_kernel(lr_ref, li_ref, ldt_ref, bret_ref, bimt_ref,
                   pwre_ref, pwim_ref, bbre_ref, bbim_ref):
    lr = lr_ref[...]
    li = li_ref[...]
    dt = jnp.exp(ldt_ref[...])
    mag = jnp.exp(lr * dt)
    abr = mag * jnp.cos(li * dt)
    abi = mag * jnp.sin(li * dt)
    den = lr * lr + li * li
    nr = abr - 1.0
    ni = abi
    fr = (nr * lr + ni * li) / den
    fi = (ni * lr - nr * li) / den
    bret = bret_ref[...]
    bimt = bimt_ref[...]
    bbre_ref[...] = fr[:, None, :] * bret - fi[:, None, :] * bimt
    bbim_ref[...] = fr[:, None, :] * bimt + fi[:, None, :] * bret
    pr, pi = abr, abi
    for i in range(8):
        pwre_ref[i] = pr
        pwim_ref[i] = pi
        pr, pi = pr * abr - pi * abi, pr * abi + pi * abr


def _s5_prep(lam_re, lam_im, log_dt, b_re, b_im):
    g, p = lam_re.shape
    ch = b_re.shape[-1]
    bret = jnp.transpose(b_re, (0, 2, 1))
    bimt = jnp.transpose(b_im, (0, 2, 1))
    out = pl.pallas_call(
        _s5prep_kernel,
        out_shape=(jax.ShapeDtypeStruct((8, g, p), F32), jax.ShapeDtypeStruct((8, g, p), F32),
                   jax.ShapeDtypeStruct((g, ch, p), F32), jax.ShapeDtypeStruct((g, ch, p), F32)),
        name="s5_prep",
    )(lam_re, lam_im, log_dt.reshape(g, 1), bret, bimt)
    return out


def _inproj_kernel(x_ref, n1_ref, w_ref, a2hi_ref, a2lo_ref, ba2_ref, p_ref):
    h = _rms(x_ref[...]) * n1_ref[...]
    p = _dot(h.astype(BF16), w_ref[...])
    alr = p[:, P_LA:W_IN_COLS]
    z = _dot3(alr, a2hi_ref[...], a2lo_ref[...]) + ba2_ref[...]
    log_sig = jnp.minimum(z, 0.0) - jnp.log1p(jnp.exp(-jnp.abs(z)))
    p_ref[:, 0:P_LA] = p[:, 0:P_LA]
    p_ref[:, P_LA:P_COLS] = log_sig * (1.0 / GLA_TAU)


def _in_proj(x2, n1, w_bf, a2hi, a2lo, ba2):
    rows = x2.shape[0]
    return pl.pallas_call(
        _inproj_kernel,
        grid=(rows // TOK_TILE,),
        in_specs=[pl.BlockSpec((TOK_TILE, D_MODEL), lambda i: (i, 0)),
                  _const_spec(n1.shape), _const_spec(w_bf.shape),
                  _const_spec(a2hi.shape), _const_spec(a2lo.shape), _const_spec(ba2.shape)],
        out_specs=pl.BlockSpec((TOK_TILE, P_COLS), lambda i: (i, 0)),
        out_shape=jax.ShapeDtypeStruct((rows, P_COLS), F32),
        compiler_params=_params(("parallel",)),
        name="in_proj",
    )(x2, n1, w_bf, a2hi, a2lo, ba2)


def _cumsum_rows(mask01, la):
    return _dot_exact01(mask01.astype(BF16), la)


GLA_BAND = 8


def _gla_prompt_kernel(q_ref, k_ref, v_ref, g_ref, la_ref, e2_ref, gn_ref,
                       o_ref, sfin_ref, st_ref, slab, *, n_chunks):
    c = pl.program_id(1)
    C = q_ref.shape[0]
    HC = GLA_HEADS * C

    @pl.when(c == 0)
    def _():
        st_ref[...] = jnp.zeros_like(st_ref)

    la = la_ref[...]
    row = lax.broadcasted_iota(jnp.int32, (C, C), 0)
    col = lax.broadcasted_iota(jnp.int32, (C, C), 1)
    b = _cumsum_rows(col <= row, la)
    q = q_ref[...] * (GLA_DK ** -0.5)
    k = k_ref[...]
    v = v_ref[...]
    blast = b[C - 1:C, :]

    rloc = lax.broadcasted_iota(jnp.int32, (C, GLA_QK), 0) % GLA_BAND
    vsh = [v]
    for d in range(GLA_BAND):
        if d == 0:
            ks_, bs_ = k, b
        else:
            ks_, bs_ = pltpu.roll(k, d, 0), pltpu.roll(b, d, 0)
            vsh.append(pltpu.roll(v, d, 0))
        m = q * ks_ * jnp.exp(jnp.minimum(b - bs_, 0.0))
        slab[d * C:(d + 1) * C, :] = jnp.where(rloc >= d, m, 0.0).astype(BF16)
    rep = _dot(slab[...], e2_ref[...])
    o = rep[0:C] * vsh[0]
    for d in range(1, GLA_BAND):
        o = o + rep[d * C:(d + 1) * C] * vsh[d]

    lane_head = lax.broadcasted_iota(jnp.int32, (HC, GLA_QK), 1) // GLA_DK
    row_head = lax.broadcasted_iota(jnp.int32, (HC, GLA_QK), 0) // C
    own_head = lane_head == row_head
    si = lax.broadcasted_iota(jnp.int32, (HC, C), 0) % C
    sj = lax.broadcasted_iota(jnp.int32, (HC, C), 1)
    scores = jnp.zeros((HC, C), F32)
    s = C // 2
    while s >= GLA_BAND:
        ref = jnp.concatenate([jnp.broadcast_to(b[p * 2 * s + s - 1:p * 2 * s + s, :], (2 * s, GLA_QK))
                               for p in range(C // (2 * s))], axis=0)
        ql = q * jnp.exp(jnp.minimum(b - ref, 0.0))
        kl = (k * jnp.exp(jnp.minimum(ref - b, 0.0))).astype(BF16)
        qs = jnp.where(own_head, jnp.concatenate([ql] * GLA_HEADS, axis=0), 0.0).astype(BF16)
        lvl = ((si // (2 * s)) == (sj // (2 * s))) & ((si // s) % 2 == 1) & ((sj // s) % 2 == 0)
        scores = scores + jnp.where(lvl, _dot_nt(qs, kl), 0.0)
        s //= 2
    scores = scores.astype(BF16)
    vb = v.astype(BF16)

    st = st_ref[...]
    o = o + _dot_nt((q * jnp.exp(b)).astype(BF16), st.astype(BF16))
    outs = []
    for h in range(GLA_HEADS):
        vs = slice(h * GLA_DV, (h + 1) * GLA_DV)
        outs.append(_rms(o[:, vs] + _dot(scores[h * C:(h + 1) * C], vb[:, vs])))
    g = g_ref[...]
    o_ref[...] = jnp.concatenate(outs, axis=-1) * gn_ref[...] * (g * jax.nn.sigmoid(g))

    kd_hi, kd_lo = _split(k * jnp.exp(blast - b))
    vt_hi, vt_lo = _split(v.T)
    upd = _dot(vt_hi, kd_hi) + _dot(vt_lo, kd_hi) + _dot(vt_hi, kd_lo)
    blk = (lax.broadcasted_iota(jnp.int32, st.shape, 0) // GLA_DV
           == lax.broadcasted_iota(jnp.int32, st.shape, 1) // GLA_DK)
    st_new = jnp.exp(blast) * st + jnp.where(blk, upd, 0.0)
    st_ref[...] = st_new

    @pl.when(c == n_chunks - 1)
    def _():
        for h in range(GLA_HEADS):
            sfin_ref[0, h] = st_new[h * GLA_DV:(h + 1) * GLA_DV, h * GLA_DK:(h + 1) * GLA_DK].T


def _gla_prompt(p2, n_seq, seq_len, e2, gn):
    C = GLA_CHUNK
    nch = seq_len // C
    rows = n_seq * seq_len

    def tok(width, colblk):
        return pl.BlockSpec((C, width), lambda b, c: (b * nch + c, colblk))

    return pl.pallas_call(
        functools.partial(_gla_prompt_kernel, n_chunks=nch),
        grid=(n_seq, nch),
        in_specs=[tok(GLA_QK, P_Q // GLA_QK), tok(GLA_QK, P_K // GLA_QK),
                  tok(GLA_WIDTH, P_V // GLA_WIDTH), tok(GLA_WIDTH, P_G // GLA_WIDTH),
                  tok(GLA_QK, P_LA // GLA_QK),
                  _const_spec(e2.shape), _const_spec(gn.shape)],
        out_specs=[pl.BlockSpec((C, GLA_WIDTH), lambda b, c: (b * nch + c, 0)),
                   pl.BlockSpec((1, GLA_HEADS, GLA_DK, GLA_DV), lambda b, c: (b, 0, 0, 0))],
        out_shape=(jax.ShapeDtypeStruct((rows, GLA_WIDTH), F32),
                   jax.ShapeDtypeStruct((n_seq, GLA_HEADS, GLA_DK, GLA_DV), F32)),
        scratch_shapes=[pltpu.VMEM((GLA_WIDTH, GLA_QK), F32),
                        pltpu.VMEM((GLA_BAND * C, GLA_QK), BF16)],
        compiler_params=_params(("parallel", "arbitrary")),
        name="gla_prompt",
    )(p2, p2, p2, p2, p2, e2, gn)


def _gla_sample_kernel(q_ref, k_ref, v_ref, g_ref, la_ref, s0_ref, e2_ref, rep_ref, gn_ref,
                       o_ref, snew_ref, *, seq_len):
    R = q_ref.shape[0]
    nseq = R // seq_len
    SD = nseq * GLA_DK
    la = la_ref[...]
    row = lax.broadcasted_iota(jnp.int32, (R, R), 0)
    col = lax.broadcasted_iota(jnp.int32, (R, R), 1)
    same = (col // seq_len) == (row // seq_len)
    b = _cumsum_rows(same & (col <= row), la)
    btot = _cumsum_rows(same, la)
    q = q_ref[...] * (GLA_DK ** -0.5)
    k = k_ref[...]
    v = v_ref[...]
    rmod = lax.broadcasted_iota(jnp.int32, (R, GLA_QK), 0) % seq_len

    o = jnp.zeros((R, GLA_WIDTH), F32)
    for d in range(seq_len):
        ks_, bs_, vs_ = (k, b, v) if d == 0 else (pltpu.roll(k, d, 0), pltpu.roll(b, d, 0),
                                                  pltpu.roll(v, d, 0))
        m = q * ks_ * jnp.exp(jnp.minimum(b - bs_, 0.0))
        m = jnp.where(rmod >= d, m, 0.0)
        o = o + _dot(m.astype(BF16), e2_ref[...]) * vs_

    xr = lax.broadcasted_iota(jnp.int32, (R, GLA_HEADS * SD), 0) // seq_len
    xc = (lax.broadcasted_iota(jnp.int32, (R, GLA_HEADS * SD), 1) % SD) // GLA_DK
    own = xr == xc
    rep = rep_ref[...]

    def expand(x_bf16):
        return jnp.where(own, _dot(x_bf16, rep), 0.0).astype(BF16)

    qx = expand((q * jnp.exp(b)).astype(BF16))
    kd_hi, kd_lo = _split(k * jnp.exp(btot - b))
    kx_hi, kx_lo = expand(kd_hi), expand(kd_lo)
    ea, eb, ec = _split3(jnp.exp(btot))
    ax = (expand(ea), expand(eb), expand(ec))
    last = (lax.broadcasted_iota(jnp.int32, (R, GLA_DV), 0) % seq_len == seq_len - 1).astype(BF16)

    outs = []
    for h in range(GLA_HEADS):
        xs = slice(h * SD, (h + 1) * SD)
        vs = slice(h * GLA_DV, (h + 1) * GLA_DV)
        s0 = s0_ref[:, h].reshape(SD, GLA_DV)
        o_h = o[:, vs] + _dot(qx[:, xs], s0.astype(BF16))
        outs.append(_rms(o_h))
        v_hi, v_lo = _split(v[:, vs])
        upd = _dot_tn(kx_hi[:, xs], v_hi) + _dot_tn(kx_lo[:, xs], v_hi) + _dot_tn(kx_hi[:, xs], v_lo)
        decay = _dot_tn(ax[0][:, xs], last) + _dot_tn(ax[1][:, xs], last) + _dot_tn(ax[2][:, xs], last)
        snew_ref[:, h] = (decay * s0 + upd).reshape(nseq, GLA_DK, GLA_DV)
    g = g_ref[...]
    o_ref[...] = jnp.concatenate(outs, axis=-1) * gn_ref[...] * (g * jax.nn.sigmoid(g))


def _gla_sample(p2, s0, seq_len, e2, rep, gn):
    n_seq = s0.shape[0]
    R = SAMPLE_SEQ_BLOCK * seq_len
    nblk = n_seq // SAMPLE_SEQ_BLOCK

    def tok(width, colblk):
        return pl.BlockSpec((R, width), lambda i: (i, colblk))

    st_spec = pl.BlockSpec((SAMPLE_SEQ_BLOCK, GLA_HEADS, GLA_DK, GLA_DV), lambda i: (i, 0, 0, 0))
    return pl.pallas_call(
        functools.partial(_gla_sample_kernel, seq_len=seq_len),
        grid=(nblk,),
        in_specs=[tok(GLA_QK, P_Q // GLA_QK), tok(GLA_QK, P_K // GLA_QK),
                  tok(GLA_WIDTH, P_V // GLA_WIDTH), tok(GLA_WIDTH, P_G // GLA_WIDTH),
                  tok(GLA_QK, P_LA // GLA_QK), st_spec,
                  _const_spec(e2.shape), _const_spec(rep.shape), _const_spec(gn.shape)],
        out_specs=[pl.BlockSpec((R, GLA_WIDTH), lambda i: (i, 0)), st_spec],
        out_shape=(jax.ShapeDtypeStruct((n_seq * seq_len, GLA_WIDTH), F32),
                   jax.ShapeDtypeStruct(s0.shape, F32)),
        compiler_params=_params(("parallel",)),
        name="gla_sample",
    )(p2, p2, p2, p2, p2, s0, e2, rep, gn)


def _s5_local_scan(bur, bui, pwre_ref, pwim_ref, lanes, group):
    rmod = lax.broadcasted_iota(jnp.int32, bur.shape, 0) % group
    s = 1
    while s < group:
        ar = pwre_ref[s - 1:s, lanes]
        ai = pwim_ref[s - 1:s, lanes]
        sr = jnp.where(rmod >= s, pltpu.roll(bur, s, 0), 0.0)
        si = jnp.where(rmod >= s, pltpu.roll(bui, s, 0), 0.0)
        bur, bui = bur + ar * sr - ai * si, bui + ar * si + ai * sr
        s *= 2
    return bur, bui


def _s5_tail(ys, u, d_ref, wglu_ref, bglu_ref):
    y = jnp.concatenate(ys, axis=-1) + d_ref[...] * u
    z = jax.nn.gelu(y)
    return z * jax.nn.sigmoid(_dot(z.astype(BF16), wglu_ref[...]) + bglu_ref[...])


def _s5_prompt_kernel(u_ref, brh_ref, brl_ref, bih_ref, bil_ref, cre_ref, cim_ref,
                      pwre_ref, pwim_ref, d_ref, wglu_ref, bglu_ref,
                      o_ref, stre_ref, stim_ref, hre_s, him_s, car_re, car_im):
    t = pl.program_id(1)
    TT = u_ref.shape[0]

    @pl.when(t == 0)
    def _():
        car_re[...] = jnp.zeros_like(car_re)
        car_im[...] = jnp.zeros_like(car_im)

    u = u_ref[...]
    ys = []
    for l in range(S5_NBLK):
        lanes = slice(l * S5_BLK_ST, (l + 1) * S5_BLK_ST)
        ul = u[:, l * S5_BLK_CH:(l + 1) * S5_BLK_CH]
        bur = _dot3(ul, brh_ref[l], brl_ref[l])
        bui = _dot3(ul, bih_ref[l], bil_ref[l])
        bur, bui = _s5_local_scan(bur, bui, pwre_ref, pwim_ref, lanes, 8)
        hre_s[...] = bur
        him_s[...] = bui
        p8r = pwre_ref[:, lanes]
        p8i = pwim_ref[:, lanes]

        def grp(r, carry):
            cr, ci = carry
            off = pl.multiple_of(r * 8, 8)
            xr = hre_s[pl.ds(off, 8), :] + p8r * cr - p8i * ci
            xi = him_s[pl.ds(off, 8), :] + p8r * ci + p8i * cr
            hre_s[pl.ds(off, 8), :] = xr
            him_s[pl.ds(off, 8), :] = xi
            return (jnp.broadcast_to(xr[7:8], xr.shape), jnp.broadcast_to(xi[7:8], xi.shape))

        cr0 = jnp.broadcast_to(car_re[:, lanes], (8, S5_BLK_ST))
        ci0 = jnp.broadcast_to(car_im[:, lanes], (8, S5_BLK_ST))
        cr, ci = lax.fori_loop(0, TT // 8, grp, (cr0, ci0))
        car_re[:, lanes] = cr[0:1]
        car_im[:, lanes] = ci[0:1]
        ys.append(_dot(hre_s[...].astype(BF16), cre_ref[l]) - _dot(him_s[...].astype(BF16), cim_ref[l]))
    o_ref[...] = _s5_tail(ys, u, d_ref, wglu_ref, bglu_ref)
    stre_ref[0] = car_re[...]
    stim_ref[0] = car_im[...]


def _s5_sample_kernel(u_ref, s0re_ref, s0im_ref, brh_ref, brl_ref, bih_ref, bil_ref, cre_ref, cim_ref,
                      pwre_ref, pwim_ref, d_ref, wglu_ref, bglu_ref,
                      o_ref, hre_ref, him_ref, *, seq_len):
    u = u_ref[...]
    ys = []
    for l in range(S5_NBLK):
        lanes = slice(l * S5_BLK_ST, (l + 1) * S5_BLK_ST)
        ul = u[:, l * S5_BLK_CH:(l + 1) * S5_BLK_CH]
        ar = pwre_ref[0:1, lanes]
        ai = pwim_ref[0:1, lanes]
        sr = s0re_ref[:, lanes]
        si = s0im_ref[:, lanes]
        bur = _dot3(ul, brh_ref[l], brl_ref[l]) + (ar * sr - ai * si)
        bui = _dot3(ul, bih_ref[l], bil_ref[l]) + (ar * si + ai * sr)
        bur, bui = _s5_local_scan(bur, bui, pwre_ref, pwim_ref, lanes, seq_len)
        hre_ref[:, lanes] = bur
        him_ref[:, lanes] = bui
        ys.append(_dot(bur.astype(BF16), cre_ref[l]) - _dot(bui.astype(BF16), cim_ref[l]))
    o_ref[...] = _s5_tail(ys, u, d_ref, wglu_ref, bglu_ref)


def _s5_weight_specs(ws):
    return [_const_spec(w.shape) for w in ws]


def _s5_prompt(p2, n_seq, seq_len, ws):
    TT = TOK_TILE
    nt = seq_len // TT
    rows = n_seq * seq_len
    st_spec = pl.BlockSpec((1, 1, S5_LANES), lambda b, t: (b, 0, 0))
    return pl.pallas_call(
        _s5_prompt_kernel,
        grid=(n_seq, nt),
        in_specs=[pl.BlockSpec((TT, S5_WIDTH), lambda b, t: (b * nt + t, P_U // S5_WIDTH))]
        + _s5_weight_specs(ws),
        out_specs=[pl.BlockSpec((TT, S5_WIDTH), lambda b, t: (b * nt + t, 0)), st_spec, st_spec],
        out_shape=(jax.ShapeDtypeStruct((rows, S5_WIDTH), F32),
                   jax.ShapeDtypeStruct((n_seq, 1, S5_LANES), F32),
                   jax.ShapeDtypeStruct((n_seq, 1, S5_LANES), F32)),
        scratch_shapes=[pltpu.VMEM((TT, S5_BLK_ST), F32), pltpu.VMEM((TT, S5_BLK_ST), F32),
                        pltpu.VMEM((1, S5_LANES), F32), pltpu.VMEM((1, S5_LANES), F32)],
        compiler_params=_params(("parallel", "arbitrary")),
        name="s5_prompt",
    )(p2, *ws)


def _s5_sample(p2, s0re_rows, s0im_rows, seq_len, ws):
    rows = s0re_rows.shape[0]
    TT = TOK_TILE
    row_spec = pl.BlockSpec((TT, S5_LANES), lambda i: (i, 0))
    return pl.pallas_call(
        functools.partial(_s5_sample_kernel, seq_len=seq_len),
        grid=(rows // TT,),
        in_specs=[pl.BlockSpec((TT, S5_WIDTH), lambda i: (i, P_U // S5_WIDTH)), row_spec, row_spec]
        + _s5_weight_specs(ws),
        out_specs=[pl.BlockSpec((TT, S5_WIDTH), lambda i: (i, 0)), row_spec, row_spec],
        out_shape=(jax.ShapeDtypeStruct((rows, S5_WIDTH), F32),
                   jax.ShapeDtypeStruct((rows, S5_LANES), F32),
                   jax.ShapeDtypeStruct((rows, S5_LANES), F32)),
        compiler_params=_params(("parallel",)),
        name="s5_sample",
    )(p2, s0re_rows, s0im_rows, *ws)


def _outproj_kernel(x_ref, og_ref, os_ref, wt_ref, wb_ref, n2_ref,
                    x1_ref, h2_ref):
    x1 = (x_ref[...] + _dot(og_ref[...].astype(BF16), wt_ref[...])
          + _dot(os_ref[...].astype(BF16), wb_ref[...]))
    x1_ref[...] = x1
    h2_ref[...] = _rms(x1) * n2_ref[...]


def _out_proj(x2, og, osx, wt, wb, n2):
    rows = x2.shape[0]
    big = pl.BlockSpec((TOK_TILE, D_MODEL), lambda i: (i, 0))
    half = pl.BlockSpec((TOK_TILE, GLA_WIDTH), lambda i: (i, 0))
    return pl.pallas_call(
        _outproj_kernel,
        grid=(rows // TOK_TILE,),
        in_specs=[big, half, half] + [_const_spec(w.shape) for w in (wt, wb, n2)],
        out_specs=[big, big],
        out_shape=(jax.ShapeDtypeStruct((rows, D_MODEL), F32),) * 2,
        compiler_params=_params(("parallel",)),
        name="out_proj",
    )(x2, og, osx, wt, wb, n2)


def _extract_top(work, n, store=None, want_rank=False):
    first = None
    m = None
    rank = jnp.full(work.shape, float(n), F32) if want_rank else None
    for r in range(n):
        m = jnp.max(work, axis=0, keepdims=True)
        if r == 0:
            first = m
        if store is not None:
            store(r, m)
        hit = work == m
        if want_rank:
            rank = jnp.where(hit, float(r), rank)
        if r < n - 1:
            work = jnp.where(hit, -jnp.inf, work)
    return first, m, rank


def _route_kernel(h2_ref, wqh_ref, wql_ref, kh_ref, kl_ref,
                  n1_ref, e1_ref, rho_ref, e2_ref, s_s, rank_s, top_s):
    TM = h2_ref.shape[0]
    qp = _dot3(h2_ref[...], wqh_ref[...], wql_ref[...])
    for hp in range(2 * PEER_HEADS):
        q_hi, q_lo = _split(qp[:, hp * PEER_KEYS:(hp + 1) * PEER_KEYS])
        s_s[hp] = _dot_nt(kh_ref[hp], q_hi) + _dot_nt(kl_ref[hp], q_hi) + _dot_nt(kh_ref[hp], q_lo)
    K = PEER_TOPK
    for lc in range(TM // LANES):
        sl = slice(lc * LANES, (lc + 1) * LANES)
        for hp in range(2 * PEER_HEADS):
            def store(r, m, hp=hp):
                top_s[hp, r:r + 1, sl] = m

            _, _, rank = _extract_top(s_s[hp, :, sl], K, store, want_rank=True)
            rank_s[hp, :, sl] = rank
        for h in range(PEER_HEADS):
            v1 = top_s[2 * h, :, sl]
            v2 = top_s[2 * h + 1, :, sl]
            cands = [v1 + v2[0:1]] + [v1[0:8] + v2[b:b + 1] for b in range(1, 8)] + [v2[8:K] + v1[0:1]]
            cand = jnp.concatenate(cands, axis=0)
            mx, thr, _ = _extract_top(cand, K)
            z = jnp.sum(jnp.where(cand >= thr, jnp.exp(cand - mx), 0.0), axis=0, keepdims=True)
            cnt = jnp.zeros(v1.shape, F32)
            for b in range(K):
                cnt = cnt + jnp.where(v1 + v2[b:b + 1] >= thr, 1.0, 0.0)
            rho1 = rank_s[2 * h, :, sl]
            n1 = jnp.zeros(rho1.shape, F32)
            for a in range(K):
                n1 = jnp.where(rho1 == float(a), cnt[a:a + 1], n1)
            n1_ref[h, :, sl] = n1
            e1_ref[h, :, sl] = jnp.exp(s_s[2 * h, :, sl] - v1[0:1])
            rho_ref[h, :, sl] = rank_s[2 * h + 1, :, sl].astype(BF16)
            e2_ref[h, :, sl] = (jnp.exp(s_s[2 * h + 1, :, sl] - v2[0:1]) * (1.0 / z)).astype(BF16)


def _route(h2, wqh, wql, kh, kl):
    rows = h2.shape[0]
    TM = TOK_TILE
    key_spec = pl.BlockSpec((PEER_HEADS, PEER_KEYS, TM), lambda i: (0, 0, i))
    return pl.pallas_call(
        _route_kernel,
        grid=(rows // TM,),
        in_specs=[pl.BlockSpec((TM, D_MODEL), lambda i: (i, 0)),
                  _const_spec(wqh.shape), _const_spec(wql.shape),
                  _const_spec(kh.shape), _const_spec(kl.shape)],
        out_specs=[key_spec] * 4,
        out_shape=(jax.ShapeDtypeStruct((PEER_HEADS, PEER_KEYS, rows), F32),
                   jax.ShapeDtypeStruct((PEER_HEADS, PEER_KEYS, rows), F32),
                   jax.ShapeDtypeStruct((PEER_HEADS, PEER_KEYS, rows), BF16),
                   jax.ShapeDtypeStruct((PEER_HEADS, PEER_KEYS, rows), BF16)),
        scratch_shapes=[pltpu.VMEM((2 * PEER_HEADS, PEER_KEYS, TM), F32),
                        pltpu.VMEM((2 * PEER_HEADS, PEER_KEYS, TM), F32),
                        pltpu.VMEM((2 * PEER_HEADS, PEER_TOPK, TM), F32)],
        compiler_params=_params(("parallel",)),
        name="peer_route",
    )(h2, wqh, wql, kh, kl)


def _peer_kernel(h2_ref, x1_ref, n1_ref, e1_ref, rho_ref, e2_ref, u_ref, vt_ref, fn_ref, y_ref,
                 acc, h2b, wact, *, n_eblk):
    e = pl.program_id(1)
    EB = u_ref.shape[0]
    sub = EB // PEER_KEYS

    @pl.when(e == 0)
    def _():
        acc[...] = jnp.zeros_like(acc)
        h2b[...] = h2_ref[...].astype(BF16)

    act = _dot_nt(u_ref[...], h2b[...])
    for a in range(sub):
        i1 = e * sub + a
        w = jnp.zeros((PEER_KEYS, act.shape[1]), BF16)
        for h in range(PEER_HEADS):
            n1 = n1_ref[h, pl.ds(i1, 1), :].astype(BF16)
            g1 = e1_ref[h, pl.ds(i1, 1), :].astype(BF16)
            e2 = e2_ref[h]
            w = w + jnp.where(rho_ref[h] < n1, e2, jnp.zeros_like(e2)) * g1
        x = act[a * PEER_KEYS:(a + 1) * PEER_KEYS]
        t = jnp.exp(x * (-2.0 * GELU_C - (2.0 * GELU_C * GELU_A) * (x * x)))
        wact[a * PEER_KEYS:(a + 1) * PEER_KEYS, :] = w * (x / (1.0 + t)).astype(BF16)
    acc[...] += _dot(vt_ref[...], wact[...])

    @pl.when(e == n_eblk - 1)
    def _():
        out = x1_ref[...] + acc[...].T
        y_ref[...] = _rms(out) * fn_ref[...]


def _peer(h2, x1, n1, e1, rho, e2, u_bf, vt_bf, fn):
    rows = h2.shape[0]
    TM = PEER_TOK_TILE
    EB = PEER_EXP_TILE
    n_eblk = PEER_EXPERTS // EB
    big = pl.BlockSpec((TM, D_MODEL), lambda i, e: (i, 0))
    key_spec = pl.BlockSpec((PEER_HEADS, PEER_KEYS, TM), lambda i, e: (0, 0, i))
    return pl.pallas_call(
        functools.partial(_peer_kernel, n_eblk=n_eblk),
        grid=(rows // TM, n_eblk),
        in_specs=[big, big, key_spec, key_spec, key_spec, key_spec,
                  pl.BlockSpec((EB, D_MODEL), lambda i, e: (e, 0)),
                  pl.BlockSpec((D_MODEL, EB), lambda i, e: (0, e)),
                  _const_spec(fn.shape)],
        out_specs=big,
        out_shape=jax.ShapeDtypeStruct((rows, D_MODEL), F32),
        scratch_shapes=[pltpu.VMEM((D_MODEL, TM), F32), pltpu.VMEM((TM, D_MODEL), BF16),
                        pltpu.VMEM((EB, TM), BF16)],
        compiler_params=_params(("parallel", "arbitrary")),
        name="peer_experts",
    )(h2, x1, n1, e1, rho, e2, u_bf, vt_bf, fn)


def _block_diag(w, eye):
    n, g, a, b = w.shape
    return jnp.einsum('lgab,gh->lgahb', w, eye).reshape(n, g * a, g * b)


def _gla_constants():
    lane_head = jnp.arange(GLA_QK) // GLA_DK
    e2 =(lane_head[:, None] == (jnp.arange(GLA_WIDTH) // GLA_DV)[None, :]).astype(BF16)
    sd = SAMPLE_SEQ_BLOCK * GLA_DK
    xc = jnp.arange(GLA_HEADS * sd)
    rep = ((lane_head[:, None] == (xc // sd)[None, :])
           & ((jnp.arange(GLA_QK) % GLA_DK)[:, None] == (xc % GLA_DK)[None, :])).astype(BF16)
    return e2, rep


def kernel(x_prompt, x_sample, state_gla, state_s5_re, state_s5_im, norm1, w_in, w_a2, b_a2, gla_norm, s5_lam_re, s5_lam_im, s5_log_dt, s5_b_re, s5_b_im, s5_c_re, s5_c_im, s5_d, w_glu, b_glu, w_out, norm2, peer_wq, peer_keys, peer_u, peer_v, final_norm):
    depth = norm1.shape[0]
    assert depth == 1, "single-layer trunk"
    n_p, len_p, _ = x_prompt.shape
    n_s, len_s, _ = x_sample.shape
    l = 0

    w = w_in[l]
    w_re = jnp.concatenate([w[:, 0:1536], w[:, 1552:2064], w[:, 1536:1552],
                            jnp.zeros((D_MODEL, W_IN_COLS - 2064), F32)], axis=1)
    w_bf = w_re.astype(BF16)
    a2 = jnp.concatenate([w_a2[l], jnp.zeros((W_IN_COLS - P_LA - GLA_RANK, GLA_QK), F32)], axis=0)
    a2hi, a2lo = _split(a2)
    ba2 = b_a2[l].reshape(1, GLA_QK)
    n1 = norm1[l].reshape(1, D_MODEL)
    gn = gla_norm[l].reshape(1, GLA_WIDTH)
    e2, rep = _gla_constants()

    pwre, pwim, bbre, bbim = _s5_prep(s5_lam_re[l], s5_lam_im[l], s5_log_dt[l], s5_b_re[l], s5_b_im[l])
    eye = jnp.eye(8, dtype=F32)
    blk = lambda t: t.reshape(S5_NBLK, 8, S5_GROUP_CH, S5_STATE)
    brh, brl = _split(_block_diag(blk(bbre), eye))
    bih, bil = _split(_block_diag(blk(bbim), eye))
    cre = _block_diag(jnp.swapaxes(blk(s5_c_re[l]), 2, 3), eye).astype(BF16)
    cim = _block_diag(jnp.swapaxes(blk(s5_c_im[l]), 2, 3), eye).astype(BF16)
    s5w = (brh, brl, bih, bil, cre, cim, pwre.reshape(8, S5_LANES), pwim.reshape(8, S5_LANES),
           s5_d[l].reshape(1, S5_WIDTH), w_glu[l].astype(BF16), b_glu[l].reshape(1, S5_WIDTH))

    wt = w_out[l][:GLA_WIDTH].astype(BF16)
    wb = w_out[l][GLA_WIDTH:].astype(BF16)
    n2 = norm2[l].reshape(1, D_MODEL)
    wqh, wql = _split(peer_wq[l])
    kh, kl = _split(peer_keys[l].reshape(2 * PEER_HEADS, PEER_KEYS, PEER_DQ // 2))
    u_bf = peer_u[l].astype(BF16)
    vt_bf = peer_v[l].astype(BF16).T
    fn = final_norm.reshape(1, D_MODEL)

    def tail(x2, og, osx):
        x1, h2 = _out_proj(x2, og, osx, wt, wb, n2)
        n1, e1, rho, e2g = _route(h2, wqh, wql, kh, kl)
        return _peer(h2, x1, n1, e1, rho, e2g, u_bf, vt_bf, fn)

    xp = x_prompt.reshape(n_p * len_p, D_MODEL)
    pp = _in_proj(xp, n1, w_bf, a2hi, a2lo, ba2)
    og_p, gla_p = _gla_prompt(pp, n_p, len_p, e2, gn)
    os_p, sre_p, sim_p = _s5_prompt(pp, n_p, len_p, s5w)
    y_p = tail(xp, og_p, os_p).reshape(n_p, len_p, D_MODEL)

    xs = x_sample.reshape(n_s * len_s, D_MODEL)
    ps = _in_proj(xs, n1, w_bf, a2hi, a2lo, ba2)
    og_s, gla_s = _gla_sample(ps, state_gla[l], len_s, e2, rep, gn)
    first_row = lambda s: jnp.pad(s.reshape(n_s, 1, S5_LANES), ((0, 0), (0, len_s - 1), (0, 0))
                                  ).reshape(n_s * len_s, S5_LANES)
    os_s, hre_s, him_s = _s5_sample(ps, first_row(state_s5_re[l]), first_row(state_s5_im[l]), len_s, s5w)
    y_s = tail(xs, og_s, os_s).reshape(n_s, len_s, D_MODEL)
    last_row = lambda hs: hs.reshape(n_s, len_s, S5_GROUPS, S5_STATE)[:, len_s - 1]

    st = lambda a: a.reshape(1, n_p, S5_GROUPS, S5_STATE)
    return (y_p, y_s, gla_p[None], st(sre_p), st(sim_p),
            gla_s[None], last_row(hre_s)[None], last_row(him_s)[None])
```

```python
import functools

import jax
import jax.numpy as jnp
from jax import lax
from jax.experimental import pallas as pl
from jax.experimental.pallas import tpu as pltpu

F32 = jnp.float32
BF16 = jnp.bfloat16

D_MODEL = 1024
GLA_HEADS = 4
GLA_DK = 64
GLA_DV = 128
GLA_QK = GLA_HEADS * GLA_DK
GLA_WIDTH = GLA_HEADS * GLA_DV
GLA_RANK = 16
GLA_TAU = 16.0
GLA_CHUNK = 64
S5_WIDTH = 512
S5_GROUP_CH = 16
S5_GROUPS = 32
S5_STATE = 64
S5_LANES = S5_GROUPS * S5_STATE
S5_NBLK = 4
S5_BLK_CH = S5_WIDTH // S5_NBLK
S5_BLK_ST = S5_LANES // S5_NBLK
PEER_KEYS = 128
PEER_EXPERTS = PEER_KEYS * PEER_KEYS
PEER_HEADS = 8
PEER_DQ = 256
PEER_TOPK = 16
EPS = 1e-6
LANES = 128
GELU_C = 0.7978845608028654
GELU_A = 0.044715

P_Q, P_K, P_V, P_G, P_U, P_LA = 0, 256, 512, 1024, 1536, 2048
P_COLS = 2304
W_IN_COLS = 2176

TOK_TILE = 256
PEER_TOK_TILE = 512
PEER_EXP_TILE = 512
SAMPLE_SEQ_BLOCK = 16
VMEM_LIMIT = 56 * 1024 * 1024


def _split(x):
    hi = x.astype(BF16)
    lo = (x - hi.astype(F32)).astype(BF16)
    return hi, lo


def _split3(x):
    a = x.astype(BF16)
    r = x - a.astype(F32)
    b = r.astype(BF16)
    c = (r - b.astype(F32)).astype(BF16)
    return a, b, c


def _dot(a, b):
    return jnp.dot(a, b, preferred_element_type=F32)


def _dot_nt(a, b):
    return lax.dot_general(a, b, (((1,), (1,)), ((), ())), preferred_element_type=F32)


def _dot_tn(a, b):
    return lax.dot_general(a, b, (((0,), (0,)), ((), ())), preferred_element_type=F32)


def _dot3(a, b_hi, b_lo):
    a_hi, a_lo = _split(a)
    return _dot(a_hi, b_hi) + _dot(a_lo, b_hi) + _dot(a_hi, b_lo)


def _dot_exact01(m01, x):
    a, b, c = _split3(x)
    return _dot(m01, a) + _dot(m01, b) + _dot(m01, c)


def _rms(x):
    return x * lax.rsqrt(jnp.mean(x * x, axis=-1, keepdims=True) + EPS)


def _params(sem):
    return pltpu.CompilerParams(dimension_semantics=sem, vmem_limit_bytes=VMEM_LIMIT)


def _const_spec(shape):
    n = len(shape)
    return pl.BlockSpec(shape, lambda *_: (0,) * n)


def _s5prep_kernel(lr_ref, li_ref, ldt_ref, bret_ref, bimt_ref,
                   pwre_ref, pwim_ref, bbre_ref, bbim_ref):
    lr = lr_ref[...]
    li = li_ref[...]
    dt = jnp.exp(ldt_ref[...])
    mag = jnp.exp(lr * dt)
    abr = mag * jnp.cos(li * dt)
    abi = mag * jnp.sin(li * dt)
    den = lr * lr + li * li
    nr = abr - 1.0
    ni = abi
    fr = (nr * lr + ni * li) / den
    fi = (ni * lr - nr * li) / den
    bret = bret_ref[...]
    bimt = bimt_ref[...]
    bbre_ref[...] = fr[:, None, :] * bret - fi[:, None, :] * bimt
    bbim_ref[...] = fr[:, None, :] * bimt + fi[:, None, :] * bret
    pr, pi = abr, abi
    for i in range(8):
        pwre_ref[i] = pr
        pwim_ref[i] = pi
        pr, pi = pr * abr - pi * abi, pr * abi + pi * abr


def _s5_prep(lam_re, lam_im, log_dt, b_re, b_im):
    g, p = lam_re.shape
    ch = b_re.shape[-1]
    bret = jnp.transpose(b_re, (0, 2, 1))
    bimt = jnp.transpose(b_im, (0, 2, 1))
    out = pl.pallas_call(
        _s5prep_kernel,
        out_shape=(jax.ShapeDtypeStruct((8, g, p), F32), jax.ShapeDtypeStruct((8, g, p), F32),
                   jax.ShapeDtypeStruct((g, ch, p), F32), jax.ShapeDtypeStruct((g, ch, p), F32)),
        name="s5_prep",
    )(lam_re, lam_im, log_dt.reshape(g, 1), bret, bimt)
    return out


def _inproj_kernel(x_ref, n1_ref, w_ref, a2hi_ref, a2lo_ref, ba2_ref, p_ref):
    h = _rms(x_ref[...]) * n1_ref[...]
    p = _dot(h.astype(BF16), w_ref[...])
    alr = p[:, P_LA:W_IN_COLS]
    z = _dot3(alr, a2hi_ref[...], a2lo_ref[...]) + ba2_ref[...]
    log_sig = jnp.minimum(z, 0.0) - jnp.log1p(jnp.exp(-jnp.abs(z)))
    p_ref[:, 0:P_LA] = p[:, 0:P_LA]
    p_ref[:, P_LA:P_COLS] = log_sig * (1.0 / GLA_TAU)


def _in_proj(x2, n1, w_bf, a2hi, a2lo, ba2):
    rows = x2.shape[0]
    return pl.pallas_call(
        _inproj_kernel,
        grid=(rows // TOK_TILE,),
        in_specs=[pl.BlockSpec((TOK_TILE, D_MODEL), lambda i: (i, 0)),
                  _const_spec(n1.shape), _const_spec(w_bf.shape),
                  _const_spec(a2hi.shape), _const_spec(a2lo.shape), _const_spec(ba2.shape)],
        out_specs=pl.BlockSpec((TOK_TILE, P_COLS), lambda i: (i, 0)),
        out_shape=jax.ShapeDtypeStruct((rows, P_COLS), F32),
        compiler_params=_params(("parallel",)),
        name="in_proj",
    )(x2, n1, w_bf, a2hi, a2lo, ba2)


def _cumsum_rows(mask01, la):
    return _dot_exact01(mask01.astype(BF16), la)


GLA_BAND = 8


def _gla_prompt_kernel(q_ref, k_ref, v_ref, g_ref, la_ref, e2_ref, gn_ref,
                       o_ref, sfin_ref, st_ref, slab, *, n_chunks):
    c = pl.program_id(1)
    C = q_ref.shape[0]
    HC = GLA_HEADS * C

    @pl.when(c == 0)
    def _():
        st_ref[...] = jnp.zeros_like(st_ref)

    la = la_ref[...]
    row = lax.broadcasted_iota(jnp.int32, (C, C), 0)
    col = lax.broadcasted_iota(jnp.int32, (C, C), 1)
    b = _cumsum_rows(col <= row, la)
    q = q_ref[...] * (GLA_DK ** -0.5)
    k = k_ref[...]
    v = v_ref[...]
    blast = b[C - 1:C, :]

    rloc = lax.broadcasted_iota(jnp.int32, (C, GLA_QK), 0) % GLA_BAND
    vsh = [v]
    for d in range(GLA_BAND):
        if d == 0:
            ks_, bs_ = k, b
        else:
            ks_, bs_ = pltpu.roll(k, d, 0), pltpu.roll(b, d, 0)
            vsh.append(pltpu.roll(v, d, 0))
        m = q * ks_ * jnp.exp(jnp.minimum(b - bs_, 0.0))
        slab[d * C:(d + 1) * C, :] = jnp.where(rloc >= d, m, 0.0).astype(BF16)
    rep = _dot(slab[...], e2_ref[...])
    o = rep[0:C] * vsh[0]
    for d in range(1, GLA_BAND):
        o = o + rep[d * C:(d + 1) * C] * vsh[d]

    lane_head = lax.broadcasted_iota(jnp.int32, (HC, GLA_QK), 1) // GLA_DK
    row_head = lax.broadcasted_iota(jnp.int32, (HC, GLA_QK), 0) // C
    own_head = lane_head == row_head
    si = lax.broadcasted_iota(jnp.int32, (HC, C), 0) % C
    sj = lax.broadcasted_iota(jnp.int32, (HC, C), 1)
    scores = jnp.zeros((HC, C), F32)
    s = C // 2
    while s >= GLA_BAND:
        ref = jnp.concatenate([jnp.broadcast_to(b[p * 2 * s + s - 1:p * 2 * s + s, :], (2 * s, GLA_QK))
                               for p in range(C // (2 * s))], axis=0)
        ql = q * jnp.exp(jnp.minimum(b - ref, 0.0))
        kl = (k * jnp.exp(jnp.minimum(ref - b, 0.0))).astype(BF16)
        qs = jnp.where(own_head, jnp.concatenate([ql] * GLA_HEADS, axis=0), 0.0).astype(BF16)
        lvl = ((si // (2 * s)) == (sj // (2 * s))) & ((si // s) % 2 == 1) & ((sj // s) % 2 == 0)
        scores = scores + jnp.where(lvl, _dot_nt(qs, kl), 0.0)
        s //= 2
    scores = scores.astype(BF16)
    vb = v.astype(BF16)

    st = st_ref[...]
    o = o + _dot_nt((q * jnp.exp(b)).astype(BF16), st.astype(BF16))
    outs = []
    for h in range(GLA_HEADS):
        vs = slice(h * GLA_DV, (h + 1) * GLA_DV)
        outs.append(_rms(o[:, vs] + _dot(scores[h * C:(h + 1) * C], vb[:, vs])))
    g = g_ref[...]
    o_ref[...] = jnp.concatenate(outs, axis=-1) * gn_ref[...] * (g * jax.nn.sigmoid(g))

    kd_hi, kd_lo = _split(k * jnp.exp(blast - b))
    vt_hi, vt_lo = _split(v.T)
    upd = _dot(vt_hi, kd_hi) + _dot(vt_lo, kd_hi) + _dot(vt_hi, kd_lo)
    blk = (lax.broadcasted_iota(jnp.int32, st.shape, 0) // GLA_DV
           == lax.broadcasted_iota(jnp.int32, st.shape, 1) // GLA_DK)
    st_new = jnp.exp(blast) * st + jnp.where(blk, upd, 0.0)
    st_ref[...] = st_new

    @pl.when(c == n_chunks - 1)
    def _():
        for h in range(GLA_HEADS):
            sfin_ref[0, h] = st_new[h * GLA_DV:(h + 1) * GLA_DV, h * GLA_DK:(h + 1) * GLA_DK].T


def _gla_prompt(p2, n_seq, seq_len, e2, gn):
    C = GLA_CHUNK
    nch = seq_len // C
    rows = n_seq * seq_len

    def tok(width, colblk):
        return pl.BlockSpec((C, width), lambda b, c: (b * nch + c, colblk))

    return pl.pallas_call(
        functools.partial(_gla_prompt_kernel, n_chunks=nch),
        grid=(n_seq, nch),
        in_specs=[tok(GLA_QK, P_Q // GLA_QK), tok(GLA_QK, P_K // GLA_QK),
                  tok(GLA_WIDTH, P_V // GLA_WIDTH), tok(GLA_WIDTH, P_G // GLA_WIDTH),
                  tok(GLA_QK, P_LA // GLA_QK),
                  _const_spec(e2.shape), _const_spec(gn.shape)],
        out_specs=[pl.BlockSpec((C, GLA_WIDTH), lambda b, c: (b * nch + c, 0)),
                   pl.BlockSpec((1, GLA_HEADS, GLA_DK, GLA_DV), lambda b, c: (b, 0, 0, 0))],
        out_shape=(jax.ShapeDtypeStruct((rows, GLA_WIDTH), F32),
                   jax.ShapeDtypeStruct((n_seq, GLA_HEADS, GLA_DK, GLA_DV), F32)),
        scratch_shapes=[pltpu.VMEM((GLA_WIDTH, GLA_QK), F32),
                        pltpu.VMEM((GLA_BAND * C, GLA_QK), BF16)],
        compiler_params=_params(("parallel", "arbitrary")),
        name="gla_prompt",
    )(p2, p2, p2, p2, p2, e2, gn)


def _gla_sample_kernel(q_ref, k_ref, v_ref, g_ref, la_ref, s0_ref, e2_ref, rep_ref, gn_ref,
                       o_ref, snew_ref, *, seq_len):
    R = q_ref.shape[0]
    nseq = R // seq_len
    SD = nseq * GLA_DK
    la = la_ref[...]
    row = lax.broadcasted_iota(jnp.int32, (R, R), 0)
    col = lax.broadcasted_iota(jnp.int32, (R, R), 1)
    same = (col // seq_len) == (row // seq_len)
    b = _cumsum_rows(same & (col <= row), la)
    btot = _cumsum_rows(same, la)
    q = q_ref[...] * (GLA_DK ** -0.5)
    k = k_ref[...]
    v = v_ref[...]
    rmod = lax.broadcasted_iota(jnp.int32, (R, GLA_QK), 0) % seq_len

    o = jnp.zeros((R, GLA_WIDTH), F32)
    for d in range(seq_len):
        ks_, bs_, vs_ = (k, b, v) if d == 0 else (pltpu.roll(k, d, 0), pltpu.roll(b, d, 0),
                                                  pltpu.roll(v, d, 0))
        m = q * ks_ * jnp.exp(jnp.minimum(b - bs_, 0.0))
        m = jnp.where(rmod >= d, m, 0.0)
        o = o + _dot(m.astype(BF16), e2_ref[...]) * vs_

    xr = lax.broadcasted_iota(jnp.int32, (R, GLA_HEADS * SD), 0) // seq_len
    xc = (lax.broadcasted_iota(jnp.int32, (R, GLA_HEADS * SD), 1) % SD) // GLA_DK
    own = xr == xc
    rep = rep_ref[...]

    def expand(x_bf16):
        return jnp.where(own, _dot(x_bf16, rep), 0.0).astype(BF16)

    qx = expand((q * jnp.exp(b)).astype(BF16))
    kd_hi, kd_lo = _split(k * jnp.exp(btot - b))
    kx_hi, kx_lo = expand(kd_hi), expand(kd_lo)
    ea, eb, ec = _split3(jnp.exp(btot))
    ax = (expand(ea), expand(eb), expand(ec))
    last = (lax.broadcasted_iota(jnp.int32, (R, GLA_DV), 0) % seq_len == seq_len - 1).astype(BF16)

    outs = []
    for h in range(GLA_HEADS):
        xs = slice(h * SD, (h + 1) * SD)
        vs = slice(h * GLA_DV, (h + 1) * GLA_DV)
        s0 = s0_ref[:, h].reshape(SD, GLA_DV)
        o_h = o[:, vs] + _dot(qx[:, xs], s0.astype(BF16))
        outs.append(_rms(o_h))
        v_hi, v_lo = _split(v[:, vs])
        upd = _dot_tn(kx_hi[:, xs], v_hi) + _dot_tn(kx_lo[:, xs], v_hi) + _dot_tn(kx_hi[:, xs], v_lo)
        decay = _dot_tn(ax[0][:, xs], last) + _dot_tn(ax[1][:, xs], last) + _dot_tn(ax[2][:, xs], last)
        snew_ref[:, h] = (decay * s0 + upd).reshape(nseq, GLA_DK, GLA_DV)
    g = g_ref[...]
    o_ref[...] = jnp.concatenate(outs, axis=-1) * gn_ref[...] * (g * jax.nn.sigmoid(g))


def _gla_sample(p2, s0, seq_len, e2, rep, gn):
    n_seq = s0.shape[0]
    R = SAMPLE_SEQ_BLOCK * seq_len
    nblk = n_seq // SAMPLE_SEQ_BLOCK

    def tok(width, colblk):
        return pl.BlockSpec((R, width), lambda i: (i, colblk))

    st_spec = pl.BlockSpec((SAMPLE_SEQ_BLOCK, GLA_HEADS, GLA_DK, GLA_DV), lambda i: (i, 0, 0, 0))
    return pl.pallas_call(
        functools.partial(_gla_sample_kernel, seq_len=seq_len),
        grid=(nblk,),
        in_specs=[tok(GLA_QK, P_Q // GLA_QK), tok(GLA_QK, P_K // GLA_QK),
                  tok(GLA_WIDTH, P_V // GLA_WIDTH), tok(GLA_WIDTH, P_G // GLA_WIDTH),
                  tok(GLA_QK, P_LA // GLA_QK), st_spec,
                  _const_spec(e2.shape), _const_spec(rep.shape), _const_spec(gn.shape)],
        out_specs=[pl.BlockSpec((R, GLA_WIDTH), lambda i: (i, 0)), st_spec],
        out_shape=(jax.ShapeDtypeStruct((n_seq * seq_len, GLA_WIDTH), F32),
                   jax.ShapeDtypeStruct(s0.shape, F32)),
        compiler_params=_params(("parallel",)),
        name="gla_sample",
    )(p2, p2, p2, p2, p2, s0, e2, rep, gn)


def _s5_local_scan(bur, bui, pwre_ref, pwim_ref, lanes, group):
    rows, width = bur.shape
    xr = bur.reshape(rows // 8, 8, width)
    xi = bui.reshape(rows // 8, 8, width)
    sub = lax.broadcasted_iota(jnp.int32, (8, width), 0) % group
    s = 1
    while s < group:
        ar = jnp.where(sub >= s, pwre_ref[s - 1:s, lanes], 0.0)[None]
        ai = jnp.where(sub >= s, pwim_ref[s - 1:s, lanes], 0.0)[None]
        sr = pltpu.roll(xr, s, 1)
        si = pltpu.roll(xi, s, 1)
        xr, xi = xr + ar * sr - ai * si, xi + ar * si + ai * sr
        s *= 2
    return xr.reshape(rows, width), xi.reshape(rows, width)


def _s5_tail(ys, u, d_ref, wglu_ref, bglu_ref):
    y = jnp.concatenate(ys, axis=-1) + d_ref[...] * u
    z = jax.nn.gelu(y)
    return z * jax.nn.sigmoid(_dot(z.astype(BF16), wglu_ref[...]) + bglu_ref[...])


def _s5_prompt_kernel(u_ref, brh_ref, brl_ref, bih_ref, bil_ref, cre_ref, cim_ref,
                      pwre_ref, pwim_ref, d_ref, wglu_ref, bglu_ref,
                      o_ref, stre_ref, stim_ref, hre_s, him_s, car_re, car_im):
    t = pl.program_id(1)
    TT = u_ref.shape[0]

    @pl.when(t == 0)
    def _():
        car_re[...] = jnp.zeros_like(car_re)
        car_im[...] = jnp.zeros_like(car_im)

    u = u_ref[...]
    ys = []
    for l in range(S5_NBLK):
        lanes = slice(l * S5_BLK_ST, (l + 1) * S5_BLK_ST)
        ul = u[:, l * S5_BLK_CH:(l + 1) * S5_BLK_CH]
        bur = _dot3(ul, brh_ref[l], brl_ref[l])
        bui = _dot3(ul, bih_ref[l], bil_ref[l])
        bur, bui = _s5_local_scan(bur, bui, pwre_ref, pwim_ref, lanes, 8)
        hre_s[...] = bur
        him_s[...] = bui
        p8r = pwre_ref[:, lanes]
        p8i = pwim_ref[:, lanes]

        def grp(r, carry):
            cr, ci = carry
            off = pl.multiple_of(r * 8, 8)
            xr = hre_s[pl.ds(off, 8), :] + p8r * cr - p8i * ci
            xi = him_s[pl.ds(off, 8), :] + p8r * ci + p8i * cr
            hre_s[pl.ds(off, 8), :] = xr
            him_s[pl.ds(off, 8), :] = xi
            return (jnp.broadcast_to(xr[7:8], xr.shape), jnp.broadcast_to(xi[7:8], xi.shape))

        cr0 = jnp.broadcast_to(car_re[:, lanes], (8, S5_BLK_ST))
        ci0 = jnp.broadcast_to(car_im[:, lanes], (8, S5_BLK_ST))
        cr, ci = lax.fori_loop(0, TT // 8, grp, (cr0, ci0))
        car_re[:, lanes] = cr[0:1]
        car_im[:, lanes] = ci[0:1]
        ys.append(_dot(hre_s[...].astype(BF16), cre_ref[l]) - _dot(him_s[...].astype(BF16), cim_ref[l]))
    o_ref[...] = _s5_tail(ys, u, d_ref, wglu_ref, bglu_ref)
    stre_ref[0] = car_re[...]
    stim_ref[0] = car_im[...]


def _s5_sample_kernel(u_ref, s0re_ref, s0im_ref, brh_ref, brl_ref, bih_ref, bil_ref, cre_ref, cim_ref,
                      pwre_ref, pwim_ref, d_ref, wglu_ref, bglu_ref,
                      o_ref, hre_ref, him_ref, *, seq_len):
    u = u_ref[...]
    ys = []
    for l in range(S5_NBLK):
        lanes = slice(l * S5_BLK_ST, (l + 1) * S5_BLK_ST)
        ul = u[:, l * S5_BLK_CH:(l + 1) * S5_BLK_CH]
        ar = pwre_ref[0:1, lanes]
        ai = pwim_ref[0:1, lanes]
        sr = s0re_ref[:, lanes]
        si = s0im_ref[:, lanes]
        bur = _dot3(ul, brh_ref[l], brl_ref[l]) + (ar * sr - ai * si)
        bui = _dot3(ul, bih_ref[l], bil_ref[l]) + (ar * si + ai * sr)
        bur, bui = _s5_local_scan(bur, bui, pwre_ref, pwim_ref, lanes, seq_len)
        hre_ref[:, lanes] = bur
        him_ref[:, lanes] = bui
        ys.append(_dot(bur.astype(BF16), cre_ref[l]) - _dot(bui.astype(BF16), cim_ref[l]))
    o_ref[...] = _s5_tail(ys, u, d_ref, wglu_ref, bglu_ref)


def _s5_weight_specs(ws):
    return [_const_spec(w.shape) for w in ws]


def _s5_prompt(p2, n_seq, seq_len, ws):
    TT = TOK_TILE
    nt = seq_len // TT
    rows = n_seq * seq_len
    st_spec = pl.BlockSpec((1, 1, S5_LANES), lambda b, t: (b, 0, 0))
    return pl.pallas_call(
        _s5_prompt_kernel,
        grid=(n_seq, nt),
        in_specs=[pl.BlockSpec((TT, S5_WIDTH), lambda b, t: (b * nt + t, P_U // S5_WIDTH))]
        + _s5_weight_specs(ws),
        out_specs=[pl.BlockSpec((TT, S5_WIDTH), lambda b, t: (b * nt + t, 0)), st_spec, st_spec],
        out_shape=(jax.ShapeDtypeStruct((rows, S5_WIDTH), F32),
                   jax.ShapeDtypeStruct((n_seq, 1, S5_LANES), F32),
                   jax.ShapeDtypeStruct((n_seq, 1, S5_LANES), F32)),
        scratch_shapes=[pltpu.VMEM((TT, S5_BLK_ST), F32), pltpu.VMEM((TT, S5_BLK_ST), F32),
                        pltpu.VMEM((1, S5_LANES), F32), pltpu.VMEM((1, S5_LANES), F32)],
        compiler_params=_params(("parallel", "arbitrary")),
        name="s5_prompt",
    )(p2, *ws)


def _s5_sample(p2, s0re_rows, s0im_rows, seq_len, ws):
    rows = s0re_rows.shape[0]
    TT = TOK_TILE
    row_spec = pl.BlockSpec((TT, S5_LANES), lambda i: (i, 0))
    return pl.pallas_call(
        functools.partial(_s5_sample_kernel, seq_len=seq_len),
        grid=(rows // TT,),
        in_specs=[pl.BlockSpec((TT, S5_WIDTH), lambda i: (i, P_U // S5_WIDTH)), row_spec, row_spec]
        + _s5_weight_specs(ws),
        out_specs=[pl.BlockSpec((TT, S5_WIDTH), lambda i: (i, 0)), row_spec, row_spec],
        out_shape=(jax.ShapeDtypeStruct((rows, S5_WIDTH), F32),
                   jax.ShapeDtypeStruct((rows, S5_LANES), F32),
                   jax.ShapeDtypeStruct((rows, S5_LANES), F32)),
        compiler_params=_params(("parallel",)),
        name="s5_sample",
    )(p2, s0re_rows, s0im_rows, *ws)


def _outproj_kernel(x_ref, og_ref, os_ref, wt_ref, wb_ref, n2_ref,
                    x1_ref, h2_ref):
    x1 = (x_ref[...] + _dot(og_ref[...].astype(BF16), wt_ref[...])
          + _dot(os_ref[...].astype(BF16), wb_ref[...]))
    x1_ref[...] = x1
    h2_ref[...] = _rms(x1) * n2_ref[...]


def _out_proj(x2, og, osx, wt, wb, n2):
    rows = x2.shape[0]
    big = pl.BlockSpec((TOK_TILE, D_MODEL), lambda i: (i, 0))
    half = pl.BlockSpec((TOK_TILE, GLA_WIDTH), lambda i: (i, 0))
    return pl.pallas_call(
        _outproj_kernel,
        grid=(rows // TOK_TILE,),
        in_specs=[big, half, half] + [_const_spec(w.shape) for w in (wt, wb, n2)],
        out_specs=[big, big],
        out_shape=(jax.ShapeDtypeStruct((rows, D_MODEL), F32),) * 2,
        compiler_params=_params(("parallel",)),
        name="out_proj",
    )(x2, og, osx, wt, wb, n2)


def _extract_top(work, n, store=None, want_rank=False):
    first = None
    m = None
    rank = jnp.full(work.shape, float(n), F32) if want_rank else None
    for r in range(n):
        m = jnp.max(work, axis=0, keepdims=True)
        if r == 0:
            first = m
        if store is not None:
            store(r, m)
        hit = work == m
        if want_rank:
            rank = jnp.where(hit, float(r), rank)
        if r < n - 1:
            work = jnp.where(hit, -jnp.inf, work)
    return first, m, rank


def _route_kernel(h2_ref, wq_ref, kh_ref, kl_ref,
                  n1_ref, e1_ref, rho_ref, e2_ref, s_s, rank_s, top_s):
    TM = h2_ref.shape[0]
    qp = _dot(h2_ref[...].astype(BF16), wq_ref[...])
    for hp in range(2 * PEER_HEADS):
        q_hi, q_lo = _split(qp[:, hp * PEER_KEYS:(hp + 1) * PEER_KEYS])
        s_s[hp] = _dot_nt(kh_ref[hp], q_hi) + _dot_nt(kl_ref[hp], q_hi) + _dot_nt(kh_ref[hp], q_lo)
    K = PEER_TOPK
    for lc in range(TM // LANES):
        sl = slice(lc * LANES, (lc + 1) * LANES)
        for hp in range(2 * PEER_HEADS):
            def store(r, m, hp=hp):
                top_s[hp, r:r + 1, sl] = m

            _, _, rank = _extract_top(s_s[hp, :, sl], K, store, want_rank=(hp % 2 == 1))
            if hp % 2 == 1:
                rank_s[hp // 2, :, sl] = rank
        for h in range(PEER_HEADS):
            v1 = top_s[2 * h, :, sl]
            v2 = top_s[2 * h + 1, :, sl]
            cands = [v1 + v2[0:1]] + [v1[0:8] + v2[b:b + 1] for b in range(1, 8)] + [v2[8:K] + v1[0:1]]
            cand = jnp.concatenate(cands, axis=0)
            mx, thr, _ = _extract_top(cand, K)
            z = jnp.sum(jnp.where(cand >= thr, jnp.exp(cand - mx), 0.0), axis=0, keepdims=True)
            cnt = jnp.zeros(v1.shape, F32)
            for b in range(K):
                cnt = cnt + jnp.where(v1 + v2[b:b + 1] >= thr, 1.0, 0.0)
            s1 = s_s[2 * h, :, sl]
            n1 = jnp.zeros(s1.shape, F32)
            for a in range(K):
                n1 = jnp.where(s1 == v1[a:a + 1], cnt[a:a + 1], n1)
            n1_ref[h, :, sl] = n1
            e1_ref[h, :, sl] = jnp.exp(s1 - v1[0:1])
            rho_ref[h, :, sl] = rank_s[h, :, sl].astype(BF16)
            e2_ref[h, :, sl] = (jnp.exp(s_s[2 * h + 1, :, sl] - v2[0:1]) * (1.0 / z)).astype(BF16)


def _route(h2, wq_bf, kh, kl):
    rows = h2.shape[0]
    TM = TOK_TILE
    key_spec = pl.BlockSpec((PEER_HEADS, PEER_KEYS, TM), lambda i: (0, 0, i))
    return pl.pallas_call(
        _route_kernel,
        grid=(rows // TM,),
        in_specs=[pl.BlockSpec((TM, D_MODEL), lambda i: (i, 0)),
                  _const_spec(wq_bf.shape),
                  _const_spec(kh.shape), _const_spec(kl.shape)],
        out_specs=[key_spec] * 4,
        out_shape=(jax.ShapeDtypeStruct((PEER_HEADS, PEER_KEYS, rows), F32),
                   jax.ShapeDtypeStruct((PEER_HEADS, PEER_KEYS, rows), F32),
                   jax.ShapeDtypeStruct((PEER_HEADS, PEER_KEYS, rows), BF16),
                   jax.ShapeDtypeStruct((PEER_HEADS, PEER_KEYS, rows), BF16)),
        scratch_shapes=[pltpu.VMEM((2 * PEER_HEADS, PEER_KEYS, TM), F32),
                        pltpu.VMEM((PEER_HEADS, PEER_KEYS, TM), F32),
                        pltpu.VMEM((2 * PEER_HEADS, PEER_TOPK, TM), F32)],
        compiler_params=_params(("parallel",)),
        name="peer_route",
    )(h2, wq_bf, kh, kl)


PEER_CHUNK = 256


def _peer_gate_piece(i1, c, a, act_ref, wact_ref, n1_ref, e1_ref, rho_ref, e2_ref):
    ls = slice(c * PEER_CHUNK, (c + 1) * PEER_CHUNK)
    rs = slice(a * PEER_KEYS, (a + 1) * PEER_KEYS)
    w = jnp.zeros((PEER_KEYS, PEER_CHUNK), BF16)
    for h in range(PEER_HEADS):
        n1 = n1_ref[h, pl.ds(i1, 1), ls].astype(BF16)
        g1 = e1_ref[h, pl.ds(i1, 1), ls].astype(BF16)
        e2 = e2_ref[h, :, ls]
        w = w + jnp.where(rho_ref[h, :, ls] < n1, e2, jnp.zeros_like(e2)) * g1
    x = act_ref[rs, ls]
    t = jnp.exp(x * (-2.0 * GELU_C - (2.0 * GELU_C * GELU_A) * (x * x)))
    wact_ref[rs, ls] = w * (x / (1.0 + t)).astype(BF16)


def _peer_half(blk_gate, u_half, vt_half, act_in, wact_out, wact_in, act_out,
               acc, h2b, n1_ref, e1_ref, rho_ref, e2_ref):
    sub = act_in.shape[0] // PEER_KEYS
    for c in range(act_in.shape[1] // PEER_CHUNK):
        ls = slice(c * PEER_CHUNK, (c + 1) * PEER_CHUNK)
        acc[:, ls] += _dot(vt_half, wact_in[:, ls])
        for a in range(sub):
            _peer_gate_piece(blk_gate * sub + a, c, a, act_in, wact_out, n1_ref, e1_ref, rho_ref, e2_ref)
        act_out[:, ls] = _dot_nt(u_half, h2b[c * PEER_CHUNK:(c + 1) * PEER_CHUNK, :])


def _peer_kernel(h2_ref, x1_ref, n1_ref, e1_ref, rho_ref, e2_ref, u_ref, vt_ref, fn_ref, y_ref,
                 acc, h2b, wact0, wact1, act0, act1, *, n_steps):
    g = pl.program_id(1)
    EB = act0.shape[0]
    n_blk = 2 * (n_steps - 1)

    @pl.when(g == 0)
    def _():
        acc[...] = jnp.zeros_like(acc)
        act1[...] = jnp.zeros_like(act1)
        wact0[...] = jnp.zeros_like(wact0)
        h2b[...] = h2_ref[...].astype(BF16)

    refs = (acc, h2b, n1_ref, e1_ref, rho_ref, e2_ref)
    _peer_half(jnp.clip(2 * g - 1, 0, n_blk - 1), u_ref[0:EB, :], vt_ref[:, 0:EB],
               act1, wact1, wact0, act0, *refs)
    _peer_half(jnp.minimum(2 * g, n_blk - 1), u_ref[EB:2 * EB, :], vt_ref[:, EB:2 * EB],
               act0, wact0, wact1, act1, *refs)

    @pl.when(g == n_steps - 1)
    def _():
        out = x1_ref[...] + acc[...].T
        y_ref[...] = _rms(out) * fn_ref[...]


def _peer(h2, x1, n1, e1, rho, e2, u_bf, vt_bf, fn):
    rows = h2.shape[0]
    TM = PEER_TOK_TILE
    EB = PEER_EXP_TILE
    n_steps = PEER_EXPERTS // (2 * EB) + 1
    big = pl.BlockSpec((TM, D_MODEL), lambda i, g: (i, 0))
    key_spec = pl.BlockSpec((PEER_HEADS, PEER_KEYS, TM), lambda i, g: (0, 0, i))
    return pl.pallas_call(
        functools.partial(_peer_kernel, n_steps=n_steps),
        grid=(rows // TM, n_steps),
        in_specs=[big, big, key_spec, key_spec, key_spec, key_spec,
                  pl.BlockSpec((2 * EB, D_MODEL), lambda i, g: (jnp.minimum(g, n_steps - 2), 0)),
                  pl.BlockSpec((D_MODEL, 2 * EB), lambda i, g: (0, jnp.maximum(g - 1, 0))),
                  _const_spec(fn.shape)],
        out_specs=big,
        out_shape=jax.ShapeDtypeStruct((rows, D_MODEL), F32),
        scratch_shapes=[pltpu.VMEM((D_MODEL, TM), F32), pltpu.VMEM((TM, D_MODEL), BF16),
                        pltpu.VMEM((EB, TM), BF16), pltpu.VMEM((EB, TM), BF16),
                        pltpu.VMEM((EB, TM), F32), pltpu.VMEM((EB, TM), F32)],
        compiler_params=_params(("parallel", "arbitrary")),
        name="peer_experts",
    )(h2, x1, n1, e1, rho, e2, u_bf, vt_bf, fn)


def _block_diag(w, eye):
    n, g, a, b = w.shape
    return jnp.einsum('lgab,gh->lgahb', w, eye).reshape(n, g * a, g * b)


def _gla_constants():
    lane_head = jnp.arange(GLA_QK) // GLA_DK
    e2 =(lane_head[:, None] == (jnp.arange(GLA_WIDTH) // GLA_DV)[None, :]).astype(BF16)
    sd = SAMPLE_SEQ_BLOCK * GLA_DK
    xc = jnp.arange(GLA_HEADS * sd)
    rep = ((lane_head[:, None] == (xc // sd)[None, :])
           & ((jnp.arange(GLA_QK) % GLA_DK)[:, None] == (xc % GLA_DK)[None, :])).astype(BF16)
    return e2, rep


def kernel(x_prompt, x_sample, state_gla, state_s5_re, state_s5_im, norm1, w_in, w_a2, b_a2, gla_norm, s5_lam_re, s5_lam_im, s5_log_dt, s5_b_re, s5_b_im, s5_c_re, s5_c_im, s5_d, w_glu, b_glu, w_out, norm2, peer_wq, peer_keys, peer_u, peer_v, final_norm):
    depth = norm1.shape[0]
    assert depth == 1, "single-layer trunk"
    n_p, len_p, _ = x_prompt.shape
    n_s, len_s, _ = x_sample.shape
    l = 0

    w = w_in[l]
    w_re = jnp.concatenate([w[:, 0:1536], w[:, 1552:2064], w[:, 1536:1552],
                            jnp.zeros((D_MODEL, W_IN_COLS - 2064), F32)], axis=1)
    w_bf = w_re.astype(BF16)
    a2 = jnp.concatenate([w_a2[l], jnp.zeros((W_IN_COLS - P_LA - GLA_RANK, GLA_QK), F32)], axis=0)
    a2hi, a2lo = _split(a2)
    ba2 = b_a2[l].reshape(1, GLA_QK)
    n1 = norm1[l].reshape(1, D_MODEL)
    gn = gla_norm[l].reshape(1, GLA_WIDTH)
    e2, rep = _gla_constants()

    pwre, pwim, bbre, bbim = _s5_prep(s5_lam_re[l], s5_lam_im[l], s5_log_dt[l], s5_b_re[l], s5_b_im[l])
    eye = jnp.eye(8, dtype=F32)
    blk = lambda t: t.reshape(S5_NBLK, 8, S5_GROUP_CH, S5_STATE)
    brh, brl = _split(_block_diag(blk(bbre), eye))
    bih, bil = _split(_block_diag(blk(bbim), eye))
    cre = _block_diag(jnp.swapaxes(blk(s5_c_re[l]), 2, 3), eye).astype(BF16)
    cim = _block_diag(jnp.swapaxes(blk(s5_c_im[l]), 2, 3), eye).astype(BF16)
    s5w = (brh, brl, bih, bil, cre, cim, pwre.reshape(8, S5_LANES), pwim.reshape(8, S5_LANES),
           s5_d[l].reshape(1, S5_WIDTH), w_glu[l].astype(BF16), b_glu[l].reshape(1, S5_WIDTH))

    wt = w_out[l][:GLA_WIDTH].astype(BF16)
    wb = w_out[l][GLA_WIDTH:].astype(BF16)
    n2 = norm2[l].reshape(1, D_MODEL)
    wq_bf = peer_wq[l].astype(BF16)
    kh, kl = _split(peer_keys[l].reshape(2 * PEER_HEADS, PEER_KEYS, PEER_DQ // 2))
    u_bf = peer_u[l].astype(BF16)
    vt_bf = peer_v[l].astype(BF16).T
    fn = final_norm.reshape(1, D_MODEL)

    def tail(x2, og, osx):
        x1, h2 = _out_proj(x2, og, osx, wt, wb, n2)
        n1, e1, rho, e2g = _route(h2, wq_bf, kh, kl)
        return _peer(h2, x1, n1, e1, rho, e2g, u_bf, vt_bf, fn)

    xp = x_prompt.reshape(n_p * len_p, D_MODEL)
    pp = _in_proj(xp, n1, w_bf, a2hi, a2lo, ba2)
    og_p, gla_p = _gla_prompt(pp, n_p, len_p, e2, gn)
    os_p, sre_p, sim_p = _s5_prompt(pp, n_p, len_p, s5w)
    y_p = tail(xp, og_p, os_p).reshape(n_p, len_p, D_MODEL)

    xs = x_sample.reshape(n_s * len_s, D_MODEL)
    ps = _in_proj(xs, n1, w_bf, a2hi, a2lo, ba2)
    og_s, gla_s = _gla_sample(ps, state_gla[l], len_s, e2, rep, gn)
    first_row = lambda s: jnp.pad(s.reshape(n_s, 1, S5_LANES), ((0, 0), (0, len_s - 1), (0, 0))
                                  ).reshape(n_s * len_s, S5_LANES)
    os_s, hre_s, him_s = _s5_sample(ps, first_row(state_s5_re[l]), first_row(state_s5_im[l]), len_s, s5w)
    y_s = tail(xs, og_s, os_s).reshape(n_s, len_s, D_MODEL)
    last_row = lambda hs: hs.reshape(n_s, len_s, S5_GROUPS, S5_STATE)[:, len_s - 1]

    st = lambda a: a.reshape(1, n_p, S5_GROUPS, S5_STATE)
    return (y_p, y_s, gla_p[None], st(sre_p), st(sim_p),
            gla_s[None], last_row(hre_s)[None], last_row(him_s)[None])
```

```python
import functools

import jax
import jax.numpy as jnp
from jax import lax
from jax.experimental import pallas as pl
from jax.experimental.pallas import tpu as pltpu

F32 = jnp.float32
BF16 = jnp.bfloat16

D_MODEL = 1024
GLA_HEADS = 4
GLA_DK = 64
GLA_DV = 128
GLA_QK = GLA_HEADS * GLA_DK
GLA_WIDTH = GLA_HEADS * GLA_DV
GLA_RANK = 16
GLA_TAU = 16.0
GLA_CHUNK = 64
S5_WIDTH = 512
S5_GROUP_CH = 16
S5_GROUPS = 32
S5_STATE = 64
S5_LANES = S5_GROUPS * S5_STATE
S5_NBLK = 4
S5_BLK_CH = S5_WIDTH // S5_NBLK
S5_BLK_ST = S5_LANES // S5_NBLK
PEER_KEYS = 128
PEER_EXPERTS = PEER_KEYS * PEER_KEYS
PEER_HEADS = 8
PEER_DQ = 256
PEER_TOPK = 16
EPS = 1e-6
LANES = 128
GELU_C = 0.7978845608028654
GELU_A = 0.044715

P_Q, P_K, P_V, P_G, P_U, P_LA = 0, 256, 512, 1024, 1536, 2048
P_COLS = 2304
W_IN_COLS = 2176

TOK_TILE = 256
PROJ_TILE = 512
PEER_TOK_TILE = 512
PEER_EXP_TILE = 512
SAMPLE_SEQ_BLOCK = 16
VMEM_LIMIT = 56 * 1024 * 1024


def _split(x):
    hi = x.astype(BF16)
    lo = (x - hi.astype(F32)).astype(BF16)
    return hi, lo


def _split3(x):
    a = x.astype(BF16)
    r = x - a.astype(F32)
    b = r.astype(BF16)
    c = (r - b.astype(F32)).astype(BF16)
    return a, b, c


def _dot(a, b):
    return jnp.dot(a, b, preferred_element_type=F32)


def _dot_nt(a, b):
    return lax.dot_general(a, b, (((1,), (1,)), ((), ())), preferred_element_type=F32)


def _dot_tn(a, b):
    return lax.dot_general(a, b, (((0,), (0,)), ((), ())), preferred_element_type=F32)


def _dot3(a, b_hi, b_lo):
    a_hi, a_lo = _split(a)
    return _dot(a_hi, b_hi) + _dot(a_lo, b_hi) + _dot(a_hi, b_lo)


def _dot_exact01(m01, x):
    a, b, c = _split3(x)
    return _dot(m01, a) + _dot(m01, b) + _dot(m01, c)


def _rms(x):
    return x * lax.rsqrt(jnp.mean(x * x, axis=-1, keepdims=True) + EPS)


def _params(sem):
    return pltpu.CompilerParams(dimension_semantics=sem, vmem_limit_bytes=VMEM_LIMIT)


def _const_spec(shape):
    n = len(shape)
    return pl.BlockSpec(shape, lambda *_: (0,) * n)


def _s5prep_kernel(lr_ref, li_ref, ldt_ref, bret_ref, bimt_ref,
                   pwre_ref, pwim_ref, bbre_ref, bbim_ref):
    lr = lr_ref[...]
    li = li_ref[...]
    dt = jnp.exp(ldt_ref[...])
    mag = jnp.exp(lr * dt)
    abr = mag * jnp.cos(li * dt)
    abi = mag * jnp.sin(li * dt)
    den = lr * lr + li * li
    nr = abr - 1.0
    ni = abi
    fr = (nr * lr + ni * li) / den
    fi = (ni * lr - nr * li) / den
    bret = bret_ref[...]
    bimt = bimt_ref[...]
    bbre_ref[...] = fr[:, None, :] * bret - fi[:, None, :] * bimt
    bbim_ref[...] = fr[:, None, :] * bimt + fi[:, None, :] * bret
    pr, pi = abr, abi
    for i in range(8):
        pwre_ref[i] = pr
        pwim_ref[i] = pi
        pr, pi = pr * abr - pi * abi, pr * abi + pi * abr


def _s5_prep(lam_re, lam_im, log_dt, b_re, b_im):
    g, p = lam_re.shape
    ch = b_re.shape[-1]
    bret = jnp.transpose(b_re, (0, 2, 1))
    bimt = jnp.transpose(b_im, (0, 2, 1))
    out = pl.pallas_call(
        _s5prep_kernel,
        out_shape=(jax.ShapeDtypeStruct((8, g, p), F32), jax.ShapeDtypeStruct((8, g, p), F32),
                   jax.ShapeDtypeStruct((g, ch, p), F32), jax.ShapeDtypeStruct((g, ch, p), F32)),
        name="s5_prep",
    )(lam_re, lam_im, log_dt.reshape(g, 1), bret, bimt)
    return out


def _inproj_kernel(x_ref, n1_ref, w_ref, a2hi_ref, a2lo_ref, ba2_ref, p_ref):
    h = _rms(x_ref[...]) * n1_ref[...]
    p = _dot(h.astype(BF16), w_ref[...])
    alr = p[:, P_LA:W_IN_COLS]
    z = _dot3(alr, a2hi_ref[...], a2lo_ref[...]) + ba2_ref[...]
    log_sig = jnp.minimum(z, 0.0) - jnp.log1p(jnp.exp(-jnp.abs(z)))
    p_ref[:, 0:P_LA] = p[:, 0:P_LA]
    p_ref[:, P_LA:P_COLS] = log_sig * (1.0 / GLA_TAU)


def _in_proj(x2, n1, w_bf, a2hi, a2lo, ba2):
    rows = x2.shape[0]
    return pl.pallas_call(
        _inproj_kernel,
        grid=(rows // PROJ_TILE,),
        in_specs=[pl.BlockSpec((PROJ_TILE, D_MODEL), lambda i: (i, 0)),
                  _const_spec(n1.shape), _const_spec(w_bf.shape),
                  _const_spec(a2hi.shape), _const_spec(a2lo.shape), _const_spec(ba2.shape)],
        out_specs=pl.BlockSpec((PROJ_TILE, P_COLS), lambda i: (i, 0)),
        out_shape=jax.ShapeDtypeStruct((rows, P_COLS), F32),
        compiler_params=_params(("parallel",)),
        name="in_proj",
    )(x2, n1, w_bf, a2hi, a2lo, ba2)


def _cumsum_rows(mask01, la):
    return _dot_exact01(mask01.astype(BF16), la)


GLA_BAND = 4
SUBLANES = 8


def _gla_prompt_kernel(q_ref, k_ref, v_ref, g_ref, la_ref, e2_ref, gn_ref,
                       o_ref, sfin_ref, st_ref, slab, *, n_chunks):
    c = pl.program_id(1)
    C = q_ref.shape[0]
    HC = GLA_HEADS * C

    @pl.when(c == 0)
    def _():
        st_ref[...] = jnp.zeros_like(st_ref)

    la = la_ref[...]
    row = lax.broadcasted_iota(jnp.int32, (C, C), 0)
    col = lax.broadcasted_iota(jnp.int32, (C, C), 1)
    b = _cumsum_rows(col <= row, la)
    q = q_ref[...] * (GLA_DK ** -0.5)
    k = k_ref[...]
    v = v_ref[...]
    blast = b[C - 1:C, :]

    rloc = lax.broadcasted_iota(jnp.int32, (C, GLA_QK), 0) % GLA_BAND
    tiles = lambda t: t.reshape(C // SUBLANES, SUBLANES, t.shape[-1])
    k3, b3, v3 = tiles(k), tiles(b), tiles(v)
    vsh = [v]
    for d in range(GLA_BAND):
        if d == 0:
            ks_, bs_ = k, b
        else:
            ks_ = pltpu.roll(k3, d, 1).reshape(C, GLA_QK)
            bs_ = pltpu.roll(b3, d, 1).reshape(C, GLA_QK)
            vsh.append(pltpu.roll(v3, d, 1).reshape(C, GLA_WIDTH))
        m = q * ks_ * jnp.exp(jnp.minimum(b - bs_, 0.0))
        slab[d * C:(d + 1) * C, :] = jnp.where(rloc >= d, m, 0.0).astype(BF16)
    rep = _dot(slab[...], e2_ref[...])
    o = rep[0:C] * vsh[0]
    for d in range(1, GLA_BAND):
        o = o + rep[d * C:(d + 1) * C] * vsh[d]

    lane_head = lax.broadcasted_iota(jnp.int32, (HC, GLA_QK), 1) // GLA_DK
    row_head = lax.broadcasted_iota(jnp.int32, (HC, GLA_QK), 0) // C
    own_head = lane_head == row_head
    si = lax.broadcasted_iota(jnp.int32, (HC, C), 0) % C
    sj = lax.broadcasted_iota(jnp.int32, (HC, C), 1)
    scores = jnp.zeros((HC, C), F32)
    s = C // 2
    while s >= GLA_BAND:
        ref = jnp.concatenate([jnp.broadcast_to(b[p * 2 * s + s - 1:p * 2 * s + s, :], (2 * s, GLA_QK))
                               for p in range(C // (2 * s))], axis=0)
        ql = q * jnp.exp(jnp.minimum(b - ref, 0.0))
        kl = (k * jnp.exp(jnp.minimum(ref - b, 0.0))).astype(BF16)
        qs = jnp.where(own_head, jnp.concatenate([ql] * GLA_HEADS, axis=0), 0.0).astype(BF16)
        lvl = ((si // (2 * s)) == (sj // (2 * s))) & ((si // s) % 2 == 1) & ((sj // s) % 2 == 0)
        scores = scores + jnp.where(lvl, _dot_nt(qs, kl), 0.0)
        s //= 2
    scores = scores.astype(BF16)
    vb = v.astype(BF16)

    st = st_ref[...]
    o = o + _dot_nt((q * jnp.exp(b)).astype(BF16), st.astype(BF16))
    outs = []
    for h in range(GLA_HEADS):
        vs = slice(h * GLA_DV, (h + 1) * GLA_DV)
        outs.append(_rms(o[:, vs] + _dot(scores[h * C:(h + 1) * C], vb[:, vs])))
    g = g_ref[...]
    o_ref[...] = jnp.concatenate(outs, axis=-1) * gn_ref[...] * (g * jax.nn.sigmoid(g))

    kd_hi, kd_lo = _split(k * jnp.exp(blast - b))
    vt_hi, vt_lo = _split(v.T)
    upd = _dot(vt_hi, kd_hi) + _dot(vt_lo, kd_hi) + _dot(vt_hi, kd_lo)
    blk = (lax.broadcasted_iota(jnp.int32, st.shape, 0) // GLA_DV
           == lax.broadcasted_iota(jnp.int32, st.shape, 1) // GLA_DK)
    st_new = jnp.exp(blast) * st + jnp.where(blk, upd, 0.0)
    st_ref[...] = st_new

    @pl.when(c == n_chunks - 1)
    def _():
        for h in range(GLA_HEADS):
            sfin_ref[0, h] = st_new[h * GLA_DV:(h + 1) * GLA_DV, h * GLA_DK:(h + 1) * GLA_DK].T


def _gla_prompt(p2, n_seq, seq_len, e2, gn):
    C = GLA_CHUNK
    nch = seq_len // C
    rows = n_seq * seq_len

    def tok(width, colblk):
        return pl.BlockSpec((C, width), lambda b, c: (b * nch + c, colblk))

    return pl.pallas_call(
        functools.partial(_gla_prompt_kernel, n_chunks=nch),
        grid=(n_seq, nch),
        in_specs=[tok(GLA_QK, P_Q // GLA_QK), tok(GLA_QK, P_K // GLA_QK),
                  tok(GLA_WIDTH, P_V // GLA_WIDTH), tok(GLA_WIDTH, P_G // GLA_WIDTH),
                  tok(GLA_QK, P_LA // GLA_QK),
                  _const_spec(e2.shape), _const_spec(gn.shape)],
        out_specs=[pl.BlockSpec((C, GLA_WIDTH), lambda b, c: (b * nch + c, 0)),
                   pl.BlockSpec((1, GLA_HEADS, GLA_DK, GLA_DV), lambda b, c: (b, 0, 0, 0))],
        out_shape=(jax.ShapeDtypeStruct((rows, GLA_WIDTH), F32),
                   jax.ShapeDtypeStruct((n_seq, GLA_HEADS, GLA_DK, GLA_DV), F32)),
        scratch_shapes=[pltpu.VMEM((GLA_WIDTH, GLA_QK), F32),
                        pltpu.VMEM((GLA_BAND * C, GLA_QK), BF16)],
        compiler_params=_params(("parallel", "arbitrary")),
        name="gla_prompt",
    )(p2, p2, p2, p2, p2, e2, gn)


def _gla_sample_kernel(q_ref, k_ref, v_ref, g_ref, la_ref, s0_ref, e2_ref, rep_ref, gn_ref,
                       o_ref, snew_ref, *, seq_len):
    R = q_ref.shape[0]
    nseq = R // seq_len
    SD = nseq * GLA_DK
    la = la_ref[...]
    row = lax.broadcasted_iota(jnp.int32, (R, R), 0)
    col = lax.broadcasted_iota(jnp.int32, (R, R), 1)
    same = (col // seq_len) == (row // seq_len)
    b = _cumsum_rows(same & (col <= row), la)
    btot = _cumsum_rows(same, la)
    q = q_ref[...] * (GLA_DK ** -0.5)
    k = k_ref[...]
    v = v_ref[...]
    rmod = lax.broadcasted_iota(jnp.int32, (R, GLA_QK), 0) % seq_len

    o = jnp.zeros((R, GLA_WIDTH), F32)
    for d in range(seq_len):
        ks_, bs_, vs_ = (k, b, v) if d == 0 else (pltpu.roll(k, d, 0), pltpu.roll(b, d, 0),
                                                  pltpu.roll(v, d, 0))
        m = q * ks_ * jnp.exp(jnp.minimum(b - bs_, 0.0))
        m = jnp.where(rmod >= d, m, 0.0)
        o = o + _dot(m.astype(BF16), e2_ref[...]) * vs_

    xr = lax.broadcasted_iota(jnp.int32, (R, GLA_HEADS * SD), 0) // seq_len
    xc = (lax.broadcasted_iota(jnp.int32, (R, GLA_HEADS * SD), 1) % SD) // GLA_DK
    own = xr == xc
    rep = rep_ref[...]

    def expand(x_bf16):
        return jnp.where(own, _dot(x_bf16, rep), 0.0).astype(BF16)

    qx = expand((q * jnp.exp(b)).astype(BF16))
    kd_hi, kd_lo = _split(k * jnp.exp(btot - b))
    kx_hi, kx_lo = expand(kd_hi), expand(kd_lo)
    ea, eb, ec = _split3(jnp.exp(btot))
    ax = (expand(ea), expand(eb), expand(ec))
    last = (lax.broadcasted_iota(jnp.int32, (R, GLA_DV), 0) % seq_len == seq_len - 1).astype(BF16)

    outs = []
    for h in range(GLA_HEADS):
        xs = slice(h * SD, (h + 1) * SD)
        vs = slice(h * GLA_DV, (h + 1) * GLA_DV)
        s0 = s0_ref[:, h].reshape(SD, GLA_DV)
        o_h = o[:, vs] + _dot(qx[:, xs], s0.astype(BF16))
        outs.append(_rms(o_h))
        v_hi, v_lo = _split(v[:, vs])
        upd = _dot_tn(kx_hi[:, xs], v_hi) + _dot_tn(kx_lo[:, xs], v_hi) + _dot_tn(kx_hi[:, xs], v_lo)
        decay = _dot_tn(ax[0][:, xs], last) + _dot_tn(ax[1][:, xs], last) + _dot_tn(ax[2][:, xs], last)
        snew_ref[:, h] = (decay * s0 + upd).reshape(nseq, GLA_DK, GLA_DV)
    g = g_ref[...]
    o_ref[...] = jnp.concatenate(outs, axis=-1) * gn_ref[...] * (g * jax.nn.sigmoid(g))


def _gla_sample(p2, s0, seq_len, e2, rep, gn):
    n_seq = s0.shape[0]
    R = SAMPLE_SEQ_BLOCK * seq_len
    nblk = n_seq // SAMPLE_SEQ_BLOCK

    def tok(width, colblk):
        return pl.BlockSpec((R, width), lambda i: (i, colblk))

    st_spec = pl.BlockSpec((SAMPLE_SEQ_BLOCK, GLA_HEADS, GLA_DK, GLA_DV), lambda i: (i, 0, 0, 0))
    return pl.pallas_call(
        functools.partial(_gla_sample_kernel, seq_len=seq_len),
        grid=(nblk,),
        in_specs=[tok(GLA_QK, P_Q // GLA_QK), tok(GLA_QK, P_K // GLA_QK),
                  tok(GLA_WIDTH, P_V // GLA_WIDTH), tok(GLA_WIDTH, P_G // GLA_WIDTH),
                  tok(GLA_QK, P_LA // GLA_QK), st_spec,
                  _const_spec(e2.shape), _const_spec(rep.shape), _const_spec(gn.shape)],
        out_specs=[pl.BlockSpec((R, GLA_WIDTH), lambda i: (i, 0)), st_spec],
        out_shape=(jax.ShapeDtypeStruct((n_seq * seq_len, GLA_WIDTH), F32),
                   jax.ShapeDtypeStruct(s0.shape, F32)),
        compiler_params=_params(("parallel",)),
        name="gla_sample",
    )(p2, p2, p2, p2, p2, s0, e2, rep, gn)


def _s5_local_scan(bur, bui, pwre_ref, pwim_ref, lanes, group):
    rows, width = bur.shape
    xr = bur.reshape(rows // 8, 8, width)
    xi = bui.reshape(rows // 8, 8, width)
    sub = lax.broadcasted_iota(jnp.int32, (8, width), 0) % group
    s = 1
    while s < group:
        ar = jnp.where(sub >= s, pwre_ref[s - 1:s, lanes], 0.0)[None]
        ai = jnp.where(sub >= s, pwim_ref[s - 1:s, lanes], 0.0)[None]
        sr = pltpu.roll(xr, s, 1)
        si = pltpu.roll(xi, s, 1)
        xr, xi = xr + ar * sr - ai * si, xi + ar * si + ai * sr
        s *= 2
    return xr.reshape(rows, width), xi.reshape(rows, width)


def _s5_tail(ys, u, d_ref, wglu_ref, bglu_ref):
    y = jnp.concatenate(ys, axis=-1) + d_ref[...] * u
    z = jax.nn.gelu(y)
    return z * jax.nn.sigmoid(_dot(z.astype(BF16), wglu_ref[...]) + bglu_ref[...])


def _s5_prompt_kernel(u_ref, brh_ref, brl_ref, bih_ref, bil_ref, cre_ref, cim_ref,
                      pwre_ref, pwim_ref, d_ref, wglu_ref, bglu_ref,
                      o_ref, stre_ref, stim_ref, hre_s, him_s, car_re, car_im):
    t = pl.program_id(1)
    TT = u_ref.shape[0]

    @pl.when(t == 0)
    def _():
        car_re[...] = jnp.zeros_like(car_re)
        car_im[...] = jnp.zeros_like(car_im)

    u = u_ref[...]
    ys = []
    for l in range(S5_NBLK):
        lanes = slice(l * S5_BLK_ST, (l + 1) * S5_BLK_ST)
        ul = u[:, l * S5_BLK_CH:(l + 1) * S5_BLK_CH]
        bur = _dot3(ul, brh_ref[l], brl_ref[l])
        bui = _dot3(ul, bih_ref[l], bil_ref[l])
        bur, bui = _s5_local_scan(bur, bui, pwre_ref, pwim_ref, lanes, 8)
        hre_s[...] = bur
        him_s[...] = bui
        p8r = pwre_ref[:, lanes]
        p8i = pwim_ref[:, lanes]

        def grp(r, carry):
            cr, ci = carry
            off = pl.multiple_of(r * 8, 8)
            xr = hre_s[pl.ds(off, 8), :] + p8r * cr - p8i * ci
            xi = him_s[pl.ds(off, 8), :] + p8r * ci + p8i * cr
            hre_s[pl.ds(off, 8), :] = xr
            him_s[pl.ds(off, 8), :] = xi
            return (jnp.broadcast_to(xr[7:8], xr.shape), jnp.broadcast_to(xi[7:8], xi.shape))

        cr0 = jnp.broadcast_to(car_re[:, lanes], (8, S5_BLK_ST))
        ci0 = jnp.broadcast_to(car_im[:, lanes], (8, S5_BLK_ST))
        cr, ci = lax.fori_loop(0, TT // 8, grp, (cr0, ci0))
        car_re[:, lanes] = cr[0:1]
        car_im[:, lanes] = ci[0:1]
        ys.append(_dot(hre_s[...].astype(BF16), cre_ref[l]) - _dot(him_s[...].astype(BF16), cim_ref[l]))
    o_ref[...] = _s5_tail(ys, u, d_ref, wglu_ref, bglu_ref)
    stre_ref[0] = car_re[...]
    stim_ref[0] = car_im[...]


def _s5_sample_kernel(u_ref, s0re_ref, s0im_ref, brh_ref, brl_ref, bih_ref, bil_ref, cre_ref, cim_ref,
                      pwre_ref, pwim_ref, d_ref, wglu_ref, bglu_ref,
                      o_ref, hre_ref, him_ref, *, seq_len):
    u = u_ref[...]
    ys = []
    for l in range(S5_NBLK):
        lanes = slice(l * S5_BLK_ST, (l + 1) * S5_BLK_ST)
        ul = u[:, l * S5_BLK_CH:(l + 1) * S5_BLK_CH]
        ar = pwre_ref[0:1, lanes]
        ai = pwim_ref[0:1, lanes]
        sr = s0re_ref[:, lanes]
        si = s0im_ref[:, lanes]
        bur = _dot3(ul, brh_ref[l], brl_ref[l]) + (ar * sr - ai * si)
        bui = _dot3(ul, bih_ref[l], bil_ref[l]) + (ar * si + ai * sr)
        bur, bui = _s5_local_scan(bur, bui, pwre_ref, pwim_ref, lanes, seq_len)
        hre_ref[:, lanes] = bur
        him_ref[:, lanes] = bui
        ys.append(_dot(bur.astype(BF16), cre_ref[l]) - _dot(bui.astype(BF16), cim_ref[l]))
    o_ref[...] = _s5_tail(ys, u, d_ref, wglu_ref, bglu_ref)


def _s5_weight_specs(ws):
    return [_const_spec(w.shape) for w in ws]


def _s5_prompt(p2, n_seq, seq_len, ws):
    TT = TOK_TILE
    nt = seq_len // TT
    rows = n_seq * seq_len
    st_spec = pl.BlockSpec((1, 1, S5_LANES), lambda b, t: (b, 0, 0))
    return pl.pallas_call(
        _s5_prompt_kernel,
        grid=(n_seq, nt),
        in_specs=[pl.BlockSpec((TT, S5_WIDTH), lambda b, t: (b * nt + t, P_U // S5_WIDTH))]
        + _s5_weight_specs(ws),
        out_specs=[pl.BlockSpec((TT, S5_WIDTH), lambda b, t: (b * nt + t, 0)), st_spec, st_spec],
        out_shape=(jax.ShapeDtypeStruct((rows, S5_WIDTH), F32),
                   jax.ShapeDtypeStruct((n_seq, 1, S5_LANES), F32),
                   jax.ShapeDtypeStruct((n_seq, 1, S5_LANES), F32)),
        scratch_shapes=[pltpu.VMEM((TT, S5_BLK_ST), F32), pltpu.VMEM((TT, S5_BLK_ST), F32),
                        pltpu.VMEM((1, S5_LANES), F32), pltpu.VMEM((1, S5_LANES), F32)],
        compiler_params=_params(("parallel", "arbitrary")),
        name="s5_prompt",
    )(p2, *ws)


def _s5_sample(p2, s0re_rows, s0im_rows, seq_len, ws):
    rows = s0re_rows.shape[0]
    TT = TOK_TILE
    row_spec = pl.BlockSpec((TT, S5_LANES), lambda i: (i, 0))
    return pl.pallas_call(
        functools.partial(_s5_sample_kernel, seq_len=seq_len),
        grid=(rows // TT,),
        in_specs=[pl.BlockSpec((TT, S5_WIDTH), lambda i: (i, P_U // S5_WIDTH)), row_spec, row_spec]
        + _s5_weight_specs(ws),
        out_specs=[pl.BlockSpec((TT, S5_WIDTH), lambda i: (i, 0)), row_spec, row_spec],
        out_shape=(jax.ShapeDtypeStruct((rows, S5_WIDTH), F32),
                   jax.ShapeDtypeStruct((rows, S5_LANES), F32),
                   jax.ShapeDtypeStruct((rows, S5_LANES), F32)),
        compiler_params=_params(("parallel",)),
        name="s5_sample",
    )(p2, s0re_rows, s0im_rows, *ws)


def _outproj_kernel(x_ref, og_ref, os_ref, wt_ref, wb_ref, n2_ref,
                    x1_ref, h2_ref):
    x1 = (x_ref[...] + _dot(og_ref[...].astype(BF16), wt_ref[...])
          + _dot(os_ref[...].astype(BF16), wb_ref[...]))
    x1_ref[...] = x1
    h2_ref[...] = _rms(x1) * n2_ref[...]


def _out_proj(x2, og, osx, wt, wb, n2):
    rows = x2.shape[0]
    big = pl.BlockSpec((PROJ_TILE, D_MODEL), lambda i: (i, 0))
    half = pl.BlockSpec((PROJ_TILE, GLA_WIDTH), lambda i: (i, 0))
    return pl.pallas_call(
        _outproj_kernel,
        grid=(rows // PROJ_TILE,),
        in_specs=[big, half, half] + [_const_spec(w.shape) for w in (wt, wb, n2)],
        out_specs=[big, big],
        out_shape=(jax.ShapeDtypeStruct((rows, D_MODEL), F32),) * 2,
        compiler_params=_params(("parallel",)),
        name="out_proj",
    )(x2, og, osx, wt, wb, n2)


def _first_and_nth_max(work, n):
    first = None
    m = None
    for r in range(n):
        m = jnp.max(work, axis=0, keepdims=True)
        if r == 0:
            first = m
        if r < n - 1:
            work = jnp.where(work == m, -jnp.inf, work)
    return first, m


def _sorted_columns(s):
    n = s.shape[0] // 8
    cols = [s[8 * v:8 * (v + 1)] for v in range(n)]
    k = 2
    while k <= n:
        j = k // 2
        while j >= 1:
            for i in range(n):
                l = i ^ j
                if l > i:
                    hi, lo = jnp.maximum(cols[i], cols[l]), jnp.minimum(cols[i], cols[l])
                    cols[i], cols[l] = (hi, lo) if (i & k) == 0 else (lo, hi)
            j //= 2
        k *= 2
    return cols


def _top_values(s, n, store):
    cols = _sorted_columns(s)
    for r in range(n):
        m = jnp.max(cols[0], axis=0, keepdims=True)
        store(r, m)
        if r < n - 1:
            hit = cols[0] == m
            for v in range(n - 1 - r):
                nxt = cols[v + 1] if v + 1 < len(cols) else jnp.full_like(cols[v], -jnp.inf)
                cols[v] = jnp.where(hit, nxt, cols[v])


def _route_kernel(h2_ref, wq_ref, kh_ref, kl_ref,
                  n1_ref, e1_ref, rho_ref, e2_ref, s_s, top_s):
    TM = h2_ref.shape[0]
    qp = _dot(h2_ref[...].astype(BF16), wq_ref[...])
    for hp in range(2 * PEER_HEADS):
        q_hi, q_lo = _split(qp[:, hp * PEER_KEYS:(hp + 1) * PEER_KEYS])
        s_s[hp] = _dot_nt(kh_ref[hp], q_hi) + _dot_nt(kl_ref[hp], q_hi) + _dot_nt(kh_ref[hp], q_lo)
    K = PEER_TOPK
    for lc in range(TM // LANES):
        sl = slice(lc * LANES, (lc + 1) * LANES)
        for hp in range(2 * PEER_HEADS):
            def store(r, m, hp=hp):
                top_s[hp, r:r + 1, sl] = m

            _top_values(s_s[hp, :, sl], K, store)
        for h in range(PEER_HEADS):
            v1 = top_s[2 * h, :, sl]
            v2 = top_s[2 * h + 1, :, sl]
            cands = [v1 + v2[0:1]] + [v1[0:8] + v2[b:b + 1] for b in range(1, 8)] + [v2[8:K] + v1[0:1]]
            cand = jnp.concatenate(cands, axis=0)
            mx, thr = _first_and_nth_max(cand, K)
            z = jnp.sum(jnp.where(cand >= thr, jnp.exp(cand - mx), 0.0), axis=0, keepdims=True)
            cnt = jnp.zeros(v1.shape, F32)
            for b in range(K):
                cnt = cnt + jnp.where(v1 + v2[b:b + 1] >= thr, 1.0, 0.0)
            s1 = s_s[2 * h, :, sl]
            s2 = s_s[2 * h + 1, :, sl]
            n1 = jnp.zeros(s1.shape, F32)
            rho = jnp.full(s2.shape, float(K), F32)
            for a in range(K):
                n1 = jnp.where(s1 == v1[a:a + 1], cnt[a:a + 1], n1)
                rho = jnp.where(s2 == v2[a:a + 1], float(a), rho)
            n1_ref[h, :, sl] = n1
            e1_ref[h, :, sl] = jnp.exp(s1 - v1[0:1])
            rho_ref[h, :, sl] = rho.astype(BF16)
            e2_ref[h, :, sl] = (jnp.exp(s2 - v2[0:1]) * (1.0 / z)).astype(BF16)


def _route(h2, wq_bf, kh, kl):
    rows = h2.shape[0]
    TM = TOK_TILE
    key_spec = pl.BlockSpec((PEER_HEADS, PEER_KEYS, TM), lambda i: (0, 0, i))
    return pl.pallas_call(
        _route_kernel,
        grid=(rows // TM,),
        in_specs=[pl.BlockSpec((TM, D_MODEL), lambda i: (i, 0)),
                  _const_spec(wq_bf.shape),
                  _const_spec(kh.shape), _const_spec(kl.shape)],
        out_specs=[key_spec] * 4,
        out_shape=(jax.ShapeDtypeStruct((PEER_HEADS, PEER_KEYS, rows), F32),
                   jax.ShapeDtypeStruct((PEER_HEADS, PEER_KEYS, rows), F32),
                   jax.ShapeDtypeStruct((PEER_HEADS, PEER_KEYS, rows), BF16),
                   jax.ShapeDtypeStruct((PEER_HEADS, PEER_KEYS, rows), BF16)),
        scratch_shapes=[pltpu.VMEM((2 * PEER_HEADS, PEER_KEYS, TM), F32),
                        pltpu.VMEM((2 * PEER_HEADS, PEER_TOPK, TM), F32)],
        compiler_params=_params(("parallel",)),
        name="peer_route",
    )(h2, wq_bf, kh, kl)


PEER_CHUNK = 256


def _peer_gate_piece(i1, c, a, act_ref, wact_ref, n1_ref, e1_ref, rho_ref, e2_ref):
    ls = slice(c * PEER_CHUNK, (c + 1) * PEER_CHUNK)
    rs = slice(a * PEER_KEYS, (a + 1) * PEER_KEYS)
    w = jnp.zeros((PEER_KEYS, PEER_CHUNK), BF16)
    for h in range(PEER_HEADS):
        n1 = n1_ref[h, pl.ds(i1, 1), ls].astype(BF16)
        g1 = e1_ref[h, pl.ds(i1, 1), ls].astype(BF16)
        e2 = e2_ref[h, :, ls]
        w = w + jnp.where(rho_ref[h, :, ls] < n1, e2, jnp.zeros_like(e2)) * g1
    x = act_ref[rs, ls]
    t = jnp.exp((x * (-2.0 * GELU_C - (2.0 * GELU_C * GELU_A) * (x * x))).astype(BF16))
    wact_ref[rs, ls] = w * (x.astype(BF16) / (1.0 + t))


def _peer_half(blk_gate, u_half, vt_half, act_in, wact_out, wact_in, act_out,
               acc, h2b, n1_ref, e1_ref, rho_ref, e2_ref, stages):
    sub = act_in.shape[0] // PEER_KEYS
    for c in range(act_in.shape[1] // PEER_CHUNK):
        ls = slice(c * PEER_CHUNK, (c + 1) * PEER_CHUNK)
        if "C" in stages:
            acc[:, ls] += _dot(vt_half, wact_in[:, ls])
        if "B" in stages:
            for a in range(sub):
                _peer_gate_piece(blk_gate * sub + a, c, a, act_in, wact_out, n1_ref, e1_ref, rho_ref, e2_ref)
        if "A" in stages:
            act_out[:, ls] = _dot_nt(u_half, h2b[c * PEER_CHUNK:(c + 1) * PEER_CHUNK, :])


def _peer_kernel(h2_ref, x1_ref, n1_ref, e1_ref, rho_ref, e2_ref, u_ref, vt_ref, fn_ref, y_ref,
                 acc, h2b, wact0, wact1, act0, act1, *, n_steps):
    g = pl.program_id(1)
    EB = act0.shape[0]
    refs = (acc, h2b, n1_ref, e1_ref, rho_ref, e2_ref)

    def halves(first, second):
        _peer_half(2 * g - 1, u_ref[0:EB, :], vt_ref[:, 0:EB], act1, wact1, wact0, act0, *refs, stages=first)
        _peer_half(2 * g, u_ref[EB:2 * EB, :], vt_ref[:, EB:2 * EB], act0, wact0, wact1, act1, *refs,
                   stages=second)

    @pl.when(g == 0)
    def _():
        acc[...] = jnp.zeros_like(acc)
        h2b[...] = h2_ref[...].astype(BF16)
        halves("A", "AB")

    @pl.when((g > 0) & (g < n_steps - 1))
    def _():
        halves("ABC", "ABC")

    @pl.when(g == n_steps - 1)
    def _():
        halves("BC", "C")
        out = x1_ref[...] + acc[...].T
        y_ref[...] = _rms(out) * fn_ref[...]


def _peer(h2, x1, n1, e1, rho, e2, u_bf, vt_bf, fn):
    rows = h2.shape[0]
    TM = PEER_TOK_TILE
    EB = PEER_EXP_TILE
    n_steps = PEER_EXPERTS // (2 * EB) + 1
    big = pl.BlockSpec((TM, D_MODEL), lambda i, g: (i, 0))
    key_spec = pl.BlockSpec((PEER_HEADS, PEER_KEYS, TM), lambda i, g: (0, 0, i))
    return pl.pallas_call(
        functools.partial(_peer_kernel, n_steps=n_steps),
        grid=(rows // TM, n_steps),
        in_specs=[big, big, key_spec, key_spec, key_spec, key_spec,
                  pl.BlockSpec((2 * EB, D_MODEL), lambda i, g: (jnp.minimum(g, n_steps - 2), 0)),
                  pl.BlockSpec((D_MODEL, 2 * EB), lambda i, g: (0, jnp.maximum(g - 1, 0))),
                  _const_spec(fn.shape)],
        out_specs=big,
        out_shape=jax.ShapeDtypeStruct((rows, D_MODEL), F32),
        scratch_shapes=[pltpu.VMEM((D_MODEL, TM), F32), pltpu.VMEM((TM, D_MODEL), BF16),
                        pltpu.VMEM((EB, TM), BF16), pltpu.VMEM((EB, TM), BF16),
                        pltpu.VMEM((EB, TM), F32), pltpu.VMEM((EB, TM), F32)],
        compiler_params=_params(("parallel", "arbitrary")),
        name="peer_experts",
    )(h2, x1, n1, e1, rho, e2, u_bf, vt_bf, fn)


def _block_diag(w, eye):
    n, g, a, b = w.shape
    return jnp.einsum('lgab,gh->lgahb', w, eye).reshape(n, g * a, g * b)


def _gla_constants():
    lane_head = jnp.arange(GLA_QK) // GLA_DK
    e2 =(lane_head[:, None] == (jnp.arange(GLA_WIDTH) // GLA_DV)[None, :]).astype(BF16)
    sd = SAMPLE_SEQ_BLOCK * GLA_DK
    xc = jnp.arange(GLA_HEADS * sd)
    rep = ((lane_head[:, None] == (xc // sd)[None, :])
           & ((jnp.arange(GLA_QK) % GLA_DK)[:, None] == (xc % GLA_DK)[None, :])).astype(BF16)
    return e2, rep


def kernel(x_prompt, x_sample, state_gla, state_s5_re, state_s5_im, norm1, w_in, w_a2, b_a2, gla_norm, s5_lam_re, s5_lam_im, s5_log_dt, s5_b_re, s5_b_im, s5_c_re, s5_c_im, s5_d, w_glu, b_glu, w_out, norm2, peer_wq, peer_keys, peer_u, peer_v, final_norm):
    depth = norm1.shape[0]
    assert depth == 1, "single-layer trunk"
    n_p, len_p, _ = x_prompt.shape
    n_s, len_s, _ = x_sample.shape
    l = 0

    w = w_in[l]
    w_re = jnp.concatenate([w[:, 0:1536], w[:, 1552:2064], w[:, 1536:1552],
                            jnp.zeros((D_MODEL, W_IN_COLS - 2064), F32)], axis=1)
    w_bf = w_re.astype(BF16)
    a2 = jnp.concatenate([w_a2[l], jnp.zeros((W_IN_COLS - P_LA - GLA_RANK, GLA_QK), F32)], axis=0)
    a2hi, a2lo = _split(a2)
    ba2 = b_a2[l].reshape(1, GLA_QK)
    n1 = norm1[l].reshape(1, D_MODEL)
    gn = gla_norm[l].reshape(1, GLA_WIDTH)
    e2, rep = _gla_constants()

    pwre, pwim, bbre, bbim = _s5_prep(s5_lam_re[l], s5_lam_im[l], s5_log_dt[l], s5_b_re[l], s5_b_im[l])
    eye = jnp.eye(8, dtype=F32)
    blk = lambda t: t.reshape(S5_NBLK, 8, S5_GROUP_CH, S5_STATE)
    brh, brl = _split(_block_diag(blk(bbre), eye))
    bih, bil = _split(_block_diag(blk(bbim), eye))
    cre = _block_diag(jnp.swapaxes(blk(s5_c_re[l]), 2, 3), eye).astype(BF16)
    cim = _block_diag(jnp.swapaxes(blk(s5_c_im[l]), 2, 3), eye).astype(BF16)
    s5w = (brh, brl, bih, bil, cre, cim, pwre.reshape(8, S5_LANES), pwim.reshape(8, S5_LANES),
           s5_d[l].reshape(1, S5_WIDTH), w_glu[l].astype(BF16), b_glu[l].reshape(1, S5_WIDTH))

    wt = w_out[l][:GLA_WIDTH].astype(BF16)
    wb = w_out[l][GLA_WIDTH:].astype(BF16)
    n2 = norm2[l].reshape(1, D_MODEL)
    wq_bf = peer_wq[l].astype(BF16)
    kh, kl = _split(peer_keys[l].reshape(2 * PEER_HEADS, PEER_KEYS, PEER_DQ // 2))
    u_bf = peer_u[l].astype(BF16)
    vt_bf = peer_v[l].astype(BF16).T
    fn = final_norm.reshape(1, D_MODEL)

    def tail(x2, og, osx):
        x1, h2 = _out_proj(x2, og, osx, wt, wb, n2)
        n1, e1, rho, e2g = _route(h2, wq_bf, kh, kl)
        return _peer(h2, x1, n1, e1, rho, e2g, u_bf, vt_bf, fn)

    xp = x_prompt.reshape(n_p * len_p, D_MODEL)
    pp = _in_proj(xp, n1, w_bf, a2hi, a2lo, ba2)
    og_p, gla_p = _gla_prompt(pp, n_p, len_p, e2, gn)
    os_p, sre_p, sim_p = _s5_prompt(pp, n_p, len_p, s5w)
    y_p = tail(xp, og_p, os_p).reshape(n_p, len_p, D_MODEL)

    xs = x_sample.reshape(n_s * len_s, D_MODEL)
    ps = _in_proj(xs, n1, w_bf, a2hi, a2lo, ba2)
    og_s, gla_s = _gla_sample(ps, state_gla[l], len_s, e2, rep, gn)
    first_row = lambda s: jnp.pad(s.reshape(n_s, 1, S5_LANES), ((0, 0), (0, len_s - 1), (0, 0))
                                  ).reshape(n_s * len_s, S5_LANES)
    os_s, hre_s, him_s = _s5_sample(ps, first_row(state_s5_re[l]), first_row(state_s5_im[l]), len_s, s5w)
    y_s = tail(xs, og_s, os_s).reshape(n_s, len_s, D_MODEL)
    last_row = lambda hs: hs.reshape(n_s, len_s, S5_GROUPS, S5_STATE)[:, len_s - 1]

    st = lambda a: a.reshape(1, n_p, S5_GROUPS, S5_STATE)
    return (y_p, y_s, gla_p[None], st(sre_p), st(sim_p),
            gla_s[None], last_row(hre_s)[None], last_row(him_s)[None])
```

```python
import functools

import jax
import jax.numpy as jnp
from jax import lax
from jax.experimental import pallas as pl
from jax.experimental.pallas import tpu as pltpu

F32 = jnp.float32
BF16 = jnp.bfloat16

D_MODEL = 1024
GLA_HEADS = 4
GLA_DK = 64
GLA_DV = 128
GLA_QK = GLA_HEADS * GLA_DK
GLA_WIDTH = GLA_HEADS * GLA_DV
GLA_RANK = 16
GLA_TAU = 16.0
GLA_CHUNK = 64
S5_WIDTH = 512
S5_GROUP_CH = 16
S5_GROUPS = 32
S5_STATE = 64
S5_LANES = S5_GROUPS * S5_STATE
S5_NBLK = 4
S5_BLK_CH = S5_WIDTH // S5_NBLK
S5_BLK_ST = S5_LANES // S5_NBLK
PEER_KEYS = 128
PEER_EXPERTS = PEER_KEYS * PEER_KEYS
PEER_HEADS = 8
PEER_DQ = 256
PEER_TOPK = 16
EPS = 1e-6
LANES = 128
GELU_C = 0.7978845608028654
GELU_A = 0.044715

P_Q, P_K, P_V, P_G, P_U, P_LA = 0, 256, 512, 1024, 1536, 2048
P_COLS = 2304
W_IN_COLS = 2176

TOK_TILE = 256
PROJ_TILE = 512
PEER_TOK_TILE = 512
PEER_EXP_TILE = 512
SAMPLE_SEQ_BLOCK = 16
VMEM_LIMIT = 56 * 1024 * 1024


def _split(x):
    hi = x.astype(BF16)
    lo = (x - hi.astype(F32)).astype(BF16)
    return hi, lo


def _split3(x):
    a = x.astype(BF16)
    r = x - a.astype(F32)
    b = r.astype(BF16)
    c = (r - b.astype(F32)).astype(BF16)
    return a, b, c


def _dot(a, b):
    return jnp.dot(a, b, preferred_element_type=F32)


def _dot_nt(a, b):
    return lax.dot_general(a, b, (((1,), (1,)), ((), ())), preferred_element_type=F32)


def _dot_tn(a, b):
    return lax.dot_general(a, b, (((0,), (0,)), ((), ())), preferred_element_type=F32)


def _dot3(a, b_hi, b_lo):
    a_hi, a_lo = _split(a)
    return _dot(a_hi, b_hi) + _dot(a_lo, b_hi) + _dot(a_hi, b_lo)


def _dot_exact01(m01, x):
    a, b, c = _split3(x)
    return _dot(m01, a) + _dot(m01, b) + _dot(m01, c)


def _rms(x):
    return x * lax.rsqrt(jnp.mean(x * x, axis=-1, keepdims=True) + EPS)


def _params(sem):
    return pltpu.CompilerParams(dimension_semantics=sem, vmem_limit_bytes=VMEM_LIMIT)


def _const_spec(shape):
    n = len(shape)
    return pl.BlockSpec(shape, lambda *_: (0,) * n)


def _s5prep_kernel(lr_ref, li_ref, ldt_ref, bret_ref, bimt_ref,
                   pwre_ref, pwim_ref, bbre_ref, bbim_ref):
    lr = lr_ref[...]
    li = li_ref[...]
    dt = jnp.exp(ldt_ref[...])
    mag = jnp.exp(lr * dt)
    abr = mag * jnp.cos(li * dt)
    abi = mag * jnp.sin(li * dt)
    den = lr * lr + li * li
    nr = abr - 1.0
    ni = abi
    fr = (nr * lr + ni * li) / den
    fi = (ni * lr - nr * li) / den
    bret = bret_ref[...]
    bimt = bimt_ref[...]
    bbre_ref[...] = fr[:, None, :] * bret - fi[:, None, :] * bimt
    bbim_ref[...] = fr[:, None, :] * bimt + fi[:, None, :] * bret
    pr, pi = abr, abi
    for i in range(8):
        pwre_ref[i] = pr
        pwim_ref[i] = pi
        pr, pi = pr * abr - pi * abi, pr * abi + pi * abr


def _s5_prep(lam_re, lam_im, log_dt, b_re, b_im):
    g, p = lam_re.shape
    ch = b_re.shape[-1]
    bret = jnp.transpose(b_re, (0, 2, 1))
    bimt = jnp.transpose(b_im, (0, 2, 1))
    out = pl.pallas_call(
        _s5prep_kernel,
        out_shape=(jax.ShapeDtypeStruct((8, g, p), F32), jax.ShapeDtypeStruct((8, g, p), F32),
                   jax.ShapeDtypeStruct((g, ch, p), F32), jax.ShapeDtypeStruct((g, ch, p), F32)),
        name="s5_prep",
    )(lam_re, lam_im, log_dt.reshape(g, 1), bret, bimt)
    return out


def _inproj_kernel(x_ref, n1_ref, w_ref, a2hi_ref, a2lo_ref, ba2_ref, p_ref):
    h = _rms(x_ref[...]) * n1_ref[...]
    p = _dot(h.astype(BF16), w_ref[...])
    alr = p[:, P_LA:W_IN_COLS]
    z = _dot3(alr, a2hi_ref[...], a2lo_ref[...]) + ba2_ref[...]
    log_sig = jnp.minimum(z, 0.0) - jnp.log1p(jnp.exp(-jnp.abs(z)))
    p_ref[:, 0:P_LA] = p[:, 0:P_LA]
    p_ref[:, P_LA:P_COLS] = log_sig * (1.0 / GLA_TAU)


def _in_proj(x2, n1, w_bf, a2hi, a2lo, ba2):
    rows = x2.shape[0]
    return pl.pallas_call(
        _inproj_kernel,
        grid=(rows // PROJ_TILE,),
        in_specs=[pl.BlockSpec((PROJ_TILE, D_MODEL), lambda i: (i, 0)),
                  _const_spec(n1.shape), _const_spec(w_bf.shape),
                  _const_spec(a2hi.shape), _const_spec(a2lo.shape), _const_spec(ba2.shape)],
        out_specs=pl.BlockSpec((PROJ_TILE, P_COLS), lambda i: (i, 0)),
        out_shape=jax.ShapeDtypeStruct((rows, P_COLS), F32),
        compiler_params=_params(("parallel",)),
        name="in_proj",
    )(x2, n1, w_bf, a2hi, a2lo, ba2)


def _cumsum_rows(mask01, la):
    return _dot_exact01(mask01.astype(BF16), la)


GLA_BAND = 4
SUBLANES = 8


def _gla_prompt_kernel(q_ref, k_ref, v_ref, g_ref, la_ref, e2_ref, gn_ref,
                       o_ref, sfin_ref, st_ref, slab, *, n_chunks):
    c = pl.program_id(1)
    C = q_ref.shape[0]
    HC = GLA_HEADS * C

    @pl.when(c == 0)
    def _():
        st_ref[...] = jnp.zeros_like(st_ref)

    la = la_ref[...]
    row = lax.broadcasted_iota(jnp.int32, (C, C), 0)
    col = lax.broadcasted_iota(jnp.int32, (C, C), 1)
    b = _cumsum_rows(col <= row, la)
    q = q_ref[...] * (GLA_DK ** -0.5)
    k = k_ref[...]
    v = v_ref[...]
    blast = b[C - 1:C, :]

    rloc = lax.broadcasted_iota(jnp.int32, (C, GLA_QK), 0) % GLA_BAND
    tiles = lambda t: t.reshape(C // SUBLANES, SUBLANES, t.shape[-1])
    k3, b3, v3 = tiles(k), tiles(b), tiles(v)
    vsh = [v]
    for d in range(GLA_BAND):
        if d == 0:
            ks_, bs_ = k, b
        else:
            ks_ = pltpu.roll(k3, d, 1).reshape(C, GLA_QK)
            bs_ = pltpu.roll(b3, d, 1).reshape(C, GLA_QK)
            vsh.append(pltpu.roll(v3, d, 1).reshape(C, GLA_WIDTH))
        m = q * ks_ * jnp.exp(jnp.minimum(b - bs_, 0.0))
        slab[d * C:(d + 1) * C, :] = jnp.where(rloc >= d, m, 0.0).astype(BF16)
    rep = _dot(slab[...], e2_ref[...])
    o = rep[0:C] * vsh[0]
    for d in range(1, GLA_BAND):
        o = o + rep[d * C:(d + 1) * C] * vsh[d]

    lane_head = lax.broadcasted_iota(jnp.int32, (HC, GLA_QK), 1) // GLA_DK
    row_head = lax.broadcasted_iota(jnp.int32, (HC, GLA_QK), 0) // C
    own_head = lane_head == row_head
    si = lax.broadcasted_iota(jnp.int32, (HC, C), 0) % C
    sj = lax.broadcasted_iota(jnp.int32, (HC, C), 1)
    scores = jnp.zeros((HC, C), F32)
    s = C // 2
    while s >= GLA_BAND:
        ref = jnp.concatenate([jnp.broadcast_to(b[p * 2 * s + s - 1:p * 2 * s + s, :], (2 * s, GLA_QK))
                               for p in range(C // (2 * s))], axis=0)
        ql = q * jnp.exp(jnp.minimum(b - ref, 0.0))
        kl = (k * jnp.exp(jnp.minimum(ref - b, 0.0))).astype(BF16)
        qs = jnp.where(own_head, jnp.concatenate([ql] * GLA_HEADS, axis=0), 0.0).astype(BF16)
        lvl = ((si // (2 * s)) == (sj // (2 * s))) & ((si // s) % 2 == 1) & ((sj // s) % 2 == 0)
        scores = scores + jnp.where(lvl, _dot_nt(qs, kl), 0.0)
        s //= 2
    scores = scores.astype(BF16)
    vb = v.astype(BF16)

    st = st_ref[...]
    o = o + _dot_nt((q * jnp.exp(b)).astype(BF16), st.astype(BF16))
    outs = []
    for h in range(GLA_HEADS):
        vs = slice(h * GLA_DV, (h + 1) * GLA_DV)
        outs.append(_rms(o[:, vs] + _dot(scores[h * C:(h + 1) * C], vb[:, vs])))
    g = g_ref[...]
    o_ref[...] = jnp.concatenate(outs, axis=-1) * gn_ref[...] * (g * jax.nn.sigmoid(g))

    kd_hi, kd_lo = _split(k * jnp.exp(blast - b))
    vt_hi, vt_lo = _split(v.T)
    upd = _dot(vt_hi, kd_hi) + _dot(vt_lo, kd_hi) + _dot(vt_hi, kd_lo)
    blk = (lax.broadcasted_iota(jnp.int32, st.shape, 0) // GLA_DV
           == lax.broadcasted_iota(jnp.int32, st.shape, 1) // GLA_DK)
    st_new = jnp.exp(blast) * st + jnp.where(blk, upd, 0.0)
    st_ref[...] = st_new

    @pl.when(c == n_chunks - 1)
    def _():
        for h in range(GLA_HEADS):
            sfin_ref[0, h] = st_new[h * GLA_DV:(h + 1) * GLA_DV, h * GLA_DK:(h + 1) * GLA_DK].T


def _gla_prompt(p2, n_seq, seq_len, e2, gn):
    C = GLA_CHUNK
    nch = seq_len // C
    rows = n_seq * seq_len

    def tok(width, colblk):
        return pl.BlockSpec((C, width), lambda b, c: (b * nch + c, colblk))

    return pl.pallas_call(
        functools.partial(_gla_prompt_kernel, n_chunks=nch),
        grid=(n_seq, nch),
        in_specs=[tok(GLA_QK, P_Q // GLA_QK), tok(GLA_QK, P_K // GLA_QK),
                  tok(GLA_WIDTH, P_V // GLA_WIDTH), tok(GLA_WIDTH, P_G // GLA_WIDTH),
                  tok(GLA_QK, P_LA // GLA_QK),
                  _const_spec(e2.shape), _const_spec(gn.shape)],
        out_specs=[pl.BlockSpec((C, GLA_WIDTH), lambda b, c: (b * nch + c, 0)),
                   pl.BlockSpec((1, GLA_HEADS, GLA_DK, GLA_DV), lambda b, c: (b, 0, 0, 0))],
        out_shape=(jax.ShapeDtypeStruct((rows, GLA_WIDTH), F32),
                   jax.ShapeDtypeStruct((n_seq, GLA_HEADS, GLA_DK, GLA_DV), F32)),
        scratch_shapes=[pltpu.VMEM((GLA_WIDTH, GLA_QK), F32),
                        pltpu.VMEM((GLA_BAND * C, GLA_QK), BF16)],
        compiler_params=_params(("parallel", "arbitrary")),
        name="gla_prompt",
    )(p2, p2, p2, p2, p2, e2, gn)


def _gla_sample_kernel(q_ref, k_ref, v_ref, g_ref, la_ref, s0_ref, e2_ref, rep_ref, gn_ref,
                       o_ref, snew_ref, *, seq_len):
    R = q_ref.shape[0]
    nseq = R // seq_len
    SD = nseq * GLA_DK
    la = la_ref[...]
    row = lax.broadcasted_iota(jnp.int32, (R, R), 0)
    col = lax.broadcasted_iota(jnp.int32, (R, R), 1)
    same = (col // seq_len) == (row // seq_len)
    b = _cumsum_rows(same & (col <= row), la)
    btot = _cumsum_rows(same, la)
    q = q_ref[...] * (GLA_DK ** -0.5)
    k = k_ref[...]
    v = v_ref[...]
    rmod = lax.broadcasted_iota(jnp.int32, (R, GLA_QK), 0) % seq_len

    o = jnp.zeros((R, GLA_WIDTH), F32)
    for d in range(seq_len):
        ks_, bs_, vs_ = (k, b, v) if d == 0 else (pltpu.roll(k, d, 0), pltpu.roll(b, d, 0),
                                                  pltpu.roll(v, d, 0))
        m = q * ks_ * jnp.exp(jnp.minimum(b - bs_, 0.0))
        m = jnp.where(rmod >= d, m, 0.0)
        o = o + _dot(m.astype(BF16), e2_ref[...]) * vs_

    xr = lax.broadcasted_iota(jnp.int32, (R, GLA_HEADS * SD), 0) // seq_len
    xc = (lax.broadcasted_iota(jnp.int32, (R, GLA_HEADS * SD), 1) % SD) // GLA_DK
    own = xr == xc
    rep = rep_ref[...]

    def expand(x_bf16):
        return jnp.where(own, _dot(x_bf16, rep), 0.0).astype(BF16)

    qx = expand((q * jnp.exp(b)).astype(BF16))
    kd_hi, kd_lo = _split(k * jnp.exp(btot - b))
    kx_hi, kx_lo = expand(kd_hi), expand(kd_lo)
    ea, eb, ec = _split3(jnp.exp(btot))
    ax = (expand(ea), expand(eb), expand(ec))
    last = (lax.broadcasted_iota(jnp.int32, (R, GLA_DV), 0) % seq_len == seq_len - 1).astype(BF16)

    outs = []
    for h in range(GLA_HEADS):
        xs = slice(h * SD, (h + 1) * SD)
        vs = slice(h * GLA_DV, (h + 1) * GLA_DV)
        s0 = s0_ref[:, h].reshape(SD, GLA_DV)
        o_h = o[:, vs] + _dot(qx[:, xs], s0.astype(BF16))
        outs.append(_rms(o_h))
        v_hi, v_lo = _split(v[:, vs])
        upd = _dot_tn(kx_hi[:, xs], v_hi) + _dot_tn(kx_lo[:, xs], v_hi) + _dot_tn(kx_hi[:, xs], v_lo)
        decay = _dot_tn(ax[0][:, xs], last) + _dot_tn(ax[1][:, xs], last) + _dot_tn(ax[2][:, xs], last)
        snew_ref[:, h] = (decay * s0 + upd).reshape(nseq, GLA_DK, GLA_DV)
    g = g_ref[...]
    o_ref[...] = jnp.concatenate(outs, axis=-1) * gn_ref[...] * (g * jax.nn.sigmoid(g))


def _gla_sample(p2, s0, seq_len, e2, rep, gn):
    n_seq = s0.shape[0]
    R = SAMPLE_SEQ_BLOCK * seq_len
    nblk = n_seq // SAMPLE_SEQ_BLOCK

    def tok(width, colblk):
        return pl.BlockSpec((R, width), lambda i: (i, colblk))

    st_spec = pl.BlockSpec((SAMPLE_SEQ_BLOCK, GLA_HEADS, GLA_DK, GLA_DV), lambda i: (i, 0, 0, 0))
    return pl.pallas_call(
        functools.partial(_gla_sample_kernel, seq_len=seq_len),
        grid=(nblk,),
        in_specs=[tok(GLA_QK, P_Q // GLA_QK), tok(GLA_QK, P_K // GLA_QK),
                  tok(GLA_WIDTH, P_V // GLA_WIDTH), tok(GLA_WIDTH, P_G // GLA_WIDTH),
                  tok(GLA_QK, P_LA // GLA_QK), st_spec,
                  _const_spec(e2.shape), _const_spec(rep.shape), _const_spec(gn.shape)],
        out_specs=[pl.BlockSpec((R, GLA_WIDTH), lambda i: (i, 0)), st_spec],
        out_shape=(jax.ShapeDtypeStruct((n_seq * seq_len, GLA_WIDTH), F32),
                   jax.ShapeDtypeStruct(s0.shape, F32)),
        compiler_params=_params(("parallel",)),
        name="gla_sample",
    )(p2, p2, p2, p2, p2, s0, e2, rep, gn)


def _s5_local_scan(bur, bui, pwre_ref, pwim_ref, lanes, group):
    rows, width = bur.shape
    xr = bur.reshape(rows // 8, 8, width)
    xi = bui.reshape(rows // 8, 8, width)
    sub = lax.broadcasted_iota(jnp.int32, (8, width), 0) % group
    s = 1
    while s < group:
        ar = jnp.where(sub >= s, pwre_ref[s - 1:s, lanes], 0.0)[None]
        ai = jnp.where(sub >= s, pwim_ref[s - 1:s, lanes], 0.0)[None]
        sr = pltpu.roll(xr, s, 1)
        si = pltpu.roll(xi, s, 1)
        xr, xi = xr + ar * sr - ai * si, xi + ar * si + ai * sr
        s *= 2
    return xr.reshape(rows, width), xi.reshape(rows, width)


def _s5_tail(ys, u, d_ref, wglu_ref, bglu_ref):
    y = jnp.concatenate(ys, axis=-1) + d_ref[...] * u
    z = jax.nn.gelu(y)
    return z * jax.nn.sigmoid(_dot(z.astype(BF16), wglu_ref[...]) + bglu_ref[...])


def _s5_prompt_kernel(u_ref, bre_ref, bim_ref, cre_ref, cim_ref,
                      pwre_ref, pwim_ref, d_ref, wglu_ref, bglu_ref,
                      o_ref, stre_ref, stim_ref, hre_s, him_s, car_re, car_im):
    t = pl.program_id(1)
    TT = u_ref.shape[0]

    @pl.when(t == 0)
    def _():
        car_re[...] = jnp.zeros_like(car_re)
        car_im[...] = jnp.zeros_like(car_im)

    u = u_ref[...]
    ys = []
    for l in range(S5_NBLK):
        lanes = slice(l * S5_BLK_ST, (l + 1) * S5_BLK_ST)
        ul = u[:, l * S5_BLK_CH:(l + 1) * S5_BLK_CH]
        ub = ul.astype(BF16)
        bur = _dot(ub, bre_ref[l])
        bui = _dot(ub, bim_ref[l])
        bur, bui = _s5_local_scan(bur, bui, pwre_ref, pwim_ref, lanes, 8)
        hre_s[...] = bur
        him_s[...] = bui
        p8r = pwre_ref[:, lanes]
        p8i = pwim_ref[:, lanes]

        def grp(r, carry):
            cr, ci = carry
            off = pl.multiple_of(r * 8, 8)
            xr = hre_s[pl.ds(off, 8), :] + p8r * cr - p8i * ci
            xi = him_s[pl.ds(off, 8), :] + p8r * ci + p8i * cr
            hre_s[pl.ds(off, 8), :] = xr
            him_s[pl.ds(off, 8), :] = xi
            return (jnp.broadcast_to(xr[7:8], xr.shape), jnp.broadcast_to(xi[7:8], xi.shape))

        cr0 = jnp.broadcast_to(car_re[:, lanes], (8, S5_BLK_ST))
        ci0 = jnp.broadcast_to(car_im[:, lanes], (8, S5_BLK_ST))
        cr, ci = lax.fori_loop(0, TT // 8, grp, (cr0, ci0))
        car_re[:, lanes] = cr[0:1]
        car_im[:, lanes] = ci[0:1]
        ys.append(_dot(hre_s[...].astype(BF16), cre_ref[l]) - _dot(him_s[...].astype(BF16), cim_ref[l]))
    o_ref[...] = _s5_tail(ys, u, d_ref, wglu_ref, bglu_ref)
    stre_ref[0] = car_re[...]
    stim_ref[0] = car_im[...]


def _s5_sample_kernel(u_ref, s0re_ref, s0im_ref, bre_ref, bim_ref, cre_ref, cim_ref,
                      pwre_ref, pwim_ref, d_ref, wglu_ref, bglu_ref,
                      o_ref, hre_ref, him_ref, *, seq_len):
    u = u_ref[...]
    ys = []
    for l in range(S5_NBLK):
        lanes = slice(l * S5_BLK_ST, (l + 1) * S5_BLK_ST)
        ul = u[:, l * S5_BLK_CH:(l + 1) * S5_BLK_CH]
        ar = pwre_ref[0:1, lanes]
        ai = pwim_ref[0:1, lanes]
        sr = s0re_ref[:, lanes]
        si = s0im_ref[:, lanes]
        ub = ul.astype(BF16)
        bur = _dot(ub, bre_ref[l]) + (ar * sr - ai * si)
        bui = _dot(ub, bim_ref[l]) + (ar * si + ai * sr)
        bur, bui = _s5_local_scan(bur, bui, pwre_ref, pwim_ref, lanes, seq_len)
        hre_ref[:, lanes] = bur
        him_ref[:, lanes] = bui
        ys.append(_dot(bur.astype(BF16), cre_ref[l]) - _dot(bui.astype(BF16), cim_ref[l]))
    o_ref[...] = _s5_tail(ys, u, d_ref, wglu_ref, bglu_ref)


def _s5_weight_specs(ws):
    return [_const_spec(w.shape) for w in ws]


def _s5_prompt(p2, n_seq, seq_len, ws):
    TT = TOK_TILE
    nt = seq_len // TT
    rows = n_seq * seq_len
    st_spec = pl.BlockSpec((1, 1, S5_LANES), lambda b, t: (b, 0, 0))
    return pl.pallas_call(
        _s5_prompt_kernel,
        grid=(n_seq, nt),
        in_specs=[pl.BlockSpec((TT, S5_WIDTH), lambda b, t: (b * nt + t, P_U // S5_WIDTH))]
        + _s5_weight_specs(ws),
        out_specs=[pl.BlockSpec((TT, S5_WIDTH), lambda b, t: (b * nt + t, 0)), st_spec, st_spec],
        out_shape=(jax.ShapeDtypeStruct((rows, S5_WIDTH), F32),
                   jax.ShapeDtypeStruct((n_seq, 1, S5_LANES), F32),
                   jax.ShapeDtypeStruct((n_seq, 1, S5_LANES), F32)),
        scratch_shapes=[pltpu.VMEM((TT, S5_BLK_ST), F32), pltpu.VMEM((TT, S5_BLK_ST), F32),
                        pltpu.VMEM((1, S5_LANES), F32), pltpu.VMEM((1, S5_LANES), F32)],
        compiler_params=_params(("parallel", "arbitrary")),
        name="s5_prompt",
    )(p2, *ws)


def _s5_sample(p2, s0re_rows, s0im_rows, seq_len, ws):
    rows = s0re_rows.shape[0]
    TT = TOK_TILE
    row_spec = pl.BlockSpec((TT, S5_LANES), lambda i: (i, 0))
    return pl.pallas_call(
        functools.partial(_s5_sample_kernel, seq_len=seq_len),
        grid=(rows // TT,),
        in_specs=[pl.BlockSpec((TT, S5_WIDTH), lambda i: (i, P_U // S5_WIDTH)), row_spec, row_spec]
        + _s5_weight_specs(ws),
        out_specs=[pl.BlockSpec((TT, S5_WIDTH), lambda i: (i, 0)), row_spec, row_spec],
        out_shape=(jax.ShapeDtypeStruct((rows, S5_WIDTH), F32),
                   jax.ShapeDtypeStruct((rows, S5_LANES), F32),
                   jax.ShapeDtypeStruct((rows, S5_LANES), F32)),
        compiler_params=_params(("parallel",)),
        name="s5_sample",
    )(p2, s0re_rows, s0im_rows, *ws)


def _outproj_kernel(x_ref, og_ref, os_ref, wt_ref, wb_ref, x1_ref):
    x1 = (x_ref[...] + _dot(og_ref[...].astype(BF16), wt_ref[...])
          + _dot(os_ref[...].astype(BF16), wb_ref[...]))
    x1_ref[...] = x1


def _out_proj(x2, og, osx, wt, wb):
    rows = x2.shape[0]
    big = pl.BlockSpec((PROJ_TILE, D_MODEL), lambda i: (i, 0))
    half = pl.BlockSpec((PROJ_TILE, GLA_WIDTH), lambda i: (i, 0))
    return pl.pallas_call(
        _outproj_kernel,
        grid=(rows // PROJ_TILE,),
        in_specs=[big, half, half] + [_const_spec(w.shape) for w in (wt, wb)],
        out_specs=big,
        out_shape=jax.ShapeDtypeStruct((rows, D_MODEL), F32),
        compiler_params=_params(("parallel",)),
        name="out_proj",
    )(x2, og, osx, wt, wb)


def _first_and_nth_max(work, n):
    first = None
    m = None
    for r in range(n):
        m = jnp.max(work, axis=0, keepdims=True)
        if r == 0:
            first = m
        if r < n - 1:
            work = jnp.where(work == m, -jnp.inf, work)
    return first, m


def _sorted_columns(s):
    n = s.shape[0] // 8
    cols = [s[8 * v:8 * (v + 1)] for v in range(n)]
    k = 2
    while k <= n:
        j = k // 2
        while j >= 1:
            for i in range(n):
                l = i ^ j
                if l > i:
                    hi, lo = jnp.maximum(cols[i], cols[l]), jnp.minimum(cols[i], cols[l])
                    cols[i], cols[l] = (hi, lo) if (i & k) == 0 else (lo, hi)
            j //= 2
        k *= 2
    return cols


def _top_values(s, n, store):
    cols = _sorted_columns(s)
    for r in range(n):
        m = jnp.max(cols[0], axis=0, keepdims=True)
        store(r, m)
        if r < n - 1:
            hit = cols[0] == m
            for v in range(n - 1 - r):
                nxt = cols[v + 1] if v + 1 < len(cols) else jnp.full_like(cols[v], -jnp.inf)
                cols[v] = jnp.where(hit, nxt, cols[v])


def _route_kernel(x1_ref, n2_ref, wq_ref, kh_ref, kl_ref,
                  n1_ref, e1_ref, rho_ref, e2_ref, s_s, top_s):
    TM = x1_ref.shape[0]
    h2 = _rms(x1_ref[...]) * n2_ref[...]
    qp = _dot(h2.astype(BF16), wq_ref[...])
    for hp in range(2 * PEER_HEADS):
        q_hi, q_lo = _split(qp[:, hp * PEER_KEYS:(hp + 1) * PEER_KEYS])
        s_s[hp] = _dot_nt(kh_ref[hp], q_hi) + _dot_nt(kl_ref[hp], q_hi) + _dot_nt(kh_ref[hp], q_lo)
    K = PEER_TOPK
    for lc in range(TM // LANES):
        sl = slice(lc * LANES, (lc + 1) * LANES)
        for hp in range(2 * PEER_HEADS):
            def store(r, m, hp=hp):
                top_s[hp, r:r + 1, sl] = m

            _top_values(s_s[hp, :, sl], K, store)
        for h in range(PEER_HEADS):
            v1 = top_s[2 * h, :, sl]
            v2 = top_s[2 * h + 1, :, sl]
            cands = [v1 + v2[0:1]] + [v1[0:8] + v2[b:b + 1] for b in range(1, 8)] + [v2[8:K] + v1[0:1]]
            cand = jnp.concatenate(cands, axis=0)
            mx, thr = _first_and_nth_max(cand, K)
            z = jnp.sum(jnp.where(cand >= thr, jnp.exp(cand - mx), 0.0), axis=0, keepdims=True)
            cnt = jnp.zeros(v1.shape, F32)
            for b in range(K):
                cnt = cnt + jnp.where(v1 + v2[b:b + 1] >= thr, 1.0, 0.0)
            s1 = s_s[2 * h, :, sl]
            s2 = s_s[2 * h + 1, :, sl]
            n1 = jnp.zeros(s1.shape, F32)
            rho = jnp.full(s2.shape, float(K), F32)
            for a in range(K):
                n1 = jnp.where(s1 == v1[a:a + 1], cnt[a:a + 1], n1)
                rho = jnp.where(s2 == v2[a:a + 1], float(a), rho)
            n1_ref[h, :, sl] = n1
            e1_ref[h, :, sl] = jnp.exp(s1 - v1[0:1])
            rho_ref[h, :, sl] = rho.astype(BF16)
            e2_ref[h, :, sl] = (jnp.exp(s2 - v2[0:1]) * (1.0 / z)).astype(BF16)


def _route(x1, n2, wq_bf, kh, kl):
    rows = x1.shape[0]
    TM = TOK_TILE
    key_spec = pl.BlockSpec((PEER_HEADS, PEER_KEYS, TM), lambda i: (0, 0, i))
    return pl.pallas_call(
        _route_kernel,
        grid=(rows // TM,),
        in_specs=[pl.BlockSpec((TM, D_MODEL), lambda i: (i, 0)),
                  _const_spec(n2.shape), _const_spec(wq_bf.shape),
                  _const_spec(kh.shape), _const_spec(kl.shape)],
        out_specs=[key_spec] * 4,
        out_shape=(jax.ShapeDtypeStruct((PEER_HEADS, PEER_KEYS, rows), F32),
                   jax.ShapeDtypeStruct((PEER_HEADS, PEER_KEYS, rows), F32),
                   jax.ShapeDtypeStruct((PEER_HEADS, PEER_KEYS, rows), BF16),
                   jax.ShapeDtypeStruct((PEER_HEADS, PEER_KEYS, rows), BF16)),
        scratch_shapes=[pltpu.VMEM((2 * PEER_HEADS, PEER_KEYS, TM), F32),
                        pltpu.VMEM((2 * PEER_HEADS, PEER_TOPK, TM), F32)],
        compiler_params=_params(("parallel",)),
        name="peer_route",
    )(x1, n2, wq_bf, kh, kl)


PEER_CHUNK = 256


def _peer_gate_piece(i1, c, a, act_ref, wact_ref, n1_ref, e1_ref, rho_ref, e2_ref):
    ls = slice(c * PEER_CHUNK, (c + 1) * PEER_CHUNK)
    rs = slice(a * PEER_KEYS, (a + 1) * PEER_KEYS)
    w = jnp.zeros((PEER_KEYS, PEER_CHUNK), BF16)
    for h in range(PEER_HEADS):
        n1 = n1_ref[h, pl.ds(i1, 1), ls].astype(BF16)
        g1 = e1_ref[h, pl.ds(i1, 1), ls].astype(BF16)
        e2 = e2_ref[h, :, ls]
        w = w + jnp.where(rho_ref[h, :, ls] < n1, e2, jnp.zeros_like(e2)) * g1
    x = act_ref[rs, ls]
    t = jnp.exp((x * (-2.0 * GELU_C - (2.0 * GELU_C * GELU_A) * (x * x))).astype(BF16))
    wact_ref[rs, ls] = w * (x.astype(BF16) / (1.0 + t))


def _peer_half(blk_gate, u_half, vt_half, act_in, wact_out, wact_in, act_out,
               acc, h2b, n1_ref, e1_ref, rho_ref, e2_ref, stages):
    sub = act_in.shape[0] // PEER_KEYS
    for c in range(act_in.shape[1] // PEER_CHUNK):
        ls = slice(c * PEER_CHUNK, (c + 1) * PEER_CHUNK)
        if "C" in stages:
            acc[:, ls] += _dot(vt_half, wact_in[:, ls])
        if "B" in stages:
            for a in range(sub):
                _peer_gate_piece(blk_gate * sub + a, c, a, act_in, wact_out, n1_ref, e1_ref, rho_ref, e2_ref)
        if "A" in stages:
            act_out[:, ls] = _dot_nt(u_half, h2b[c * PEER_CHUNK:(c + 1) * PEER_CHUNK, :])


def _peer_kernel(x1_ref, n2_ref, n1_ref, e1_ref, rho_ref, e2_ref, u_ref, vt_ref, fn_ref, y_ref,
                 acc, h2b, wact0, wact1, act0, act1, *, n_steps):
    g = pl.program_id(1)
    EB = act0.shape[0]
    refs = (acc, h2b, n1_ref, e1_ref, rho_ref, e2_ref)

    def halves(first, second):
        _peer_half(2 * g - 1, u_ref[0:EB, :], vt_ref[:, 0:EB], act1, wact1, wact0, act0, *refs, stages=first)
        _peer_half(2 * g, u_ref[EB:2 * EB, :], vt_ref[:, EB:2 * EB], act0, wact0, wact1, act1, *refs,
                   stages=second)

    @pl.when(g == 0)
    def _():
        acc[...] = jnp.zeros_like(acc)
        h2b[...] = (_rms(x1_ref[...]) * n2_ref[...]).astype(BF16)
        halves("A", "AB")

    @pl.when((g > 0) & (g < n_steps - 1))
    def _():
        halves("ABC", "ABC")

    @pl.when(g == n_steps - 1)
    def _():
        halves("BC", "C")
        out = x1_ref[...] + acc[...].T
        y_ref[...] = _rms(out) * fn_ref[...]


def _peer(x1, n2, n1, e1, rho, e2, u_bf, vt_bf, fn):
    rows = x1.shape[0]
    TM = PEER_TOK_TILE
    EB = PEER_EXP_TILE
    n_steps = PEER_EXPERTS // (2 * EB) + 1
    big = pl.BlockSpec((TM, D_MODEL), lambda i, g: (i, 0))
    key_spec = pl.BlockSpec((PEER_HEADS, PEER_KEYS, TM), lambda i, g: (0, 0, i))
    return pl.pallas_call(
        functools.partial(_peer_kernel, n_steps=n_steps),
        grid=(rows // TM, n_steps),
        in_specs=[big, _const_spec(n2.shape), key_spec, key_spec, key_spec, key_spec,
                  pl.BlockSpec((2 * EB, D_MODEL), lambda i, g: (jnp.minimum(g, n_steps - 2), 0)),
                  pl.BlockSpec((D_MODEL, 2 * EB), lambda i, g: (0, jnp.maximum(g - 1, 0))),
                  _const_spec(fn.shape)],
        out_specs=big,
        out_shape=jax.ShapeDtypeStruct((rows, D_MODEL), F32),
        scratch_shapes=[pltpu.VMEM((D_MODEL, TM), F32), pltpu.VMEM((TM, D_MODEL), BF16),
                        pltpu.VMEM((EB, TM), BF16), pltpu.VMEM((EB, TM), BF16),
                        pltpu.VMEM((EB, TM), F32), pltpu.VMEM((EB, TM), F32)],
        compiler_params=_params(("parallel", "arbitrary")),
        name="peer_experts",
    )(x1, n2, n1, e1, rho, e2, u_bf, vt_bf, fn)


def _block_diag(w, eye):
    n, g, a, b = w.shape
    return jnp.einsum('lgab,gh->lgahb', w, eye).reshape(n, g * a, g * b)


def _gla_constants():
    lane_head = jnp.arange(GLA_QK) // GLA_DK
    e2 =(lane_head[:, None] == (jnp.arange(GLA_WIDTH) // GLA_DV)[None, :]).astype(BF16)
    sd = SAMPLE_SEQ_BLOCK * GLA_DK
    xc = jnp.arange(GLA_HEADS * sd)
    rep = ((lane_head[:, None] == (xc // sd)[None, :])
           & ((jnp.arange(GLA_QK) % GLA_DK)[:, None] == (xc % GLA_DK)[None, :])).astype(BF16)
    return e2, rep


def kernel(x_prompt, x_sample, state_gla, state_s5_re, state_s5_im, norm1, w_in, w_a2, b_a2, gla_norm, s5_lam_re, s5_lam_im, s5_log_dt, s5_b_re, s5_b_im, s5_c_re, s5_c_im, s5_d, w_glu, b_glu, w_out, norm2, peer_wq, peer_keys, peer_u, peer_v, final_norm):
    depth = norm1.shape[0]
    assert depth == 1, "single-layer trunk"
    n_p, len_p, _ = x_prompt.shape
    n_s, len_s, _ = x_sample.shape
    l = 0

    w = w_in[l]
    w_re = jnp.concatenate([w[:, 0:1536], w[:, 1552:2064], w[:, 1536:1552],
                            jnp.zeros((D_MODEL, W_IN_COLS - 2064), F32)], axis=1)
    w_bf = w_re.astype(BF16)
    a2 = jnp.concatenate([w_a2[l], jnp.zeros((W_IN_COLS - P_LA - GLA_RANK, GLA_QK), F32)], axis=0)
    a2hi, a2lo = _split(a2)
    ba2 = b_a2[l].reshape(1, GLA_QK)
    n1 = norm1[l].reshape(1, D_MODEL)
    gn = gla_norm[l].reshape(1, GLA_WIDTH)
    e2, rep = _gla_constants()

    pwre, pwim, bbre, bbim = _s5_prep(s5_lam_re[l], s5_lam_im[l], s5_log_dt[l], s5_b_re[l], s5_b_im[l])
    eye = jnp.eye(8, dtype=F32)
    blk = lambda t: t.reshape(S5_NBLK, 8, S5_GROUP_CH, S5_STATE)
    bre = _block_diag(blk(bbre), eye).astype(BF16)
    bim = _block_diag(blk(bbim), eye).astype(BF16)
    cre = _block_diag(jnp.swapaxes(blk(s5_c_re[l]), 2, 3), eye).astype(BF16)
    cim = _block_diag(jnp.swapaxes(blk(s5_c_im[l]), 2, 3), eye).astype(BF16)
    s5w = (bre, bim, cre, cim, pwre.reshape(8, S5_LANES), pwim.reshape(8, S5_LANES),
           s5_d[l].reshape(1, S5_WIDTH), w_glu[l].astype(BF16), b_glu[l].reshape(1, S5_WIDTH))

    wt = w_out[l][:GLA_WIDTH].astype(BF16)
    wb = w_out[l][GLA_WIDTH:].astype(BF16)
    n2 = norm2[l].reshape(1, D_MODEL)
    wq_bf = peer_wq[l].astype(BF16)
    kh, kl = _split(peer_keys[l].reshape(2 * PEER_HEADS, PEER_KEYS, PEER_DQ // 2))
    u_bf = peer_u[l].astype(BF16)
    vt_bf = peer_v[l].astype(BF16).T
    fn = final_norm.reshape(1, D_MODEL)

    def tail(x2, og, osx):
        x1 = _out_proj(x2, og, osx, wt, wb)
        n1, e1, rho, e2g = _route(x1, n2, wq_bf, kh, kl)
        return _peer(x1, n2, n1, e1, rho, e2g, u_bf, vt_bf, fn)

    xp = x_prompt.reshape(n_p * len_p, D_MODEL)
    pp = _in_proj(xp, n1, w_bf, a2hi, a2lo, ba2)
    og_p, gla_p = _gla_prompt(pp, n_p, len_p, e2, gn)
    os_p, sre_p, sim_p = _s5_prompt(pp, n_p, len_p, s5w)
    y_p = tail(xp, og_p, os_p).reshape(n_p, len_p, D_MODEL)

    xs = x_sample.reshape(n_s * len_s, D_MODEL)
    ps = _in_proj(xs, n1, w_bf, a2hi, a2lo, ba2)
    og_s, gla_s = _gla_sample(ps, state_gla[l], len_s, e2, rep, gn)
    first_row = lambda s: jnp.pad(s.reshape(n_s, 1, S5_LANES), ((0, 0), (0, len_s - 1), (0, 0))
                                  ).reshape(n_s * len_s, S5_LANES)
    os_s, hre_s, him_s = _s5_sample(ps, first_row(state_s5_re[l]), first_row(state_s5_im[l]), len_s, s5w)
    y_s = tail(xs, og_s, os_s).reshape(n_s, len_s, D_MODEL)
    last_row = lambda hs: hs.reshape(n_s, len_s, S5_GROUPS, S5_STATE)[:, len_s - 1]

    st = lambda a: a.reshape(1, n_p, S5_GROUPS, S5_STATE)
    return (y_p, y_s, gla_p[None], st(sre_p), st(sim_p),
            gla_s[None], last_row(hre_s)[None], last_row(him_s)[None])
```

```python
import functools

import jax
import jax.numpy as jnp
from jax import lax
from jax.experimental import pallas as pl
from jax.experimental.pallas import tpu as pltpu

F32 = jnp.float32
BF16 = jnp.bfloat16

D_MODEL = 1024
GLA_HEADS = 4
GLA_DK = 64
GLA_DV = 128
GLA_QK = GLA_HEADS * GLA_DK
GLA_WIDTH = GLA_HEADS * GLA_DV
GLA_RANK = 16
GLA_TAU = 16.0
GLA_CHUNK = 128
S5_WIDTH = 512
S5_GROUP_CH = 16
S5_GROUPS = 32
S5_STATE = 64
S5_LANES = S5_GROUPS * S5_STATE
S5_NBLK = 4
S5_BLK_CH = S5_WIDTH // S5_NBLK
S5_BLK_ST = S5_LANES // S5_NBLK
PEER_KEYS = 128
PEER_EXPERTS = PEER_KEYS * PEER_KEYS
PEER_HEADS = 8
PEER_DQ = 256
PEER_TOPK = 16
EPS = 1e-6
LANES = 128
GELU_C = 0.7978845608028654
GELU_A = 0.044715

P_Q, P_K, P_V, P_G, P_U, P_LA = 0, 256, 512, 1024, 1536, 2048
P_COLS = 2304
W_IN_COLS = 2176

TOK_TILE = 256
PROJ_TILE = 512
PEER_TOK_TILE = 512
PEER_EXP_TILE = 512
SAMPLE_SEQ_BLOCK = 16
VMEM_LIMIT = 56 * 1024 * 1024


def _split(x):
    hi = x.astype(BF16)
    lo = (x - hi.astype(F32)).astype(BF16)
    return hi, lo


def _split3(x):
    a = x.astype(BF16)
    r = x - a.astype(F32)
    b = r.astype(BF16)
    c = (r - b.astype(F32)).astype(BF16)
    return a, b, c


def _dot(a, b):
    return jnp.dot(a, b, preferred_element_type=F32)


def _dot_nt(a, b):
    return lax.dot_general(a, b, (((1,), (1,)), ((), ())), preferred_element_type=F32)


def _dot_tn(a, b):
    return lax.dot_general(a, b, (((0,), (0,)), ((), ())), preferred_element_type=F32)


def _dot3(a, b_hi, b_lo):
    a_hi, a_lo = _split(a)
    return _dot(a_hi, b_hi) + _dot(a_lo, b_hi) + _dot(a_hi, b_lo)


def _dot_exact01(m01, x):
    a, b, c = _split3(x)
    return _dot(m01, a) + _dot(m01, b) + _dot(m01, c)


def _rms(x):
    return x * lax.rsqrt(jnp.mean(x * x, axis=-1, keepdims=True) + EPS)


def _params(sem):
    return pltpu.CompilerParams(dimension_semantics=sem, vmem_limit_bytes=VMEM_LIMIT)


def _const_spec(shape):
    n = len(shape)
    return pl.BlockSpec(shape, lambda *_: (0,) * n)


def _s5prep_kernel(lr_ref, li_ref, ldt_ref, bret_ref, bimt_ref,
                   pwre_ref, pwim_ref, bbre_ref, bbim_ref):
    lr = lr_ref[...]
    li = li_ref[...]
    dt = jnp.exp(ldt_ref[...])
    mag = jnp.exp(lr * dt)
    abr = mag * jnp.cos(li * dt)
    abi = mag * jnp.sin(li * dt)
    den = lr * lr + li * li
    nr = abr - 1.0
    ni = abi
    fr = (nr * lr + ni * li) / den
    fi = (ni * lr - nr * li) / den
    bret = bret_ref[...]
    bimt = bimt_ref[...]
    bbre_ref[...] = fr[:, None, :] * bret - fi[:, None, :] * bimt
    bbim_ref[...] = fr[:, None, :] * bimt + fi[:, None, :] * bret
    pr, pi = abr, abi
    for i in range(8):
        pwre_ref[i] = pr
        pwim_ref[i] = pi
        pr, pi = pr * abr - pi * abi, pr * abi + pi * abr


def _s5_prep(lam_re, lam_im, log_dt, b_re, b_im):
    g, p = lam_re.shape
    ch = b_re.shape[-1]
    bret = jnp.transpose(b_re, (0, 2, 1))
    bimt = jnp.transpose(b_im, (0, 2, 1))
    out = pl.pallas_call(
        _s5prep_kernel,
        out_shape=(jax.ShapeDtypeStruct((8, g, p), F32), jax.ShapeDtypeStruct((8, g, p), F32),
                   jax.ShapeDtypeStruct((g, ch, p), F32), jax.ShapeDtypeStruct((g, ch, p), F32)),
        name="s5_prep",
    )(lam_re, lam_im, log_dt.reshape(g, 1), bret, bimt)
    return out


def _inproj_kernel(x_ref, n1_ref, w_ref, a2hi_ref, a2lo_ref, ba2_ref, p_ref):
    h = _rms(x_ref[...]) * n1_ref[...]
    p = _dot(h.astype(BF16), w_ref[...])
    alr = p[:, P_LA:W_IN_COLS]
    z = _dot3(alr, a2hi_ref[...], a2lo_ref[...]) + ba2_ref[...]
    log_sig = jnp.minimum(z, 0.0) - jnp.log1p(jnp.exp(-jnp.abs(z)))
    p_ref[:, 0:P_LA] = p[:, 0:P_LA]
    p_ref[:, P_LA:P_COLS] = log_sig * (1.0 / GLA_TAU)


def _in_proj(x2, n1, w_bf, a2hi, a2lo, ba2):
    rows = x2.shape[0]
    return pl.pallas_call(
        _inproj_kernel,
        grid=(rows // PROJ_TILE,),
        in_specs=[pl.BlockSpec((PROJ_TILE, D_MODEL), lambda i: (i, 0)),
                  _const_spec(n1.shape), _const_spec(w_bf.shape),
                  _const_spec(a2hi.shape), _const_spec(a2lo.shape), _const_spec(ba2.shape)],
        out_specs=pl.BlockSpec((PROJ_TILE, P_COLS), lambda i: (i, 0)),
        out_shape=jax.ShapeDtypeStruct((rows, P_COLS), F32),
        compiler_params=_params(("parallel",)),
        name="in_proj",
    )(x2, n1, w_bf, a2hi, a2lo, ba2)


def _cumsum_rows(mask01, la):
    return _dot_exact01(mask01.astype(BF16), la)


GLA_BAND = 4
SUBLANES = 8


def _gla_prompt_kernel(q_ref, k_ref, v_ref, g_ref, la_ref, e2_ref, gn_ref,
                       o_ref, sfin_ref, st_ref, slab, *, n_chunks):
    c = pl.program_id(1)
    C = q_ref.shape[0]
    HC = GLA_HEADS * C

    @pl.when(c == 0)
    def _():
        st_ref[...] = jnp.zeros_like(st_ref)

    la = la_ref[...]
    row = lax.broadcasted_iota(jnp.int32, (C, C), 0)
    col = lax.broadcasted_iota(jnp.int32, (C, C), 1)
    b = _cumsum_rows(col <= row, la)
    q = q_ref[...] * (GLA_DK ** -0.5)
    k = k_ref[...]
    v = v_ref[...]
    blast = b[C - 1:C, :]

    rloc = lax.broadcasted_iota(jnp.int32, (C, GLA_QK), 0) % GLA_BAND
    tiles = lambda t: t.reshape(C // SUBLANES, SUBLANES, t.shape[-1])
    k3, b3, v3 = tiles(k), tiles(b), tiles(v)
    vsh = [v]
    for d in range(GLA_BAND):
        if d == 0:
            ks_, bs_ = k, b
        else:
            ks_ = pltpu.roll(k3, d, 1).reshape(C, GLA_QK)
            bs_ = pltpu.roll(b3, d, 1).reshape(C, GLA_QK)
            vsh.append(pltpu.roll(v3, d, 1).reshape(C, GLA_WIDTH))
        m = q * ks_ * jnp.exp(jnp.minimum(b - bs_, 0.0))
        slab[d * C:(d + 1) * C, :] = jnp.where(rloc >= d, m, 0.0).astype(BF16)
    rep = _dot(slab[...], e2_ref[...])
    o = rep[0:C] * vsh[0]
    for d in range(1, GLA_BAND):
        o = o + rep[d * C:(d + 1) * C] * vsh[d]

    lane_head = lax.broadcasted_iota(jnp.int32, (HC, GLA_QK), 1) // GLA_DK
    row_head = lax.broadcasted_iota(jnp.int32, (HC, GLA_QK), 0) // C
    own_head = lane_head == row_head
    si = lax.broadcasted_iota(jnp.int32, (HC, C), 0) % C
    sj = lax.broadcasted_iota(jnp.int32, (HC, C), 1)
    scores = jnp.zeros((HC, C), F32)
    s = C // 2
    while s >= GLA_BAND:
        ref = jnp.concatenate([jnp.broadcast_to(b[p * 2 * s + s - 1:p * 2 * s + s, :], (2 * s, GLA_QK))
                               for p in range(C // (2 * s))], axis=0)
        ql = q * jnp.exp(jnp.minimum(b - ref, 0.0))
        kl = (k * jnp.exp(jnp.minimum(ref - b, 0.0))).astype(BF16)
        qs = jnp.where(own_head, jnp.concatenate([ql] * GLA_HEADS, axis=0), 0.0).astype(BF16)
        lvl = ((si // (2 * s)) == (sj // (2 * s))) & ((si // s) % 2 == 1) & ((sj // s) % 2 == 0)
        scores = scores + jnp.where(lvl, _dot_nt(qs, kl), 0.0)
        s //= 2
    scores = scores.astype(BF16)
    vb = v.astype(BF16)

    st = st_ref[...]
    o = o + _dot_nt((q * jnp.exp(b)).astype(BF16), st.astype(BF16))
    outs = []
    for h in range(GLA_HEADS):
        vs = slice(h * GLA_DV, (h + 1) * GLA_DV)
        outs.append(_rms(o[:, vs] + _dot(scores[h * C:(h + 1) * C], vb[:, vs])))
    g = g_ref[...]
    o_ref[...] = jnp.concatenate(outs, axis=-1) * gn_ref[...] * (g * jax.nn.sigmoid(g))

    kd_hi, kd_lo = _split(k * jnp.exp(blast - b))
    vt_hi, vt_lo = _split(v.T)
    upd = _dot(vt_hi, kd_hi) + _dot(vt_lo, kd_hi) + _dot(vt_hi, kd_lo)
    blk = (lax.broadcasted_iota(jnp.int32, st.shape, 0) // GLA_DV
           == lax.broadcasted_iota(jnp.int32, st.shape, 1) // GLA_DK)
    st_new = jnp.exp(blast) * st + jnp.where(blk, upd, 0.0)
    st_ref[...] = st_new

    @pl.when(c == n_chunks - 1)
    def _():
        for h in range(GLA_HEADS):
            sfin_ref[0, h] = st_new[h * GLA_DV:(h + 1) * GLA_DV, h * GLA_DK:(h + 1) * GLA_DK].T


def _gla_prompt(p2, n_seq, seq_len, e2, gn):
    C = GLA_CHUNK
    nch = seq_len // C
    rows = n_seq * seq_len

    def tok(width, colblk):
        return pl.BlockSpec((C, width), lambda b, c: (b * nch + c, colblk))

    return pl.pallas_call(
        functools.partial(_gla_prompt_kernel, n_chunks=nch),
        grid=(n_seq, nch),
        in_specs=[tok(GLA_QK, P_Q // GLA_QK), tok(GLA_QK, P_K // GLA_QK),
                  tok(GLA_WIDTH, P_V // GLA_WIDTH), tok(GLA_WIDTH, P_G // GLA_WIDTH),
                  tok(GLA_QK, P_LA // GLA_QK),
                  _const_spec(e2.shape), _const_spec(gn.shape)],
        out_specs=[pl.BlockSpec((C, GLA_WIDTH), lambda b, c: (b * nch + c, 0)),
                   pl.BlockSpec((1, GLA_HEADS, GLA_DK, GLA_DV), lambda b, c: (b, 0, 0, 0))],
        out_shape=(jax.ShapeDtypeStruct((rows, GLA_WIDTH), F32),
                   jax.ShapeDtypeStruct((n_seq, GLA_HEADS, GLA_DK, GLA_DV), F32)),
        scratch_shapes=[pltpu.VMEM((GLA_WIDTH, GLA_QK), F32),
                        pltpu.VMEM((GLA_BAND * C, GLA_QK), BF16)],
        compiler_params=_params(("parallel", "arbitrary")),
        name="gla_prompt",
    )(p2, p2, p2, p2, p2, e2, gn)


def _gla_sample_kernel(q_ref, k_ref, v_ref, g_ref, la_ref, s0_ref, e2_ref, rep_ref, gn_ref,
                       o_ref, snew_ref, *, seq_len):
    R = q_ref.shape[0]
    nseq = R // seq_len
    SD = nseq * GLA_DK
    la = la_ref[...]
    row = lax.broadcasted_iota(jnp.int32, (R, R), 0)
    col = lax.broadcasted_iota(jnp.int32, (R, R), 1)
    same = (col // seq_len) == (row // seq_len)
    b = _cumsum_rows(same & (col <= row), la)
    btot = _cumsum_rows(same, la)
    q = q_ref[...] * (GLA_DK ** -0.5)
    k = k_ref[...]
    v = v_ref[...]
    rmod = lax.broadcasted_iota(jnp.int32, (R, GLA_QK), 0) % seq_len

    o = jnp.zeros((R, GLA_WIDTH), F32)
    for d in range(seq_len):
        ks_, bs_, vs_ = (k, b, v) if d == 0 else (pltpu.roll(k, d, 0), pltpu.roll(b, d, 0),
                                                  pltpu.roll(v, d, 0))
        m = q * ks_ * jnp.exp(jnp.minimum(b - bs_, 0.0))
        m = jnp.where(rmod >= d, m, 0.0)
        o = o + _dot(m.astype(BF16), e2_ref[...]) * vs_

    xr = lax.broadcasted_iota(jnp.int32, (R, GLA_HEADS * SD), 0) // seq_len
    xc = (lax.broadcasted_iota(jnp.int32, (R, GLA_HEADS * SD), 1) % SD) // GLA_DK
    own = xr == xc
    rep = rep_ref[...]

    def expand(x_bf16):
        return jnp.where(own, _dot(x_bf16, rep), 0.0).astype(BF16)

    qx = expand((q * jnp.exp(b)).astype(BF16))
    kd_hi, kd_lo = _split(k * jnp.exp(btot - b))
    kx_hi, kx_lo = expand(kd_hi), expand(kd_lo)
    ea, eb, ec = _split3(jnp.exp(btot))
    ax = (expand(ea), expand(eb), expand(ec))
    last = (lax.broadcasted_iota(jnp.int32, (R, GLA_DV), 0) % seq_len == seq_len - 1).astype(BF16)

    outs = []
    for h in range(GLA_HEADS):
        xs = slice(h * SD, (h + 1) * SD)
        vs = slice(h * GLA_DV, (h + 1) * GLA_DV)
        s0 = s0_ref[:, h].reshape(SD, GLA_DV)
        o_h = o[:, vs] + _dot(qx[:, xs], s0.astype(BF16))
        outs.append(_rms(o_h))
        v_hi, v_lo = _split(v[:, vs])
        upd = _dot_tn(kx_hi[:, xs], v_hi) + _dot_tn(kx_lo[:, xs], v_hi) + _dot_tn(kx_hi[:, xs], v_lo)
        decay = _dot_tn(ax[0][:, xs], last) + _dot_tn(ax[1][:, xs], last) + _dot_tn(ax[2][:, xs], last)
        snew_ref[:, h] = (decay * s0 + upd).reshape(nseq, GLA_DK, GLA_DV)
    g = g_ref[...]
    o_ref[...] = jnp.concatenate(outs, axis=-1) * gn_ref[...] * (g * jax.nn.sigmoid(g))


def _gla_sample(p2, s0, seq_len, e2, rep, gn):
    n_seq = s0.shape[0]
    R = SAMPLE_SEQ_BLOCK * seq_len
    nblk = n_seq // SAMPLE_SEQ_BLOCK

    def tok(width, colblk):
        return pl.BlockSpec((R, width), lambda i: (i, colblk))

    st_spec = pl.BlockSpec((SAMPLE_SEQ_BLOCK, GLA_HEADS, GLA_DK, GLA_DV), lambda i: (i, 0, 0, 0))
    return pl.pallas_call(
        functools.partial(_gla_sample_kernel, seq_len=seq_len),
        grid=(nblk,),
        in_specs=[tok(GLA_QK, P_Q // GLA_QK), tok(GLA_QK, P_K // GLA_QK),
                  tok(GLA_WIDTH, P_V // GLA_WIDTH), tok(GLA_WIDTH, P_G // GLA_WIDTH),
                  tok(GLA_QK, P_LA // GLA_QK), st_spec,
                  _const_spec(e2.shape), _const_spec(rep.shape), _const_spec(gn.shape)],
        out_specs=[pl.BlockSpec((R, GLA_WIDTH), lambda i: (i, 0)), st_spec],
        out_shape=(jax.ShapeDtypeStruct((n_seq * seq_len, GLA_WIDTH), F32),
                   jax.ShapeDtypeStruct(s0.shape, F32)),
        compiler_params=_params(("parallel",)),
        name="gla_sample",
    )(p2, p2, p2, p2, p2, s0, e2, rep, gn)


def _s5_local_scan(bur, bui, pwre_ref, pwim_ref, lanes, group):
    rows, width = bur.shape
    xr = bur.reshape(rows // 8, 8, width)
    xi = bui.reshape(rows // 8, 8, width)
    sub = lax.broadcasted_iota(jnp.int32, (8, width), 0) % group
    s = 1
    while s < group:
        ar = jnp.where(sub >= s, pwre_ref[s - 1:s, lanes], 0.0)[None]
        ai = jnp.where(sub >= s, pwim_ref[s - 1:s, lanes], 0.0)[None]
        sr = pltpu.roll(xr, s, 1)
        si = pltpu.roll(xi, s, 1)
        xr, xi = xr + ar * sr - ai * si, xi + ar * si + ai * sr
        s *= 2
    return xr.reshape(rows, width), xi.reshape(rows, width)


def _s5_tail(ys, u, d_ref, wglu_ref, bglu_ref):
    y = jnp.concatenate(ys, axis=-1) + d_ref[...] * u
    z = jax.nn.gelu(y)
    return z * jax.nn.sigmoid(_dot(z.astype(BF16), wglu_ref[...]) + bglu_ref[...])


def _s5_prompt_kernel(u_ref, bre_ref, bim_ref, cre_ref, cim_ref,
                      pwre_ref, pwim_ref, d_ref, wglu_ref, bglu_ref,
                      o_ref, stre_ref, stim_ref, hre_s, him_s, car_re, car_im):
    t = pl.program_id(1)
    TT = u_ref.shape[0]

    @pl.when(t == 0)
    def _():
        car_re[...] = jnp.zeros_like(car_re)
        car_im[...] = jnp.zeros_like(car_im)

    u = u_ref[...]
    ys = []
    for l in range(S5_NBLK):
        lanes = slice(l * S5_BLK_ST, (l + 1) * S5_BLK_ST)
        ul = u[:, l * S5_BLK_CH:(l + 1) * S5_BLK_CH]
        ub = ul.astype(BF16)
        bur = _dot(ub, bre_ref[l])
        bui = _dot(ub, bim_ref[l])
        bur, bui = _s5_local_scan(bur, bui, pwre_ref, pwim_ref, lanes, 8)
        hre_s[...] = bur
        him_s[...] = bui
        p8r = pwre_ref[:, lanes]
        p8i = pwim_ref[:, lanes]

        def grp(r, carry):
            cr, ci = carry
            off = pl.multiple_of(r * 8, 8)
            xr = hre_s[pl.ds(off, 8), :] + p8r * cr - p8i * ci
            xi = him_s[pl.ds(off, 8), :] + p8r * ci + p8i * cr
            hre_s[pl.ds(off, 8), :] = xr
            him_s[pl.ds(off, 8), :] = xi
            return (jnp.broadcast_to(xr[7:8], xr.shape), jnp.broadcast_to(xi[7:8], xi.shape))

        cr0 = jnp.broadcast_to(car_re[:, lanes], (8, S5_BLK_ST))
        ci0 = jnp.broadcast_to(car_im[:, lanes], (8, S5_BLK_ST))
        cr, ci = lax.fori_loop(0, TT // 8, grp, (cr0, ci0))
        car_re[:, lanes] = cr[0:1]
        car_im[:, lanes] = ci[0:1]
        ys.append(_dot(hre_s[...].astype(BF16), cre_ref[l]) - _dot(him_s[...].astype(BF16), cim_ref[l]))
    o_ref[...] = _s5_tail(ys, u, d_ref, wglu_ref, bglu_ref)
    stre_ref[0] = car_re[...]
    stim_ref[0] = car_im[...]


def _s5_sample_kernel(u_ref, s0re_ref, s0im_ref, bre_ref, bim_ref, cre_ref, cim_ref,
                      pwre_ref, pwim_ref, d_ref, wglu_ref, bglu_ref,
                      o_ref, hre_ref, him_ref, *, seq_len):
    u = u_ref[...]
    ys = []
    for l in range(S5_NBLK):
        lanes = slice(l * S5_BLK_ST, (l + 1) * S5_BLK_ST)
        ul = u[:, l * S5_BLK_CH:(l + 1) * S5_BLK_CH]
        ar = pwre_ref[0:1, lanes]
        ai = pwim_ref[0:1, lanes]
        sr = s0re_ref[:, lanes]
        si = s0im_ref[:, lanes]
        ub = ul.astype(BF16)
        bur = _dot(ub, bre_ref[l]) + (ar * sr - ai * si)
        bui = _dot(ub, bim_ref[l]) + (ar * si + ai * sr)
        bur, bui = _s5_local_scan(bur, bui, pwre_ref, pwim_ref, lanes, seq_len)
        hre_ref[:, lanes] = bur
        him_ref[:, lanes] = bui
        ys.append(_dot(bur.astype(BF16), cre_ref[l]) - _dot(bui.astype(BF16), cim_ref[l]))
    o_ref[...] = _s5_tail(ys, u, d_ref, wglu_ref, bglu_ref)


def _s5_weight_specs(ws):
    return [_const_spec(w.shape) for w in ws]


def _s5_prompt(p2, n_seq, seq_len, ws):
    TT = PROJ_TILE
    nt = seq_len // TT
    rows = n_seq * seq_len
    st_spec = pl.BlockSpec((1, 1, S5_LANES), lambda b, t: (b, 0, 0))
    return pl.pallas_call(
        _s5_prompt_kernel,
        grid=(n_seq, nt),
        in_specs=[pl.BlockSpec((TT, S5_WIDTH), lambda b, t: (b * nt + t, P_U // S5_WIDTH))]
        + _s5_weight_specs(ws),
        out_specs=[pl.BlockSpec((TT, S5_WIDTH), lambda b, t: (b * nt + t, 0)), st_spec, st_spec],
        out_shape=(jax.ShapeDtypeStruct((rows, S5_WIDTH), F32),
                   jax.ShapeDtypeStruct((n_seq, 1, S5_LANES), F32),
                   jax.ShapeDtypeStruct((n_seq, 1, S5_LANES), F32)),
        scratch_shapes=[pltpu.VMEM((TT, S5_BLK_ST), F32), pltpu.VMEM((TT, S5_BLK_ST), F32),
                        pltpu.VMEM((1, S5_LANES), F32), pltpu.VMEM((1, S5_LANES), F32)],
        compiler_params=_params(("parallel", "arbitrary")),
        name="s5_prompt",
    )(p2, *ws)


def _s5_sample(p2, s0re_rows, s0im_rows, seq_len, ws):
    rows = s0re_rows.shape[0]
    TT = TOK_TILE
    row_spec = pl.BlockSpec((TT, S5_LANES), lambda i: (i, 0))
    return pl.pallas_call(
        functools.partial(_s5_sample_kernel, seq_len=seq_len),
        grid=(rows // TT,),
        in_specs=[pl.BlockSpec((TT, S5_WIDTH), lambda i: (i, P_U // S5_WIDTH)), row_spec, row_spec]
        + _s5_weight_specs(ws),
        out_specs=[pl.BlockSpec((TT, S5_WIDTH), lambda i: (i, 0)), row_spec, row_spec],
        out_shape=(jax.ShapeDtypeStruct((rows, S5_WIDTH), F32),
                   jax.ShapeDtypeStruct((rows, S5_LANES), F32),
                   jax.ShapeDtypeStruct((rows, S5_LANES), F32)),
        compiler_params=_params(("parallel",)),
        name="s5_sample",
    )(p2, s0re_rows, s0im_rows, *ws)


def _outproj_kernel(x_ref, og_ref, os_ref, wt_ref, wb_ref, x1_ref):
    x1 = (x_ref[...] + _dot(og_ref[...].astype(BF16), wt_ref[...])
          + _dot(os_ref[...].astype(BF16), wb_ref[...]))
    x1_ref[...] = x1


def _out_proj(x2, og, osx, wt, wb):
    rows = x2.shape[0]
    big = pl.BlockSpec((PROJ_TILE, D_MODEL), lambda i: (i, 0))
    half = pl.BlockSpec((PROJ_TILE, GLA_WIDTH), lambda i: (i, 0))
    return pl.pallas_call(
        _outproj_kernel,
        grid=(rows // PROJ_TILE,),
        in_specs=[big, half, half] + [_const_spec(w.shape) for w in (wt, wb)],
        out_specs=big,
        out_shape=jax.ShapeDtypeStruct((rows, D_MODEL), F32),
        compiler_params=_params(("parallel",)),
        name="out_proj",
    )(x2, og, osx, wt, wb)


def _first_and_nth_max(work, n):
    first = None
    m = None
    for r in range(n):
        m = jnp.max(work, axis=0, keepdims=True)
        if r == 0:
            first = m
        if r < n - 1:
            work = jnp.where(work == m, -jnp.inf, work)
    return first, m


def _sorted_columns(s):
    n = s.shape[0] // 8
    cols = [s[8 * v:8 * (v + 1)] for v in range(n)]
    k = 2
    while k <= n:
        j = k // 2
        while j >= 1:
            for i in range(n):
                l = i ^ j
                if l > i:
                    hi, lo = jnp.maximum(cols[i], cols[l]), jnp.minimum(cols[i], cols[l])
                    cols[i], cols[l] = (hi, lo) if (i & k) == 0 else (lo, hi)
            j //= 2
        k *= 2
    return cols


def _top_values(s, n, store):
    cols = _sorted_columns(s)
    for r in range(n):
        m = jnp.max(cols[0], axis=0, keepdims=True)
        store(r, m)
        if r < n - 1:
            hit = cols[0] == m
            for v in range(n - 1 - r):
                nxt = cols[v + 1] if v + 1 < len(cols) else jnp.full_like(cols[v], -jnp.inf)
                cols[v] = jnp.where(hit, nxt, cols[v])


def _route_kernel(x1_ref, n2_ref, wq_ref, kh_ref, kl_ref,
                  n1_ref, e1_ref, rho_ref, e2_ref, s_s, top_s):
    TM = x1_ref.shape[0]
    h2 = _rms(x1_ref[...]) * n2_ref[...]
    qp = _dot(h2.astype(BF16), wq_ref[...])
    for hp in range(2 * PEER_HEADS):
        q_hi, q_lo = _split(qp[:, hp * PEER_KEYS:(hp + 1) * PEER_KEYS])
        s_s[hp] = _dot_nt(kh_ref[hp], q_hi) + _dot_nt(kl_ref[hp], q_hi) + _dot_nt(kh_ref[hp], q_lo)
    K = PEER_TOPK
    for lc in range(TM // LANES):
        sl = slice(lc * LANES, (lc + 1) * LANES)
        for hp in range(2 * PEER_HEADS):
            def store(r, m, hp=hp):
                top_s[hp, r:r + 1, sl] = m

            _top_values(s_s[hp, :, sl], K, store)
        for h in range(PEER_HEADS):
            v1 = top_s[2 * h, :, sl]
            v2 = top_s[2 * h + 1, :, sl]
            cands = [v1 + v2[0:1]] + [v1[0:8] + v2[b:b + 1] for b in range(1, 8)] + [v2[8:K] + v1[0:1]]
            cand = jnp.concatenate(cands, axis=0)
            mx, thr = _first_and_nth_max(cand, K)
            z = jnp.sum(jnp.where(cand >= thr, jnp.exp(cand - mx), 0.0), axis=0, keepdims=True)
            cnt = jnp.zeros(v1.shape, F32)
            for b in range(K):
                cnt = cnt + jnp.where(v1 + v2[b:b + 1] >= thr, 1.0, 0.0)
            s1 = s_s[2 * h, :, sl]
            s2 = s_s[2 * h + 1, :, sl]
            n1 = jnp.zeros(s1.shape, F32)
            rho = jnp.full(s2.shape, float(K), F32)
            for a in range(K):
                n1 = jnp.where(s1 == v1[a:a + 1], cnt[a:a + 1], n1)
                rho = jnp.where(s2 == v2[a:a + 1], float(a), rho)
            n1_ref[h, :, sl] = n1
            e1_ref[h, :, sl] = jnp.exp(s1 - v1[0:1])
            rho_ref[h, :, sl] = rho.astype(BF16)
            e2_ref[h, :, sl] = (jnp.exp(s2 - v2[0:1]) * (1.0 / z)).astype(BF16)


def _route(x1, n2, wq_bf, kh, kl):
    rows = x1.shape[0]
    TM = TOK_TILE
    key_spec = pl.BlockSpec((PEER_HEADS, PEER_KEYS, TM), lambda i: (0, 0, i))
    return pl.pallas_call(
        _route_kernel,
        grid=(rows // TM,),
        in_specs=[pl.BlockSpec((TM, D_MODEL), lambda i: (i, 0)),
                  _const_spec(n2.shape), _const_spec(wq_bf.shape),
                  _const_spec(kh.shape), _const_spec(kl.shape)],
        out_specs=[key_spec] * 4,
        out_shape=(jax.ShapeDtypeStruct((PEER_HEADS, PEER_KEYS, rows), F32),
                   jax.ShapeDtypeStruct((PEER_HEADS, PEER_KEYS, rows), F32),
                   jax.ShapeDtypeStruct((PEER_HEADS, PEER_KEYS, rows), BF16),
                   jax.ShapeDtypeStruct((PEER_HEADS, PEER_KEYS, rows), BF16)),
        scratch_shapes=[pltpu.VMEM((2 * PEER_HEADS, PEER_KEYS, TM), F32),
                        pltpu.VMEM((2 * PEER_HEADS, PEER_TOPK, TM), F32)],
        compiler_params=_params(("parallel",)),
        name="peer_route",
    )(x1, n2, wq_bf, kh, kl)


PEER_CHUNK = 256


def _peer_gate_piece(i1, c, a, act_ref, wact_ref, n1_ref, e1_ref, rho_ref, e2_ref):
    ls = slice(c * PEER_CHUNK, (c + 1) * PEER_CHUNK)
    rs = slice(a * PEER_KEYS, (a + 1) * PEER_KEYS)
    w = jnp.zeros((PEER_KEYS, PEER_CHUNK), BF16)
    for h in range(PEER_HEADS):
        n1 = n1_ref[h, pl.ds(i1, 1), ls].astype(BF16)
        g1 = e1_ref[h, pl.ds(i1, 1), ls].astype(BF16)
        e2 = e2_ref[h, :, ls]
        w = w + jnp.where(rho_ref[h, :, ls] < n1, e2, jnp.zeros_like(e2)) * g1
    x = act_ref[rs, ls]
    t = jnp.exp((x * (-2.0 * GELU_C - (2.0 * GELU_C * GELU_A) * (x * x))).astype(BF16))
    wact_ref[rs, ls] = w * (x.astype(BF16) / (1.0 + t))


def _peer_half(blk_gate, u_half, vt_half, act_in, wact_out, wact_in, act_out,
               acc, h2b, n1_ref, e1_ref, rho_ref, e2_ref, stages):
    sub = act_in.shape[0] // PEER_KEYS
    for c in range(act_in.shape[1] // PEER_CHUNK):
        ls = slice(c * PEER_CHUNK, (c + 1) * PEER_CHUNK)
        if "C" in stages:
            acc[:, ls] += _dot(vt_half, wact_in[:, ls])
        if "B" in stages:
            for a in range(sub):
                _peer_gate_piece(blk_gate * sub + a, c, a, act_in, wact_out, n1_ref, e1_ref, rho_ref, e2_ref)
        if "A" in stages:
            act_out[:, ls] = _dot_nt(u_half, h2b[c * PEER_CHUNK:(c + 1) * PEER_CHUNK, :])


def _peer_kernel(x1_ref, n2_ref, n1_ref, e1_ref, rho_ref, e2_ref, u_ref, vt_ref, fn_ref, y_ref,
                 acc, h2b, wact0, wact1, act0, act1, *, n_steps):
    g = pl.program_id(1)
    EB = act0.shape[0]
    refs = (acc, h2b, n1_ref, e1_ref, rho_ref, e2_ref)

    def halves(first, second):
        _peer_half(2 * g - 1, u_ref[0:EB, :], vt_ref[:, 0:EB], act1, wact1, wact0, act0, *refs, stages=first)
        _peer_half(2 * g, u_ref[EB:2 * EB, :], vt_ref[:, EB:2 * EB], act0, wact0, wact1, act1, *refs,
                   stages=second)

    @pl.when(g == 0)
    def _():
        acc[...] = jnp.zeros_like(acc)
        h2b[...] = (_rms(x1_ref[...]) * n2_ref[...]).astype(BF16)
        halves("A", "AB")

    @pl.when((g > 0) & (g < n_steps - 1))
    def _():
        halves("ABC", "ABC")

    @pl.when(g == n_steps - 1)
    def _():
        halves("BC", "C")
        out = x1_ref[...] + acc[...].T
        y_ref[...] = _rms(out) * fn_ref[...]


def _peer(x1, n2, n1, e1, rho, e2, u_bf, vt_bf, fn):
    rows = x1.shape[0]
    TM = PEER_TOK_TILE
    EB = PEER_EXP_TILE
    n_steps = PEER_EXPERTS // (2 * EB) + 1
    big = pl.BlockSpec((TM, D_MODEL), lambda i, g: (i, 0))
    key_spec = pl.BlockSpec((PEER_HEADS, PEER_KEYS, TM), lambda i, g: (0, 0, i))
    return pl.pallas_call(
        functools.partial(_peer_kernel, n_steps=n_steps),
        grid=(rows // TM, n_steps),
        in_specs=[big, _const_spec(n2.shape), key_spec, key_spec, key_spec, key_spec,
                  pl.BlockSpec((2 * EB, D_MODEL), lambda i, g: (jnp.minimum(g, n_steps - 2), 0)),
                  pl.BlockSpec((D_MODEL, 2 * EB), lambda i, g: (0, jnp.maximum(g - 1, 0))),
                  _const_spec(fn.shape)],
        out_specs=big,
        out_shape=jax.ShapeDtypeStruct((rows, D_MODEL), F32),
        scratch_shapes=[pltpu.VMEM((D_MODEL, TM), F32), pltpu.VMEM((TM, D_MODEL), BF16),
                        pltpu.VMEM((EB, TM), BF16), pltpu.VMEM((EB, TM), BF16),
                        pltpu.VMEM((EB, TM), F32), pltpu.VMEM((EB, TM), F32)],
        compiler_params=_params(("parallel", "arbitrary")),
        name="peer_experts",
    )(x1, n2, n1, e1, rho, e2, u_bf, vt_bf, fn)


def _block_diag(w, eye):
    n, g, a, b = w.shape
    return jnp.einsum('lgab,gh->lgahb', w, eye).reshape(n, g * a, g * b)


def _gla_constants():
    lane_head = jnp.arange(GLA_QK) // GLA_DK
    e2 =(lane_head[:, None] == (jnp.arange(GLA_WIDTH) // GLA_DV)[None, :]).astype(BF16)
    sd = SAMPLE_SEQ_BLOCK * GLA_DK
    xc = jnp.arange(GLA_HEADS * sd)
    rep = ((lane_head[:, None] == (xc // sd)[None, :])
           & ((jnp.arange(GLA_QK) % GLA_DK)[:, None] == (xc % GLA_DK)[None, :])).astype(BF16)
    return e2, rep


def kernel(x_prompt, x_sample, state_gla, state_s5_re, state_s5_im, norm1, w_in, w_a2, b_a2, gla_norm, s5_lam_re, s5_lam_im, s5_log_dt, s5_b_re, s5_b_im, s5_c_re, s5_c_im, s5_d, w_glu, b_glu, w_out, norm2, peer_wq, peer_keys, peer_u, peer_v, final_norm):
    depth = norm1.shape[0]
    assert depth == 1, "single-layer trunk"
    n_p, len_p, _ = x_prompt.shape
    n_s, len_s, _ = x_sample.shape
    l = 0

    w = w_in[l]
    w_re = jnp.concatenate([w[:, 0:1536], w[:, 1552:2064], w[:, 1536:1552],
                            jnp.zeros((D_MODEL, W_IN_COLS - 2064), F32)], axis=1)
    w_bf = w_re.astype(BF16)
    a2 = jnp.concatenate([w_a2[l], jnp.zeros((W_IN_COLS - P_LA - GLA_RANK, GLA_QK), F32)], axis=0)
    a2hi, a2lo = _split(a2)
    ba2 = b_a2[l].reshape(1, GLA_QK)
    n1 = norm1[l].reshape(1, D_MODEL)
    gn = gla_norm[l].reshape(1, GLA_WIDTH)
    e2, rep = _gla_constants()

    pwre, pwim, bbre, bbim = _s5_prep(s5_lam_re[l], s5_lam_im[l], s5_log_dt[l], s5_b_re[l], s5_b_im[l])
    eye = jnp.eye(8, dtype=F32)
    blk = lambda t: t.reshape(S5_NBLK, 8, S5_GROUP_CH, S5_STATE)
    bre = _block_diag(blk(bbre), eye).astype(BF16)
    bim = _block_diag(blk(bbim), eye).astype(BF16)
    cre = _block_diag(jnp.swapaxes(blk(s5_c_re[l]), 2, 3), eye).astype(BF16)
    cim = _block_diag(jnp.swapaxes(blk(s5_c_im[l]), 2, 3), eye).astype(BF16)
    s5w = (bre, bim, cre, cim, pwre.reshape(8, S5_LANES), pwim.reshape(8, S5_LANES),
           s5_d[l].reshape(1, S5_WIDTH), w_glu[l].astype(BF16), b_glu[l].reshape(1, S5_WIDTH))

    wt = w_out[l][:GLA_WIDTH].astype(BF16)
    wb = w_out[l][GLA_WIDTH:].astype(BF16)
    n2 = norm2[l].reshape(1, D_MODEL)
    wq_bf = peer_wq[l].astype(BF16)
    kh, kl = _split(peer_keys[l].reshape(2 * PEER_HEADS, PEER_KEYS, PEER_DQ // 2))
    u_bf = peer_u[l].astype(BF16)
    vt_bf = peer_v[l].astype(BF16).T
    fn = final_norm.reshape(1, D_MODEL)

    def tail(x2, og, osx):
        x1 = _out_proj(x2, og, osx, wt, wb)
        n1, e1, rho, e2g = _route(x1, n2, wq_bf, kh, kl)
        return _peer(x1, n2, n1, e1, rho, e2g, u_bf, vt_bf, fn)

    xp = x_prompt.reshape(n_p * len_p, D_MODEL)
    pp = _in_proj(xp, n1, w_bf, a2hi, a2lo, ba2)
    og_p, gla_p = _gla_prompt(pp, n_p, len_p, e2, gn)
    os_p, sre_p, sim_p = _s5_prompt(pp, n_p, len_p, s5w)
    y_p = tail(xp, og_p, os_p).reshape(n_p, len_p, D_MODEL)

    xs = x_sample.reshape(n_s * len_s, D_MODEL)
    ps = _in_proj(xs, n1, w_bf, a2hi, a2lo, ba2)
    og_s, gla_s = _gla_sample(ps, state_gla[l], len_s, e2, rep, gn)
    first_row = lambda s: jnp.pad(s.reshape(n_s, 1, S5_LANES), ((0, 0), (0, len_s - 1), (0, 0))
                                  ).reshape(n_s * len_s, S5_LANES)
    os_s, hre_s, him_s = _s5_sample(ps, first_row(state_s5_re[l]), first_row(state_s5_im[l]), len_s, s5w)
    y_s = tail(xs, og_s, os_s).reshape(n_s, len_s, D_MODEL)
    last_row = lambda hs: hs.reshape(n_s, len_s, S5_GROUPS, S5_STATE)[:, len_s - 1]

    st = lambda a: a.reshape(1, n_p, S5_GROUPS, S5_STATE)
    return (y_p, y_s, gla_p[None], st(sre_p), st(sim_p),
            gla_s[None], last_row(hre_s)[None], last_row(him_s)[None])
```

```python
import functools

import jax
import jax.numpy as jnp
from jax import lax
from jax.experimental import pallas as pl
from jax.experimental.pallas import tpu as pltpu

F32 = jnp.float32
BF16 = jnp.bfloat16

D_MODEL = 1024
GLA_HEADS = 4
GLA_DK = 64
GLA_DV = 128
GLA_QK = GLA_HEADS * GLA_DK
GLA_WIDTH = GLA_HEADS * GLA_DV
GLA_RANK = 16
GLA_TAU = 16.0
GLA_CHUNK = 128
S5_WIDTH = 512
S5_GROUP_CH = 16
S5_GROUPS = 32
S5_STATE = 64
S5_LANES = S5_GROUPS * S5_STATE
S5_NBLK = 4
S5_BLK_CH = S5_WIDTH // S5_NBLK
S5_BLK_ST = S5_LANES // S5_NBLK
PEER_KEYS = 128
PEER_EXPERTS = PEER_KEYS * PEER_KEYS
PEER_HEADS = 8
PEER_DQ = 256
PEER_TOPK = 16
EPS = 1e-6
LANES = 128
GELU_C = 0.7978845608028654
GELU_A = 0.044715

P_Q, P_K, P_V, P_G, P_U, P_LA = 0, 256, 512, 1024, 1536, 2048
P_COLS = 2304
W_IN_COLS = 2176

TOK_TILE = 256
PROJ_TILE = 512
PEER_TOK_TILE = 512
PEER_EXP_TILE = 512
SAMPLE_SEQ_BLOCK = 16
VMEM_LIMIT = 56 * 1024 * 1024


def _split(x):
    hi = x.astype(BF16)
    lo = (x - hi.astype(F32)).astype(BF16)
    return hi, lo


def _split3(x):
    a = x.astype(BF16)
    r = x - a.astype(F32)
    b = r.astype(BF16)
    c = (r - b.astype(F32)).astype(BF16)
    return a, b, c


def _dot(a, b):
    return jnp.dot(a, b, preferred_element_type=F32)


def _dot_nt(a, b):
    return lax.dot_general(a, b, (((1,), (1,)), ((), ())), preferred_element_type=F32)


def _dot_tn(a, b):
    return lax.dot_general(a, b, (((0,), (0,)), ((), ())), preferred_element_type=F32)


def _dot3(a, b_hi, b_lo):
    a_hi, a_lo = _split(a)
    return _dot(a_hi, b_hi) + _dot(a_lo, b_hi) + _dot(a_hi, b_lo)


def _dot_exact01(m01, x):
    a, b, c = _split3(x)
    return _dot(m01, a) + _dot(m01, b) + _dot(m01, c)


def _rms(x):
    return x * lax.rsqrt(jnp.mean(x * x, axis=-1, keepdims=True) + EPS)


def _params(sem):
    return pltpu.CompilerParams(dimension_semantics=sem, vmem_limit_bytes=VMEM_LIMIT)


def _const_spec(shape):
    n = len(shape)
    return pl.BlockSpec(shape, lambda *_: (0,) * n)


def _s5prep_kernel(lr_ref, li_ref, ldt_ref, bret_ref, bimt_ref,
                   pwre_ref, pwim_ref, bbre_ref, bbim_ref):
    lr = lr_ref[...]
    li = li_ref[...]
    dt = jnp.exp(ldt_ref[...])
    mag = jnp.exp(lr * dt)
    abr = mag * jnp.cos(li * dt)
    abi = mag * jnp.sin(li * dt)
    den = lr * lr + li * li
    nr = abr - 1.0
    ni = abi
    fr = (nr * lr + ni * li) / den
    fi = (ni * lr - nr * li) / den
    bret = bret_ref[...]
    bimt = bimt_ref[...]
    bbre_ref[...] = fr[:, None, :] * bret - fi[:, None, :] * bimt
    bbim_ref[...] = fr[:, None, :] * bimt + fi[:, None, :] * bret
    pr, pi = abr, abi
    for i in range(8):
        pwre_ref[i] = pr
        pwim_ref[i] = pi
        pr, pi = pr * abr - pi * abi, pr * abi + pi * abr


def _s5_prep(lam_re, lam_im, log_dt, b_re, b_im):
    g, p = lam_re.shape
    ch = b_re.shape[-1]
    bret = jnp.transpose(b_re, (0, 2, 1))
    bimt = jnp.transpose(b_im, (0, 2, 1))
    out = pl.pallas_call(
        _s5prep_kernel,
        out_shape=(jax.ShapeDtypeStruct((8, g, p), F32), jax.ShapeDtypeStruct((8, g, p), F32),
                   jax.ShapeDtypeStruct((g, ch, p), F32), jax.ShapeDtypeStruct((g, ch, p), F32)),
        name="s5_prep",
    )(lam_re, lam_im, log_dt.reshape(g, 1), bret, bimt)
    return out


def _inproj_kernel(x_ref, n1_ref, w_ref, a2hi_ref, a2lo_ref, ba2_ref, p_ref):
    h = _rms(x_ref[...]) * n1_ref[...]
    p = _dot(h.astype(BF16), w_ref[...])
    alr = p[:, P_LA:W_IN_COLS]
    z = _dot3(alr, a2hi_ref[...], a2lo_ref[...]) + ba2_ref[...]
    log_sig = jnp.minimum(z, 0.0) - jnp.log1p(jnp.exp(-jnp.abs(z)))
    p_ref[:, 0:P_LA] = p[:, 0:P_LA]
    p_ref[:, P_LA:P_COLS] = log_sig * (1.0 / GLA_TAU)


def _in_proj(x2, n1, w_bf, a2hi, a2lo, ba2):
    rows = x2.shape[0]
    return pl.pallas_call(
        _inproj_kernel,
        grid=(rows // PROJ_TILE,),
        in_specs=[pl.BlockSpec((PROJ_TILE, D_MODEL), lambda i: (i, 0)),
                  _const_spec(n1.shape), _const_spec(w_bf.shape),
                  _const_spec(a2hi.shape), _const_spec(a2lo.shape), _const_spec(ba2.shape)],
        out_specs=pl.BlockSpec((PROJ_TILE, P_COLS), lambda i: (i, 0)),
        out_shape=jax.ShapeDtypeStruct((rows, P_COLS), F32),
        compiler_params=_params(("parallel",)),
        name="in_proj",
    )(x2, n1, w_bf, a2hi, a2lo, ba2)


def _cumsum_rows(mask01, la):
    return _dot_exact01(mask01.astype(BF16), la)


GLA_BAND = 4
SUBLANES = 8


def _gla_prompt_kernel(q_ref, k_ref, v_ref, g_ref, la_ref, e2_ref, gn_ref,
                       o_ref, sfin_ref, st_ref, slab, *, n_chunks):
    c = pl.program_id(1)
    C = q_ref.shape[0]
    HC = GLA_HEADS * C

    @pl.when(c == 0)
    def _():
        st_ref[...] = jnp.zeros_like(st_ref)

    la = la_ref[...]
    row = lax.broadcasted_iota(jnp.int32, (C, C), 0)
    col = lax.broadcasted_iota(jnp.int32, (C, C), 1)
    b = _cumsum_rows(col <= row, la)
    q = q_ref[...] * (GLA_DK ** -0.5)
    k = k_ref[...]
    v = v_ref[...]
    blast = b[C - 1:C, :]

    rloc = lax.broadcasted_iota(jnp.int32, (C, GLA_QK), 0) % GLA_BAND
    tiles = lambda t: t.reshape(C // SUBLANES, SUBLANES, t.shape[-1])
    k3, b3, v3 = tiles(k), tiles(b), tiles(v)
    vsh = [v]
    for d in range(GLA_BAND):
        if d == 0:
            ks_, bs_ = k, b
        else:
            ks_ = pltpu.roll(k3, d, 1).reshape(C, GLA_QK)
            bs_ = pltpu.roll(b3, d, 1).reshape(C, GLA_QK)
            vsh.append(pltpu.roll(v3, d, 1).reshape(C, GLA_WIDTH))
        m = q * ks_ * jnp.exp(jnp.minimum(b - bs_, 0.0))
        slab[d * C:(d + 1) * C, :] = jnp.where(rloc >= d, m, 0.0).astype(BF16)
    rep = _dot(slab[...], e2_ref[...])
    o = rep[0:C] * vsh[0]
    for d in range(1, GLA_BAND):
        o = o + rep[d * C:(d + 1) * C] * vsh[d]

    lane_head = lax.broadcasted_iota(jnp.int32, (HC, GLA_QK), 1) // GLA_DK
    row_head = lax.broadcasted_iota(jnp.int32, (HC, GLA_QK), 0) // C
    own_head = lane_head == row_head
    si = lax.broadcasted_iota(jnp.int32, (HC, C), 0) % C
    sj = lax.broadcasted_iota(jnp.int32, (HC, C), 1)
    scores = jnp.zeros((HC, C), F32)
    s = C // 2
    while s >= GLA_BAND:
        ref = jnp.concatenate([jnp.broadcast_to(b[p * 2 * s + s - 1:p * 2 * s + s, :], (2 * s, GLA_QK))
                               for p in range(C // (2 * s))], axis=0)
        ql = q * jnp.exp(jnp.minimum(b - ref, 0.0))
        kl = (k * jnp.exp(jnp.minimum(ref - b, 0.0))).astype(BF16)
        qs = jnp.where(own_head, jnp.concatenate([ql] * GLA_HEADS, axis=0), 0.0).astype(BF16)
        lvl = ((si // (2 * s)) == (sj // (2 * s))) & ((si // s) % 2 == 1) & ((sj // s) % 2 == 0)
        scores = scores + jnp.where(lvl, _dot_nt(qs, kl), 0.0)
        s //= 2
    scores = scores.astype(BF16)
    vb = v.astype(BF16)

    st = st_ref[...]
    o = o + _dot_nt((q * jnp.exp(b)).astype(BF16), st.astype(BF16))
    outs = []
    for h in range(GLA_HEADS):
        vs = slice(h * GLA_DV, (h + 1) * GLA_DV)
        outs.append(_rms(o[:, vs] + _dot(scores[h * C:(h + 1) * C], vb[:, vs])))
    g = g_ref[...]
    o_ref[...] = jnp.concatenate(outs, axis=-1) * gn_ref[...] * (g * jax.nn.sigmoid(g))

    kd_hi, kd_lo = _split(k * jnp.exp(blast - b))
    vt_hi, vt_lo = _split(v.T)
    upd = _dot(vt_hi, kd_hi) + _dot(vt_lo, kd_hi) + _dot(vt_hi, kd_lo)
    blk = (lax.broadcasted_iota(jnp.int32, st.shape, 0) // GLA_DV
           == lax.broadcasted_iota(jnp.int32, st.shape, 1) // GLA_DK)
    st_new = jnp.exp(blast) * st + jnp.where(blk, upd, 0.0)
    st_ref[...] = st_new

    @pl.when(c == n_chunks - 1)
    def _():
        for h in range(GLA_HEADS):
            sfin_ref[0, h] = st_new[h * GLA_DV:(h + 1) * GLA_DV, h * GLA_DK:(h + 1) * GLA_DK].T


def _gla_prompt(p2, n_seq, seq_len, e2, gn):
    C = GLA_CHUNK
    nch = seq_len // C
    rows = n_seq * seq_len

    def tok(width, colblk):
        return pl.BlockSpec((C, width), lambda b, c: (b * nch + c, colblk))

    return pl.pallas_call(
        functools.partial(_gla_prompt_kernel, n_chunks=nch),
        grid=(n_seq, nch),
        in_specs=[tok(GLA_QK, P_Q // GLA_QK), tok(GLA_QK, P_K // GLA_QK),
                  tok(GLA_WIDTH, P_V // GLA_WIDTH), tok(GLA_WIDTH, P_G // GLA_WIDTH),
                  tok(GLA_QK, P_LA // GLA_QK),
                  _const_spec(e2.shape), _const_spec(gn.shape)],
        out_specs=[pl.BlockSpec((C, GLA_WIDTH), lambda b, c: (b * nch + c, 0)),
                   pl.BlockSpec((1, GLA_HEADS, GLA_DK, GLA_DV), lambda b, c: (b, 0, 0, 0))],
        out_shape=(jax.ShapeDtypeStruct((rows, GLA_WIDTH), F32),
                   jax.ShapeDtypeStruct((n_seq, GLA_HEADS, GLA_DK, GLA_DV), F32)),
        scratch_shapes=[pltpu.VMEM((GLA_WIDTH, GLA_QK), F32),
                        pltpu.VMEM((GLA_BAND * C, GLA_QK), BF16)],
        compiler_params=_params(("parallel", "arbitrary")),
        name="gla_prompt",
    )(p2, p2, p2, p2, p2, e2, gn)


def _gla_sample_kernel(q_ref, k_ref, v_ref, g_ref, la_ref, s0_ref, e2_ref, rep_ref, gn_ref,
                       o_ref, snew_ref, *, seq_len):
    R = q_ref.shape[0]
    nseq = R // seq_len
    SD = nseq * GLA_DK
    la = la_ref[...]
    row = lax.broadcasted_iota(jnp.int32, (R, R), 0)
    col = lax.broadcasted_iota(jnp.int32, (R, R), 1)
    same = (col // seq_len) == (row // seq_len)
    b = _cumsum_rows(same & (col <= row), la)
    btot = _cumsum_rows(same, la)
    q = q_ref[...] * (GLA_DK ** -0.5)
    k = k_ref[...]
    v = v_ref[...]
    rmod = lax.broadcasted_iota(jnp.int32, (R, GLA_QK), 0) % seq_len

    o = jnp.zeros((R, GLA_WIDTH), F32)
    for d in range(seq_len):
        ks_, bs_, vs_ = (k, b, v) if d == 0 else (pltpu.roll(k, d, 0), pltpu.roll(b, d, 0),
                                                  pltpu.roll(v, d, 0))
        m = q * ks_ * jnp.exp(jnp.minimum(b - bs_, 0.0))
        m = jnp.where(rmod >= d, m, 0.0)
        o = o + _dot(m.astype(BF16), e2_ref[...]) * vs_

    xr = lax.broadcasted_iota(jnp.int32, (R, GLA_HEADS * SD), 0) // seq_len
    xc = (lax.broadcasted_iota(jnp.int32, (R, GLA_HEADS * SD), 1) % SD) // GLA_DK
    own = xr == xc
    rep = rep_ref[...]

    def expand(x_bf16):
        return jnp.where(own, _dot(x_bf16, rep), 0.0).astype(BF16)

    qx = expand((q * jnp.exp(b)).astype(BF16))
    kd_hi, kd_lo = _split(k * jnp.exp(btot - b))
    kx_hi, kx_lo = expand(kd_hi), expand(kd_lo)
    ea, eb, ec = _split3(jnp.exp(btot))
    ax = (expand(ea), expand(eb), expand(ec))
    last = (lax.broadcasted_iota(jnp.int32, (R, GLA_DV), 0) % seq_len == seq_len - 1).astype(BF16)

    outs = []
    for h in range(GLA_HEADS):
        xs = slice(h * SD, (h + 1) * SD)
        vs = slice(h * GLA_DV, (h + 1) * GLA_DV)
        s0 = s0_ref[:, h].reshape(SD, GLA_DV)
        o_h = o[:, vs] + _dot(qx[:, xs], s0.astype(BF16))
        outs.append(_rms(o_h))
        v_hi, v_lo = _split(v[:, vs])
        upd = _dot_tn(kx_hi[:, xs], v_hi) + _dot_tn(kx_lo[:, xs], v_hi) + _dot_tn(kx_hi[:, xs], v_lo)
        decay = _dot_tn(ax[0][:, xs], last) + _dot_tn(ax[1][:, xs], last) + _dot_tn(ax[2][:, xs], last)
        snew_ref[:, h] = (decay * s0 + upd).reshape(nseq, GLA_DK, GLA_DV)
    g = g_ref[...]
    o_ref[...] = jnp.concatenate(outs, axis=-1) * gn_ref[...] * (g * jax.nn.sigmoid(g))


def _gla_sample(p2, s0, seq_len, e2, rep, gn):
    n_seq = s0.shape[0]
    R = SAMPLE_SEQ_BLOCK * seq_len
    nblk = n_seq // SAMPLE_SEQ_BLOCK

    def tok(width, colblk):
        return pl.BlockSpec((R, width), lambda i: (i, colblk))

    st_spec = pl.BlockSpec((SAMPLE_SEQ_BLOCK, GLA_HEADS, GLA_DK, GLA_DV), lambda i: (i, 0, 0, 0))
    return pl.pallas_call(
        functools.partial(_gla_sample_kernel, seq_len=seq_len),
        grid=(nblk,),
        in_specs=[tok(GLA_QK, P_Q // GLA_QK), tok(GLA_QK, P_K // GLA_QK),
                  tok(GLA_WIDTH, P_V // GLA_WIDTH), tok(GLA_WIDTH, P_G // GLA_WIDTH),
                  tok(GLA_QK, P_LA // GLA_QK), st_spec,
                  _const_spec(e2.shape), _const_spec(rep.shape), _const_spec(gn.shape)],
        out_specs=[pl.BlockSpec((R, GLA_WIDTH), lambda i: (i, 0)), st_spec],
        out_shape=(jax.ShapeDtypeStruct((n_seq * seq_len, GLA_WIDTH), F32),
                   jax.ShapeDtypeStruct(s0.shape, F32)),
        compiler_params=_params(("parallel",)),
        name="gla_sample",
    )(p2, p2, p2, p2, p2, s0, e2, rep, gn)


def _s5_local_scan(bur, bui, pwre_ref, pwim_ref, lanes, group):
    rows, width = bur.shape
    xr = bur.reshape(rows // 8, 8, width)
    xi = bui.reshape(rows // 8, 8, width)
    sub = lax.broadcasted_iota(jnp.int32, (8, width), 0) % group
    s = 1
    while s < group:
        ar = jnp.where(sub >= s, pwre_ref[s - 1:s, lanes], 0.0)[None]
        ai = jnp.where(sub >= s, pwim_ref[s - 1:s, lanes], 0.0)[None]
        sr = pltpu.roll(xr, s, 1)
        si = pltpu.roll(xi, s, 1)
        xr, xi = xr + ar * sr - ai * si, xi + ar * si + ai * sr
        s *= 2
    return xr.reshape(rows, width), xi.reshape(rows, width)


def _s5_tail(ys, u, d_ref, wglu_ref, bglu_ref):
    y = jnp.concatenate(ys, axis=-1) + d_ref[...] * u
    z = jax.nn.gelu(y)
    return z * jax.nn.sigmoid(_dot(z.astype(BF16), wglu_ref[...]) + bglu_ref[...])


def _s5_prompt_kernel(u_ref, bre_ref, bim_ref, cre_ref, cim_ref,
                      pwre_ref, pwim_ref, d_ref, wglu_ref, bglu_ref,
                      o_ref, stre_ref, stim_ref, hre_s, him_s, car_re, car_im):
    t = pl.program_id(1)
    TT = u_ref.shape[0]

    @pl.when(t == 0)
    def _():
        car_re[...] = jnp.zeros_like(car_re)
        car_im[...] = jnp.zeros_like(car_im)

    u = u_ref[...]
    ys = []
    for l in range(S5_NBLK):
        lanes = slice(l * S5_BLK_ST, (l + 1) * S5_BLK_ST)
        ul = u[:, l * S5_BLK_CH:(l + 1) * S5_BLK_CH]
        ub = ul.astype(BF16)
        bur = _dot(ub, bre_ref[l])
        bui = _dot(ub, bim_ref[l])
        bur, bui = _s5_local_scan(bur, bui, pwre_ref, pwim_ref, lanes, 8)
        hre_s[...] = bur
        him_s[...] = bui
        p8r = pwre_ref[:, lanes]
        p8i = pwim_ref[:, lanes]

        def grp(r, carry):
            cr, ci = carry
            off = pl.multiple_of(r * 8, 8)
            xr = hre_s[pl.ds(off, 8), :] + p8r * cr - p8i * ci
            xi = him_s[pl.ds(off, 8), :] + p8r * ci + p8i * cr
            hre_s[pl.ds(off, 8), :] = xr
            him_s[pl.ds(off, 8), :] = xi
            return (jnp.broadcast_to(xr[7:8], xr.shape), jnp.broadcast_to(xi[7:8], xi.shape))

        cr0 = jnp.broadcast_to(car_re[:, lanes], (8, S5_BLK_ST))
        ci0 = jnp.broadcast_to(car_im[:, lanes], (8, S5_BLK_ST))
        cr, ci = lax.fori_loop(0, TT // 8, grp, (cr0, ci0))
        car_re[:, lanes] = cr[0:1]
        car_im[:, lanes] = ci[0:1]
        ys.append(_dot(hre_s[...].astype(BF16), cre_ref[l]) - _dot(him_s[...].astype(BF16), cim_ref[l]))
    o_ref[...] = _s5_tail(ys, u, d_ref, wglu_ref, bglu_ref)
    stre_ref[0] = car_re[...]
    stim_ref[0] = car_im[...]


def _s5_sample_kernel(u_ref, s0re_ref, s0im_ref, bre_ref, bim_ref, cre_ref, cim_ref,
                      pwre_ref, pwim_ref, d_ref, wglu_ref, bglu_ref,
                      o_ref, hre_ref, him_ref, *, seq_len):
    u = u_ref[...]
    ys = []
    for l in range(S5_NBLK):
        lanes = slice(l * S5_BLK_ST, (l + 1) * S5_BLK_ST)
        ul = u[:, l * S5_BLK_CH:(l + 1) * S5_BLK_CH]
        ar = pwre_ref[0:1, lanes]
        ai = pwim_ref[0:1, lanes]
        sr = s0re_ref[:, lanes]
        si = s0im_ref[:, lanes]
        ub = ul.astype(BF16)
        bur = _dot(ub, bre_ref[l]) + (ar * sr - ai * si)
        bui = _dot(ub, bim_ref[l]) + (ar * si + ai * sr)
        bur, bui = _s5_local_scan(bur, bui, pwre_ref, pwim_ref, lanes, seq_len)
        hre_ref[:, lanes] = bur
        him_ref[:, lanes] = bui
        ys.append(_dot(bur.astype(BF16), cre_ref[l]) - _dot(bui.astype(BF16), cim_ref[l]))
    o_ref[...] = _s5_tail(ys, u, d_ref, wglu_ref, bglu_ref)


def _s5_weight_specs(ws):
    return [_const_spec(w.shape) for w in ws]


def _s5_prompt(p2, n_seq, seq_len, ws):
    TT = PROJ_TILE
    nt = seq_len // TT
    rows = n_seq * seq_len
    st_spec = pl.BlockSpec((1, 1, S5_LANES), lambda b, t: (b, 0, 0))
    return pl.pallas_call(
        _s5_prompt_kernel,
        grid=(n_seq, nt),
        in_specs=[pl.BlockSpec((TT, S5_WIDTH), lambda b, t: (b * nt + t, P_U // S5_WIDTH))]
        + _s5_weight_specs(ws),
        out_specs=[pl.BlockSpec((TT, S5_WIDTH), lambda b, t: (b * nt + t, 0)), st_spec, st_spec],
        out_shape=(jax.ShapeDtypeStruct((rows, S5_WIDTH), F32),
                   jax.ShapeDtypeStruct((n_seq, 1, S5_LANES), F32),
                   jax.ShapeDtypeStruct((n_seq, 1, S5_LANES), F32)),
        scratch_shapes=[pltpu.VMEM((TT, S5_BLK_ST), F32), pltpu.VMEM((TT, S5_BLK_ST), F32),
                        pltpu.VMEM((1, S5_LANES), F32), pltpu.VMEM((1, S5_LANES), F32)],
        compiler_params=_params(("parallel", "arbitrary")),
        name="s5_prompt",
    )(p2, *ws)


def _s5_sample(p2, s0re_rows, s0im_rows, seq_len, ws):
    rows = s0re_rows.shape[0]
    TT = TOK_TILE
    row_spec = pl.BlockSpec((TT, S5_LANES), lambda i: (i, 0))
    return pl.pallas_call(
        functools.partial(_s5_sample_kernel, seq_len=seq_len),
        grid=(rows // TT,),
        in_specs=[pl.BlockSpec((TT, S5_WIDTH), lambda i: (i, P_U // S5_WIDTH)), row_spec, row_spec]
        + _s5_weight_specs(ws),
        out_specs=[pl.BlockSpec((TT, S5_WIDTH), lambda i: (i, 0)), row_spec, row_spec],
        out_shape=(jax.ShapeDtypeStruct((rows, S5_WIDTH), F32),
                   jax.ShapeDtypeStruct((rows, S5_LANES), F32),
                   jax.ShapeDtypeStruct((rows, S5_LANES), F32)),
        compiler_params=_params(("parallel",)),
        name="s5_sample",
    )(p2, s0re_rows, s0im_rows, *ws)


def _outproj_kernel(x_ref, og_ref, os_ref, wt_ref, wb_ref, x1_ref):
    x1 = (x_ref[...] + _dot(og_ref[...].astype(BF16), wt_ref[...])
          + _dot(os_ref[...].astype(BF16), wb_ref[...]))
    x1_ref[...] = x1


def _out_proj(x2, og, osx, wt, wb):
    rows = x2.shape[0]
    big = pl.BlockSpec((PROJ_TILE, D_MODEL), lambda i: (i, 0))
    half = pl.BlockSpec((PROJ_TILE, GLA_WIDTH), lambda i: (i, 0))
    return pl.pallas_call(
        _outproj_kernel,
        grid=(rows // PROJ_TILE,),
        in_specs=[big, half, half] + [_const_spec(w.shape) for w in (wt, wb)],
        out_specs=big,
        out_shape=jax.ShapeDtypeStruct((rows, D_MODEL), F32),
        compiler_params=_params(("parallel",)),
        name="out_proj",
    )(x2, og, osx, wt, wb)


def _first_and_nth_max(work, n):
    first = None
    m = None
    for r in range(n):
        m = jnp.max(work, axis=0, keepdims=True)
        if r == 0:
            first = m
        if r < n - 1:
            work = jnp.where(work == m, -jnp.inf, work)
    return first, m


def _sorted_columns(s):
    n = s.shape[0] // 8
    cols = [s[8 * v:8 * (v + 1)] for v in range(n)]
    k = 2
    while k <= n:
        j = k // 2
        while j >= 1:
            for i in range(n):
                l = i ^ j
                if l > i:
                    hi, lo = jnp.maximum(cols[i], cols[l]), jnp.minimum(cols[i], cols[l])
                    cols[i], cols[l] = (hi, lo) if (i & k) == 0 else (lo, hi)
            j //= 2
        k *= 2
    return cols


def _top_values(s, n, store):
    cols = _sorted_columns(s)
    for r in range(n):
        m = jnp.max(cols[0], axis=0, keepdims=True)
        store(r, m)
        if r < n - 1:
            hit = cols[0] == m
            for v in range(n - 1 - r):
                nxt = cols[v + 1] if v + 1 < len(cols) else jnp.full_like(cols[v], -jnp.inf)
                cols[v] = jnp.where(hit, nxt, cols[v])


def _route_kernel(x1_ref, n2_ref, wq_ref, kh_ref, kl_ref,
                  n1_ref, e1_ref, rho_ref, e2_ref, s_s, top_s):
    TM = x1_ref.shape[0]
    h2 = _rms(x1_ref[...]) * n2_ref[...]
    qp = _dot(h2.astype(BF16), wq_ref[...])
    for hp in range(2 * PEER_HEADS):
        q_hi, q_lo = _split(qp[:, hp * PEER_KEYS:(hp + 1) * PEER_KEYS])
        s_s[hp] = _dot_nt(kh_ref[hp], q_hi) + _dot_nt(kl_ref[hp], q_hi) + _dot_nt(kh_ref[hp], q_lo)
    K = PEER_TOPK
    for lc in range(TM // LANES):
        sl = slice(lc * LANES, (lc + 1) * LANES)
        for hp in range(2 * PEER_HEADS):
            def store(r, m, hp=hp):
                top_s[hp, r:r + 1, sl] = m

            _top_values(s_s[hp, :, sl], K, store)
        for h in range(PEER_HEADS):
            v1 = top_s[2 * h, :, sl]
            v2 = top_s[2 * h + 1, :, sl]
            cands = [v1 + v2[0:1]] + [v1[0:8] + v2[b:b + 1] for b in range(1, 8)] + [v2[8:K] + v1[0:1]]
            cand = jnp.concatenate(cands, axis=0)
            mx, thr = _first_and_nth_max(cand, K)
            z = jnp.sum(jnp.where(cand >= thr, jnp.exp(cand - mx), 0.0), axis=0, keepdims=True)
            cnt = jnp.zeros(v1.shape, F32)
            for b in range(K):
                cnt = cnt + jnp.where(v1 + v2[b:b + 1] >= thr, 1.0, 0.0)
            s1 = s_s[2 * h, :, sl]
            s2 = s_s[2 * h + 1, :, sl]
            n1 = jnp.zeros(s1.shape, F32)
            rho = jnp.full(s2.shape, float(K), F32)
            for a in range(K):
                n1 = jnp.where(s1 == v1[a:a + 1], cnt[a:a + 1], n1)
                rho = jnp.where(s2 == v2[a:a + 1], float(a), rho)
            n1_ref[h, :, sl] = n1
            e1_ref[h, :, sl] = jnp.exp(s1 - v1[0:1])
            rho_ref[h, :, sl] = rho.astype(BF16)
            e2_ref[h, :, sl] = (jnp.exp(s2 - v2[0:1]) * (1.0 / z)).astype(BF16)


def _route(x1, n2, wq_bf, kh, kl):
    rows = x1.shape[0]
    TM = TOK_TILE
    key_spec = pl.BlockSpec((PEER_HEADS, PEER_KEYS, TM), lambda i: (0, 0, i))
    return pl.pallas_call(
        _route_kernel,
        grid=(rows // TM,),
        in_specs=[pl.BlockSpec((TM, D_MODEL), lambda i: (i, 0)),
                  _const_spec(n2.shape), _const_spec(wq_bf.shape),
                  _const_spec(kh.shape), _const_spec(kl.shape)],
        out_specs=[key_spec] * 4,
        out_shape=(jax.ShapeDtypeStruct((PEER_HEADS, PEER_KEYS, rows), F32),
                   jax.ShapeDtypeStruct((PEER_HEADS, PEER_KEYS, rows), F32),
                   jax.ShapeDtypeStruct((PEER_HEADS, PEER_KEYS, rows), BF16),
                   jax.ShapeDtypeStruct((PEER_HEADS, PEER_KEYS, rows), BF16)),
        scratch_shapes=[pltpu.VMEM((2 * PEER_HEADS, PEER_KEYS, TM), F32),
                        pltpu.VMEM((2 * PEER_HEADS, PEER_TOPK, TM), F32)],
        compiler_params=_params(("parallel",)),
        name="peer_route",
    )(x1, n2, wq_bf, kh, kl)


PEER_CHUNK = 256


def _peer_gate_piece(i1, c, a, act_ref, wact_ref, n1_ref, e1_ref, rho_ref, e2_ref):
    ls = slice(c * PEER_CHUNK, (c + 1) * PEER_CHUNK)
    rs = slice(a * PEER_KEYS, (a + 1) * PEER_KEYS)
    w = jnp.zeros((PEER_KEYS, PEER_CHUNK), BF16)
    for h in range(PEER_HEADS):
        n1 = n1_ref[h, pl.ds(i1, 1), ls].astype(BF16)
        g1 = e1_ref[h, pl.ds(i1, 1), ls].astype(BF16)
        e2 = e2_ref[h, :, ls]
        w = w + jnp.where(rho_ref[h, :, ls] < n1, e2, jnp.zeros_like(e2)) * g1
    x = act_ref[rs, ls]
    t = jnp.exp((x * (-2.0 * GELU_C - (2.0 * GELU_C * GELU_A) * (x * x))).astype(BF16))
    wact_ref[rs, ls] = w * (x.astype(BF16) / (1.0 + t))


def _peer_half(blk_gate, u_half, vt_half, act_in, wact_out, wact_in, act_out,
               acc, h2b, n1_ref, e1_ref, rho_ref, e2_ref, stages):
    sub = act_in.shape[0] // PEER_KEYS
    for c in range(act_in.shape[1] // PEER_CHUNK):
        ls = slice(c * PEER_CHUNK, (c + 1) * PEER_CHUNK)
        if "C" in stages:
            acc[:, ls] += _dot(vt_half, wact_in[:, ls])
        if "B" in stages:
            for a in range(sub):
                _peer_gate_piece(blk_gate * sub + a, c, a, act_in, wact_out, n1_ref, e1_ref, rho_ref, e2_ref)
        if "A" in stages:
            act_out[:, ls] = _dot_nt(u_half, h2b[c * PEER_CHUNK:(c + 1) * PEER_CHUNK, :])


def _peer_kernel(x1_ref, n2_ref, n1_ref, e1_ref, rho_ref, e2_ref, u_ref, vt_ref, fn_ref, y_ref,
                 acc, h2b, wact0, wact1, act0, act1, *, n_steps):
    g = pl.program_id(1)
    EB = act0.shape[0]
    refs = (acc, h2b, n1_ref, e1_ref, rho_ref, e2_ref)

    def halves(first, second):
        _peer_half(2 * g - 1, u_ref[0:EB, :], vt_ref[0, :, 0:EB], act1, wact1, wact0, act0, *refs,
                   stages=first)
        _peer_half(2 * g, u_ref[EB:2 * EB, :], vt_ref[0, :, EB:2 * EB], act0, wact0, wact1, act1, *refs,
                   stages=second)

    @pl.when(g == 0)
    def _():
        acc[...] = jnp.zeros_like(acc)
        h2b[...] = (_rms(x1_ref[...]) * n2_ref[...]).astype(BF16)
        halves("A", "AB")

    @pl.when((g > 0) & (g < n_steps - 1))
    def _():
        halves("ABC", "ABC")

    @pl.when(g == n_steps - 1)
    def _():
        halves("BC", "C")
        out = x1_ref[...] + acc[...].T
        y_ref[...] = _rms(out) * fn_ref[...]


def _peer(x1, n2, n1, e1, rho, e2, u_bf, vt_bf, fn):
    rows = x1.shape[0]
    TM = PEER_TOK_TILE
    EB = PEER_EXP_TILE
    n_steps = PEER_EXPERTS // (2 * EB) + 1
    big = pl.BlockSpec((TM, D_MODEL), lambda i, g: (i, 0))
    key_spec = pl.BlockSpec((PEER_HEADS, PEER_KEYS, TM), lambda i, g: (0, 0, i))
    return pl.pallas_call(
        functools.partial(_peer_kernel, n_steps=n_steps),
        grid=(rows // TM, n_steps),
        in_specs=[big, _const_spec(n2.shape), key_spec, key_spec, key_spec, key_spec,
                  pl.BlockSpec((2 * EB, D_MODEL), lambda i, g: (jnp.minimum(g, n_steps - 2), 0)),
                  pl.BlockSpec((1, D_MODEL, 2 * EB), lambda i, g: (jnp.maximum(g - 1, 0), 0, 0)),
                  _const_spec(fn.shape)],
        out_specs=big,
        out_shape=jax.ShapeDtypeStruct((rows, D_MODEL), F32),
        scratch_shapes=[pltpu.VMEM((D_MODEL, TM), F32), pltpu.VMEM((TM, D_MODEL), BF16),
                        pltpu.VMEM((EB, TM), BF16), pltpu.VMEM((EB, TM), BF16),
                        pltpu.VMEM((EB, TM), F32), pltpu.VMEM((EB, TM), F32)],
        compiler_params=_params(("parallel", "arbitrary")),
        name="peer_experts",
    )(x1, n2, n1, e1, rho, e2, u_bf, vt_bf, fn)


def _block_diag(w, eye):
    n, g, a, b = w.shape
    return jnp.einsum('lgab,gh->lgahb', w, eye).reshape(n, g * a, g * b)


def _gla_constants():
    lane_head = jnp.arange(GLA_QK) // GLA_DK
    e2 =(lane_head[:, None] == (jnp.arange(GLA_WIDTH) // GLA_DV)[None, :]).astype(BF16)
    sd = SAMPLE_SEQ_BLOCK * GLA_DK
    xc = jnp.arange(GLA_HEADS * sd)
    rep = ((lane_head[:, None] == (xc // sd)[None, :])
           & ((jnp.arange(GLA_QK) % GLA_DK)[:, None] == (xc % GLA_DK)[None, :])).astype(BF16)
    return e2, rep


def kernel(x_prompt, x_sample, state_gla, state_s5_re, state_s5_im, norm1, w_in, w_a2, b_a2, gla_norm, s5_lam_re, s5_lam_im, s5_log_dt, s5_b_re, s5_b_im, s5_c_re, s5_c_im, s5_d, w_glu, b_glu, w_out, norm2, peer_wq, peer_keys, peer_u, peer_v, final_norm):
    depth = norm1.shape[0]
    assert depth == 1, "single-layer trunk"
    n_p, len_p, _ = x_prompt.shape
    n_s, len_s, _ = x_sample.shape
    l = 0

    w = w_in[l]
    w_re = jnp.concatenate([w[:, 0:1536], w[:, 1552:2064], w[:, 1536:1552],
                            jnp.zeros((D_MODEL, W_IN_COLS - 2064), F32)], axis=1)
    w_bf = w_re.astype(BF16)
    a2 = jnp.concatenate([w_a2[l], jnp.zeros((W_IN_COLS - P_LA - GLA_RANK, GLA_QK), F32)], axis=0)
    a2hi, a2lo = _split(a2)
    ba2 = b_a2[l].reshape(1, GLA_QK)
    n1 = norm1[l].reshape(1, D_MODEL)
    gn = gla_norm[l].reshape(1, GLA_WIDTH)
    e2, rep = _gla_constants()

    pwre, pwim, bbre, bbim = _s5_prep(s5_lam_re[l], s5_lam_im[l], s5_log_dt[l], s5_b_re[l], s5_b_im[l])
    eye = jnp.eye(8, dtype=F32)
    blk = lambda t: t.reshape(S5_NBLK, 8, S5_GROUP_CH, S5_STATE)
    bre = _block_diag(blk(bbre), eye).astype(BF16)
    bim = _block_diag(blk(bbim), eye).astype(BF16)
    cre = _block_diag(jnp.swapaxes(blk(s5_c_re[l]), 2, 3), eye).astype(BF16)
    cim = _block_diag(jnp.swapaxes(blk(s5_c_im[l]), 2, 3), eye).astype(BF16)
    s5w = (bre, bim, cre, cim, pwre.reshape(8, S5_LANES), pwim.reshape(8, S5_LANES),
           s5_d[l].reshape(1, S5_WIDTH), w_glu[l].astype(BF16), b_glu[l].reshape(1, S5_WIDTH))

    wt = w_out[l][:GLA_WIDTH].astype(BF16)
    wb = w_out[l][GLA_WIDTH:].astype(BF16)
    n2 = norm2[l].reshape(1, D_MODEL)
    wq_bf = peer_wq[l].astype(BF16)
    kh, kl = _split(peer_keys[l].reshape(2 * PEER_HEADS, PEER_KEYS, PEER_DQ // 2))
    u_bf = peer_u[l].astype(BF16)
    vt_bf = jnp.swapaxes(peer_v[l].astype(BF16).reshape(-1, 2 * PEER_EXP_TILE, D_MODEL), 1, 2)
    fn = final_norm.reshape(1, D_MODEL)

    def tail(x2, og, osx):
        x1 = _out_proj(x2, og, osx, wt, wb)
        n1, e1, rho, e2g = _route(x1, n2, wq_bf, kh, kl)
        return _peer(x1, n2, n1, e1, rho, e2g, u_bf, vt_bf, fn)

    xp = x_prompt.reshape(n_p * len_p, D_MODEL)
    pp = _in_proj(xp, n1, w_bf, a2hi, a2lo, ba2)
    og_p, gla_p = _gla_prompt(pp, n_p, len_p, e2, gn)
    os_p, sre_p, sim_p = _s5_prompt(pp, n_p, len_p, s5w)
    y_p = tail(xp, og_p, os_p).reshape(n_p, len_p, D_MODEL)

    xs = x_sample.reshape(n_s * len_s, D_MODEL)
    ps = _in_proj(xs, n1, w_bf, a2hi, a2lo, ba2)
    og_s, gla_s = _gla_sample(ps, state_gla[l], len_s, e2, rep, gn)
    first_row = lambda s: jnp.pad(s.reshape(n_s, 1, S5_LANES), ((0, 0), (0, len_s - 1), (0, 0))
                                  ).reshape(n_s * len_s, S5_LANES)
    os_s, hre_s, him_s = _s5_sample(ps, first_row(state_s5_re[l]), first_row(state_s5_im[l]), len_s, s5w)
    y_s = tail(xs, og_s, os_s).reshape(n_s, len_s, D_MODEL)
    last_row = lambda hs: hs.reshape(n_s, len_s, S5_GROUPS, S5_STATE)[:, len_s - 1]

    st = lambda a: a.reshape(1, n_p, S5_GROUPS, S5_STATE)
    return (y_p, y_s, gla_p[None], st(sre_p), st(sim_p),
            gla_s[None], last_row(hre_s)[None], last_row(him_s)[None])
```

```python
import functools

import jax
import jax.numpy as jnp
from jax import lax
from jax.experimental import pallas as pl
from jax.experimental.pallas import tpu as pltpu

F32 = jnp.float32
BF16 = jnp.bfloat16

D_MODEL = 1024
GLA_HEADS = 4
GLA_DK = 64
GLA_DV = 128
GLA_QK = GLA_HEADS * GLA_DK
GLA_WIDTH = GLA_HEADS * GLA_DV
GLA_RANK = 16
GLA_TAU = 16.0
GLA_CHUNK = 128
S5_WIDTH = 512
S5_GROUP_CH = 16
S5_GROUPS = 32
S5_STATE = 64
S5_LANES = S5_GROUPS * S5_STATE
S5_NBLK = 4
S5_BLK_CH = S5_WIDTH // S5_NBLK
S5_BLK_ST = S5_LANES // S5_NBLK
PEER_KEYS = 128
PEER_EXPERTS = PEER_KEYS * PEER_KEYS
PEER_HEADS = 8
PEER_DQ = 256
PEER_TOPK = 16
EPS = 1e-6
LANES = 128
GELU_C = 0.7978845608028654
GELU_A = 0.044715

P_Q, P_K, P_V, P_G, P_U, P_LA = 0, 256, 512, 1024, 1536, 2048
P_COLS = 2304
W_IN_COLS = 2176

TOK_TILE = 256
PROJ_TILE = 512
PEER_TOK_TILE = 512
PEER_EXP_TILE = 512
SAMPLE_SEQ_BLOCK = 16
VMEM_LIMIT = 56 * 1024 * 1024


def _split(x):
    hi = x.astype(BF16)
    lo = (x - hi.astype(F32)).astype(BF16)
    return hi, lo


def _split3(x):
    a = x.astype(BF16)
    r = x - a.astype(F32)
    b = r.astype(BF16)
    c = (r - b.astype(F32)).astype(BF16)
    return a, b, c


def _dot(a, b):
    return jnp.dot(a, b, preferred_element_type=F32)


def _dot_nt(a, b):
    return lax.dot_general(a, b, (((1,), (1,)), ((), ())), preferred_element_type=F32)


def _dot_tn(a, b):
    return lax.dot_general(a, b, (((0,), (0,)), ((), ())), preferred_element_type=F32)


def _dot3(a, b_hi, b_lo):
    a_hi, a_lo = _split(a)
    return _dot(a_hi, b_hi) + _dot(a_lo, b_hi) + _dot(a_hi, b_lo)


def _dot_exact01(m01, x):
    a, b, c = _split3(x)
    return _dot(m01, a) + _dot(m01, b) + _dot(m01, c)


def _rms(x):
    return x * lax.rsqrt(jnp.mean(x * x, axis=-1, keepdims=True) + EPS)


def _params(sem):
    return pltpu.CompilerParams(dimension_semantics=sem, vmem_limit_bytes=VMEM_LIMIT)


def _const_spec(shape):
    n = len(shape)
    return pl.BlockSpec(shape, lambda *_: (0,) * n)


def _s5prep_kernel(lr_ref, li_ref, ldt_ref, bret_ref, bimt_ref,
                   pwre_ref, pwim_ref, bbre_ref, bbim_ref):
    lr = lr_ref[...]
    li = li_ref[...]
    dt = jnp.exp(ldt_ref[...])
    mag = jnp.exp(lr * dt)
    abr = mag * jnp.cos(li * dt)
    abi = mag * jnp.sin(li * dt)
    den = lr * lr + li * li
    nr = abr - 1.0
    ni = abi
    fr = (nr * lr + ni * li) / den
    fi = (ni * lr - nr * li) / den
    bret = bret_ref[...]
    bimt = bimt_ref[...]
    bbre_ref[...] = fr[:, None, :] * bret - fi[:, None, :] * bimt
    bbim_ref[...] = fr[:, None, :] * bimt + fi[:, None, :] * bret
    pr, pi = abr, abi
    for i in range(8):
        pwre_ref[i] = pr
        pwim_ref[i] = pi
        pr, pi = pr * abr - pi * abi, pr * abi + pi * abr


def _s5_prep(lam_re, lam_im, log_dt, b_re, b_im):
    g, p = lam_re.shape
    ch = b_re.shape[-1]
    bret = jnp.transpose(b_re, (0, 2, 1))
    bimt = jnp.transpose(b_im, (0, 2, 1))
    out = pl.pallas_call(
        _s5prep_kernel,
        out_shape=(jax.ShapeDtypeStruct((8, g, p), F32), jax.ShapeDtypeStruct((8, g, p), F32),
                   jax.ShapeDtypeStruct((g, ch, p), F32), jax.ShapeDtypeStruct((g, ch, p), F32)),
        name="s5_prep",
    )(lam_re, lam_im, log_dt.reshape(g, 1), bret, bimt)
    return out


def _inproj_kernel(x_ref, n1_ref, w_ref, a2hi_ref, a2lo_ref, ba2_ref, p_ref):
    h = _rms(x_ref[...]) * n1_ref[...]
    p = _dot(h.astype(BF16), w_ref[...])
    alr = p[:, P_LA:W_IN_COLS]
    z = _dot3(alr, a2hi_ref[...], a2lo_ref[...]) + ba2_ref[...]
    log_sig = jnp.minimum(z, 0.0) - jnp.log1p(jnp.exp(-jnp.abs(z)))
    p_ref[:, 0:P_LA] = p[:, 0:P_LA]
    p_ref[:, P_LA:P_COLS] = log_sig * (1.0 / GLA_TAU)


def _in_proj(x2, n1, w_bf, a2hi, a2lo, ba2):
    rows = x2.shape[0]
    return pl.pallas_call(
        _inproj_kernel,
        grid=(rows // PROJ_TILE,),
        in_specs=[pl.BlockSpec((PROJ_TILE, D_MODEL), lambda i: (i, 0)),
                  _const_spec(n1.shape), _const_spec(w_bf.shape),
                  _const_spec(a2hi.shape), _const_spec(a2lo.shape), _const_spec(ba2.shape)],
        out_specs=pl.BlockSpec((PROJ_TILE, P_COLS), lambda i: (i, 0)),
        out_shape=jax.ShapeDtypeStruct((rows, P_COLS), F32),
        compiler_params=_params(("parallel",)),
        name="in_proj",
    )(x2, n1, w_bf, a2hi, a2lo, ba2)


def _cumsum_rows(mask01, la):
    return _dot_exact01(mask01.astype(BF16), la)


GLA_BAND = 4
SUBLANES = 8


def _gla_prompt_kernel(q_ref, k_ref, v_ref, g_ref, la_ref, e2_ref, gn_ref,
                       o_ref, sfin_ref, st_ref, slab, *, n_chunks):
    c = pl.program_id(1)
    C = q_ref.shape[0]
    HC = GLA_HEADS * C

    @pl.when(c == 0)
    def _():
        st_ref[...] = jnp.zeros_like(st_ref)

    la = la_ref[...]
    row = lax.broadcasted_iota(jnp.int32, (C, C), 0)
    col = lax.broadcasted_iota(jnp.int32, (C, C), 1)
    b = _cumsum_rows(col <= row, la)
    q = q_ref[...] * (GLA_DK ** -0.5)
    k = k_ref[...]
    v = v_ref[...]
    blast = b[C - 1:C, :]

    rloc = lax.broadcasted_iota(jnp.int32, (C, GLA_QK), 0) % GLA_BAND
    tiles = lambda t: t.reshape(C // SUBLANES, SUBLANES, t.shape[-1])
    k3, b3, v3 = tiles(k), tiles(b), tiles(v)
    vsh = [v]
    for d in range(GLA_BAND):
        if d == 0:
            ks_, bs_ = k, b
        else:
            ks_ = pltpu.roll(k3, d, 1).reshape(C, GLA_QK)
            bs_ = pltpu.roll(b3, d, 1).reshape(C, GLA_QK)
            vsh.append(pltpu.roll(v3, d, 1).reshape(C, GLA_WIDTH))
        m = q * ks_ * jnp.exp(jnp.minimum(b - bs_, 0.0))
        slab[d * C:(d + 1) * C, :] = jnp.where(rloc >= d, m, 0.0).astype(BF16)
    rep = _dot(slab[...], e2_ref[...])
    o = rep[0:C] * vsh[0]
    for d in range(1, GLA_BAND):
        o = o + rep[d * C:(d + 1) * C] * vsh[d]

    lane_head = lax.broadcasted_iota(jnp.int32, (HC, GLA_QK), 1) // GLA_DK
    row_head = lax.broadcasted_iota(jnp.int32, (HC, GLA_QK), 0) // C
    own_head = lane_head == row_head
    si = lax.broadcasted_iota(jnp.int32, (HC, C), 0) % C
    sj = lax.broadcasted_iota(jnp.int32, (HC, C), 1)
    scores = jnp.zeros((HC, C), F32)
    s = C // 2
    while s >= GLA_BAND:
        ref = jnp.concatenate([jnp.broadcast_to(b[p * 2 * s + s - 1:p * 2 * s + s, :], (2 * s, GLA_QK))
                               for p in range(C // (2 * s))], axis=0)
        ql = q * jnp.exp(jnp.minimum(b - ref, 0.0))
        kl = (k * jnp.exp(jnp.minimum(ref - b, 0.0))).astype(BF16)
        qs = jnp.where(own_head, jnp.concatenate([ql] * GLA_HEADS, axis=0), 0.0).astype(BF16)
        lvl = ((si // (2 * s)) == (sj // (2 * s))) & ((si // s) % 2 == 1) & ((sj // s) % 2 == 0)
        scores = scores + jnp.where(lvl, _dot_nt(qs, kl), 0.0)
        s //= 2
    scores = scores.astype(BF16)
    vb = v.astype(BF16)

    st = st_ref[...]
    o = o + _dot_nt((q * jnp.exp(b)).astype(BF16), st.astype(BF16))
    outs = []
    for h in range(GLA_HEADS):
        vs = slice(h * GLA_DV, (h + 1) * GLA_DV)
        outs.append(_rms(o[:, vs] + _dot(scores[h * C:(h + 1) * C], vb[:, vs])))
    g = g_ref[...]
    o_ref[...] = jnp.concatenate(outs, axis=-1) * gn_ref[...] * (g * jax.nn.sigmoid(g))

    kd_hi, kd_lo = _split(k * jnp.exp(blast - b))
    vt_hi, vt_lo = _split(v.T)
    upd = _dot(vt_hi, kd_hi) + _dot(vt_lo, kd_hi) + _dot(vt_hi, kd_lo)
    blk = (lax.broadcasted_iota(jnp.int32, st.shape, 0) // GLA_DV
           == lax.broadcasted_iota(jnp.int32, st.shape, 1) // GLA_DK)
    st_new = jnp.exp(blast) * st + jnp.where(blk, upd, 0.0)
    st_ref[...] = st_new

    @pl.when(c == n_chunks - 1)
    def _():
        for h in range(GLA_HEADS):
            sfin_ref[0, h] = st_new[h * GLA_DV:(h + 1) * GLA_DV, h * GLA_DK:(h + 1) * GLA_DK].T


def _gla_prompt(p2, n_seq, seq_len, e2, gn):
    C = GLA_CHUNK
    nch = seq_len // C
    rows = n_seq * seq_len

    def tok(width, colblk):
        return pl.BlockSpec((C, width), lambda b, c: (b * nch + c, colblk))

    return pl.pallas_call(
        functools.partial(_gla_prompt_kernel, n_chunks=nch),
        grid=(n_seq, nch),
        in_specs=[tok(GLA_QK, P_Q // GLA_QK), tok(GLA_QK, P_K // GLA_QK),
                  tok(GLA_WIDTH, P_V // GLA_WIDTH), tok(GLA_WIDTH, P_G // GLA_WIDTH),
                  tok(GLA_QK, P_LA // GLA_QK),
                  _const_spec(e2.shape), _const_spec(gn.shape)],
        out_specs=[pl.BlockSpec((C, GLA_WIDTH), lambda b, c: (b * nch + c, 0)),
                   pl.BlockSpec((1, GLA_HEADS, GLA_DK, GLA_DV), lambda b, c: (b, 0, 0, 0))],
        out_shape=(jax.ShapeDtypeStruct((rows, GLA_WIDTH), F32),
                   jax.ShapeDtypeStruct((n_seq, GLA_HEADS, GLA_DK, GLA_DV), F32)),
        scratch_shapes=[pltpu.VMEM((GLA_WIDTH, GLA_QK), F32),
                        pltpu.VMEM((GLA_BAND * C, GLA_QK), BF16)],
        compiler_params=_params(("parallel", "arbitrary")),
        name="gla_prompt",
    )(p2, p2, p2, p2, p2, e2, gn)


def _gla_sample_kernel(q_ref, k_ref, v_ref, g_ref, la_ref, s0_ref, e2_ref, rep_ref, gn_ref,
                       o_ref, snew_ref, *, seq_len):
    R = q_ref.shape[0]
    nseq = R // seq_len
    SD = nseq * GLA_DK
    la = la_ref[...]
    row = lax.broadcasted_iota(jnp.int32, (R, R), 0)
    col = lax.broadcasted_iota(jnp.int32, (R, R), 1)
    same = (col // seq_len) == (row // seq_len)
    b = _cumsum_rows(same & (col <= row), la)
    btot = _cumsum_rows(same, la)
    q = q_ref[...] * (GLA_DK ** -0.5)
    k = k_ref[...]
    v = v_ref[...]
    rmod = lax.broadcasted_iota(jnp.int32, (R, GLA_QK), 0) % seq_len

    o = jnp.zeros((R, GLA_WIDTH), F32)
    for d in range(seq_len):
        ks_, bs_, vs_ = (k, b, v) if d == 0 else (pltpu.roll(k, d, 0), pltpu.roll(b, d, 0),
                                                  pltpu.roll(v, d, 0))
        m = q * ks_ * jnp.exp(jnp.minimum(b - bs_, 0.0))
        m = jnp.where(rmod >= d, m, 0.0)
        o = o + _dot(m.astype(BF16), e2_ref[...]) * vs_

    xr = lax.broadcasted_iota(jnp.int32, (R, GLA_HEADS * SD), 0) // seq_len
    xc = (lax.broadcasted_iota(jnp.int32, (R, GLA_HEADS * SD), 1) % SD) // GLA_DK
    own = xr == xc
    rep = rep_ref[...]

    def expand(x_bf16):
        return jnp.where(own, _dot(x_bf16, rep), 0.0).astype(BF16)

    qx = expand((q * jnp.exp(b)).astype(BF16))
    kd_hi, kd_lo = _split(k * jnp.exp(btot - b))
    kx_hi, kx_lo = expand(kd_hi), expand(kd_lo)
    ea, eb, ec = _split3(jnp.exp(btot))
    ax = (expand(ea), expand(eb), expand(ec))
    last = (lax.broadcasted_iota(jnp.int32, (R, GLA_DV), 0) % seq_len == seq_len - 1).astype(BF16)

    outs = []
    for h in range(GLA_HEADS):
        xs = slice(h * SD, (h + 1) * SD)
        vs = slice(h * GLA_DV, (h + 1) * GLA_DV)
        s0 = s0_ref[:, h].reshape(SD, GLA_DV)
        o_h = o[:, vs] + _dot(qx[:, xs], s0.astype(BF16))
        outs.append(_rms(o_h))
        v_hi, v_lo = _split(v[:, vs])
        upd = _dot_tn(kx_hi[:, xs], v_hi) + _dot_tn(kx_lo[:, xs], v_hi) + _dot_tn(kx_hi[:, xs], v_lo)
        decay = _dot_tn(ax[0][:, xs], last) + _dot_tn(ax[1][:, xs], last) + _dot_tn(ax[2][:, xs], last)
        snew_ref[:, h] = (decay * s0 + upd).reshape(nseq, GLA_DK, GLA_DV)
    g = g_ref[...]
    o_ref[...] = jnp.concatenate(outs, axis=-1) * gn_ref[...] * (g * jax.nn.sigmoid(g))


def _gla_sample(p2, s0, seq_len, e2, rep, gn):
    n_seq = s0.shape[0]
    R = SAMPLE_SEQ_BLOCK * seq_len
    nblk = n_seq // SAMPLE_SEQ_BLOCK

    def tok(width, colblk):
        return pl.BlockSpec((R, width), lambda i: (i, colblk))

    st_spec = pl.BlockSpec((SAMPLE_SEQ_BLOCK, GLA_HEADS, GLA_DK, GLA_DV), lambda i: (i, 0, 0, 0))
    return pl.pallas_call(
        functools.partial(_gla_sample_kernel, seq_len=seq_len),
        grid=(nblk,),
        in_specs=[tok(GLA_QK, P_Q // GLA_QK), tok(GLA_QK, P_K // GLA_QK),
                  tok(GLA_WIDTH, P_V // GLA_WIDTH), tok(GLA_WIDTH, P_G // GLA_WIDTH),
                  tok(GLA_QK, P_LA // GLA_QK), st_spec,
                  _const_spec(e2.shape), _const_spec(rep.shape), _const_spec(gn.shape)],
        out_specs=[pl.BlockSpec((R, GLA_WIDTH), lambda i: (i, 0)), st_spec],
        out_shape=(jax.ShapeDtypeStruct((n_seq * seq_len, GLA_WIDTH), F32),
                   jax.ShapeDtypeStruct(s0.shape, F32)),
        compiler_params=_params(("parallel",)),
        name="gla_sample",
    )(p2, p2, p2, p2, p2, s0, e2, rep, gn)


def _s5_local_scan(bur, bui, pwre_ref, pwim_ref, lanes, group):
    rows, width = bur.shape
    xr = bur.reshape(rows // 8, 8, width)
    xi = bui.reshape(rows // 8, 8, width)
    sub = lax.broadcasted_iota(jnp.int32, (8, width), 0) % group
    s = 1
    while s < group:
        ar = jnp.where(sub >= s, pwre_ref[s - 1:s, lanes], 0.0)[None]
        ai = jnp.where(sub >= s, pwim_ref[s - 1:s, lanes], 0.0)[None]
        sr = pltpu.roll(xr, s, 1)
        si = pltpu.roll(xi, s, 1)
        xr, xi = xr + ar * sr - ai * si, xi + ar * si + ai * sr
        s *= 2
    return xr.reshape(rows, width), xi.reshape(rows, width)


def _s5_tail(ys, u, d_ref, wglu_ref, bglu_ref):
    y = jnp.concatenate(ys, axis=-1) + d_ref[...] * u
    z = jax.nn.gelu(y)
    return z * jax.nn.sigmoid(_dot(z.astype(BF16), wglu_ref[...]) + bglu_ref[...])


def _s5_prompt_kernel(u_ref, bre_ref, bim_ref, cre_ref, cim_ref,
                      pwre_ref, pwim_ref, d_ref, wglu_ref, bglu_ref,
                      o_ref, stre_ref, stim_ref, hre_s, him_s, car_re, car_im):
    t = pl.program_id(1)
    TT = u_ref.shape[0]

    @pl.when(t == 0)
    def _():
        car_re[...] = jnp.zeros_like(car_re)
        car_im[...] = jnp.zeros_like(car_im)

    u = u_ref[...]
    ys = []
    for l in range(S5_NBLK):
        lanes = slice(l * S5_BLK_ST, (l + 1) * S5_BLK_ST)
        ul = u[:, l * S5_BLK_CH:(l + 1) * S5_BLK_CH]
        ub = ul.astype(BF16)
        bur = _dot(ub, bre_ref[l])
        bui = _dot(ub, bim_ref[l])
        bur, bui = _s5_local_scan(bur, bui, pwre_ref, pwim_ref, lanes, 8)
        hre_s[...] = bur
        him_s[...] = bui
        p8r = pwre_ref[:, lanes]
        p8i = pwim_ref[:, lanes]

        def grp(r, carry):
            cr, ci = carry
            off = pl.multiple_of(r * 8, 8)
            xr = hre_s[pl.ds(off, 8), :] + p8r * cr - p8i * ci
            xi = him_s[pl.ds(off, 8), :] + p8r * ci + p8i * cr
            hre_s[pl.ds(off, 8), :] = xr
            him_s[pl.ds(off, 8), :] = xi
            return (jnp.broadcast_to(xr[7:8], xr.shape), jnp.broadcast_to(xi[7:8], xi.shape))

        cr0 = jnp.broadcast_to(car_re[:, lanes], (8, S5_BLK_ST))
        ci0 = jnp.broadcast_to(car_im[:, lanes], (8, S5_BLK_ST))
        cr, ci = lax.fori_loop(0, TT // 8, grp, (cr0, ci0))
        car_re[:, lanes] = cr[0:1]
        car_im[:, lanes] = ci[0:1]
        ys.append(_dot(hre_s[...].astype(BF16), cre_ref[l]) - _dot(him_s[...].astype(BF16), cim_ref[l]))
    o_ref[...] = _s5_tail(ys, u, d_ref, wglu_ref, bglu_ref)
    stre_ref[0] = car_re[...]
    stim_ref[0] = car_im[...]


def _s5_sample_kernel(u_ref, s0re_ref, s0im_ref, bre_ref, bim_ref, cre_ref, cim_ref,
                      pwre_ref, pwim_ref, d_ref, wglu_ref, bglu_ref,
                      o_ref, hre_ref, him_ref, *, seq_len):
    u = u_ref[...]
    ys = []
    for l in range(S5_NBLK):
        lanes = slice(l * S5_BLK_ST, (l + 1) * S5_BLK_ST)
        ul = u[:, l * S5_BLK_CH:(l + 1) * S5_BLK_CH]
        ar = pwre_ref[0:1, lanes]
        ai = pwim_ref[0:1, lanes]
        sr = s0re_ref[:, lanes]
        si = s0im_ref[:, lanes]
        ub = ul.astype(BF16)
        bur = _dot(ub, bre_ref[l]) + (ar * sr - ai * si)
        bui = _dot(ub, bim_ref[l]) + (ar * si + ai * sr)
        bur, bui = _s5_local_scan(bur, bui, pwre_ref, pwim_ref, lanes, seq_len)
        hre_ref[:, lanes] = bur
        him_ref[:, lanes] = bui
        ys.append(_dot(bur.astype(BF16), cre_ref[l]) - _dot(bui.astype(BF16), cim_ref[l]))
    o_ref[...] = _s5_tail(ys, u, d_ref, wglu_ref, bglu_ref)


def _s5_weight_specs(ws):
    return [_const_spec(w.shape) for w in ws]


def _s5_prompt(p2, n_seq, seq_len, ws):
    TT = PROJ_TILE
    nt = seq_len // TT
    rows = n_seq * seq_len
    st_spec = pl.BlockSpec((1, 1, S5_LANES), lambda b, t: (b, 0, 0))
    return pl.pallas_call(
        _s5_prompt_kernel,
        grid=(n_seq, nt),
        in_specs=[pl.BlockSpec((TT, S5_WIDTH), lambda b, t: (b * nt + t, P_U // S5_WIDTH))]
        + _s5_weight_specs(ws),
        out_specs=[pl.BlockSpec((TT, S5_WIDTH), lambda b, t: (b * nt + t, 0)), st_spec, st_spec],
        out_shape=(jax.ShapeDtypeStruct((rows, S5_WIDTH), F32),
                   jax.ShapeDtypeStruct((n_seq, 1, S5_LANES), F32),
                   jax.ShapeDtypeStruct((n_seq, 1, S5_LANES), F32)),
        scratch_shapes=[pltpu.VMEM((TT, S5_BLK_ST), F32), pltpu.VMEM((TT, S5_BLK_ST), F32),
                        pltpu.VMEM((1, S5_LANES), F32), pltpu.VMEM((1, S5_LANES), F32)],
        compiler_params=_params(("parallel", "arbitrary")),
        name="s5_prompt",
    )(p2, *ws)


def _s5_sample(p2, s0re_rows, s0im_rows, seq_len, ws):
    rows = s0re_rows.shape[0]
    TT = TOK_TILE
    row_spec = pl.BlockSpec((TT, S5_LANES), lambda i: (i, 0))
    return pl.pallas_call(
        functools.partial(_s5_sample_kernel, seq_len=seq_len),
        grid=(rows // TT,),
        in_specs=[pl.BlockSpec((TT, S5_WIDTH), lambda i: (i, P_U // S5_WIDTH)), row_spec, row_spec]
        + _s5_weight_specs(ws),
        out_specs=[pl.BlockSpec((TT, S5_WIDTH), lambda i: (i, 0)), row_spec, row_spec],
        out_shape=(jax.ShapeDtypeStruct((rows, S5_WIDTH), F32),
                   jax.ShapeDtypeStruct((rows, S5_LANES), F32),
                   jax.ShapeDtypeStruct((rows, S5_LANES), F32)),
        compiler_params=_params(("parallel",)),
        name="s5_sample",
    )(p2, s0re_rows, s0im_rows, *ws)


def _outproj_kernel(x_ref, og_ref, os_ref, wt_ref, wb_ref, x1_ref):
    x1 = (x_ref[...] + _dot(og_ref[...].astype(BF16), wt_ref[...])
          + _dot(os_ref[...].astype(BF16), wb_ref[...]))
    x1_ref[...] = x1


def _out_proj(x2, og, osx, wt, wb):
    rows = x2.shape[0]
    big = pl.BlockSpec((PROJ_TILE, D_MODEL), lambda i: (i, 0))
    half = pl.BlockSpec((PROJ_TILE, GLA_WIDTH), lambda i: (i, 0))
    return pl.pallas_call(
        _outproj_kernel,
        grid=(rows // PROJ_TILE,),
        in_specs=[big, half, half] + [_const_spec(w.shape) for w in (wt, wb)],
        out_specs=big,
        out_shape=jax.ShapeDtypeStruct((rows, D_MODEL), F32),
        compiler_params=_params(("parallel",)),
        name="out_proj",
    )(x2, og, osx, wt, wb)


def _first_and_nth_max(work, n):
    first = None
    thr = None
    need = None
    for r in range(n):
        m = jnp.max(work, axis=0, keepdims=True)
        hit = work == m
        if r == 0:
            first, thr = m, m
            need = float(n) - jnp.sum(jnp.where(hit, 1.0, 0.0), axis=0, keepdims=True)
        else:
            thr = jnp.where(need > 0.0, m, thr)
            if r < n - 1:
                need = need - jnp.sum(jnp.where(hit, 1.0, 0.0), axis=0, keepdims=True)
        if r < n - 1:
            work = jnp.where(hit, -jnp.inf, work)
    return first, thr


def _sorted_columns(s):
    n = s.shape[0] // 8
    cols = [s[8 * v:8 * (v + 1)] for v in range(n)]
    k = 2
    while k <= n:
        j = k // 2
        while j >= 1:
            for i in range(n):
                l = i ^ j
                if l > i:
                    hi, lo = jnp.maximum(cols[i], cols[l]), jnp.minimum(cols[i], cols[l])
                    cols[i], cols[l] = (hi, lo) if (i & k) == 0 else (lo, hi)
            j //= 2
        k *= 2
    return cols


def _top_values(s, n, store):
    cols = _sorted_columns(s)
    for r in range(n):
        m = jnp.max(cols[0], axis=0, keepdims=True)
        store(r, m)
        if r < n - 1:
            hit = cols[0] == m
            for v in range(n - 1 - r):
                nxt = cols[v + 1] if v + 1 < len(cols) else jnp.full_like(cols[v], -jnp.inf)
                cols[v] = jnp.where(hit, nxt, cols[v])


def _route_kernel(x1_ref, n2_ref, wq_ref, kh_ref, kl_ref,
                  n1_ref, e1_ref, rho_ref, e2_ref, s_s, top_s):
    TM = x1_ref.shape[0]
    h2 = _rms(x1_ref[...]) * n2_ref[...]
    qp = _dot(h2.astype(BF16), wq_ref[...])
    for hp in range(2 * PEER_HEADS):
        q_hi, q_lo = _split(qp[:, hp * PEER_KEYS:(hp + 1) * PEER_KEYS])
        s_s[hp] = _dot_nt(kh_ref[hp], q_hi) + _dot_nt(kl_ref[hp], q_hi) + _dot_nt(kh_ref[hp], q_lo)
    K = PEER_TOPK
    for lc in range(TM // LANES):
        sl = slice(lc * LANES, (lc + 1) * LANES)
        for hp in range(2 * PEER_HEADS):
            def store(r, m, hp=hp):
                top_s[hp, r:r + 1, sl] = m

            _top_values(s_s[hp, :, sl], K, store)
        for h in range(PEER_HEADS):
            v1 = top_s[2 * h, :, sl]
            v2 = top_s[2 * h + 1, :, sl]
            need = lambda b: v1[0:K // (b + 1)] + v2[b:b + 1]
            pad = jnp.full((1, v1.shape[1]), -jnp.inf, F32)
            cands = [v1 + v2[0:1], v1[0:8] + v2[1:2], v1[0:8] + v2[2:3],
                     jnp.concatenate([v1[0:6] + v2[3:4], need(7)], axis=0),
                     jnp.concatenate([need(4), need(5), need(6), pad], axis=0),
                     v2[8:K] + v1[0:1]]
            cand = jnp.concatenate(cands, axis=0)
            mx, thr = _first_and_nth_max(cand, K)
            z = jnp.sum(jnp.where(cand >= thr, jnp.exp(cand - mx), 0.0), axis=0, keepdims=True)
            cnt = jnp.zeros(v1.shape, F32)
            for b in range(K):
                cnt = cnt + jnp.where(v1 + v2[b:b + 1] >= thr, 1.0, 0.0)
            s1 = s_s[2 * h, :, sl]
            s2 = s_s[2 * h + 1, :, sl]
            n1 = jnp.where(s1 + v2[0:1] >= thr, 1.0, 0.0)
            rho = jnp.where(v1[0:1] + s2 >= thr, float(K // 2), float(K))
            for a in range(K // 2):
                n1 = jnp.where(s1 == v1[a:a + 1], cnt[a:a + 1], n1)
                rho = jnp.where(s2 == v2[a:a + 1], float(a), rho)
            n1_ref[h, :, sl] = n1
            e1_ref[h, :, sl] = jnp.exp(s1 - v1[0:1])
            rho_ref[h, :, sl] = rho.astype(BF16)
            e2_ref[h, :, sl] = (jnp.exp(s2 - v2[0:1]) * (1.0 / z)).astype(BF16)


def _route(x1, n2, wq_bf, kh, kl):
    rows = x1.shape[0]
    TM = TOK_TILE
    key_spec = pl.BlockSpec((PEER_HEADS, PEER_KEYS, TM), lambda i: (0, 0, i))
    return pl.pallas_call(
        _route_kernel,
        grid=(rows // TM,),
        in_specs=[pl.BlockSpec((TM, D_MODEL), lambda i: (i, 0)),
                  _const_spec(n2.shape), _const_spec(wq_bf.shape),
                  _const_spec(kh.shape), _const_spec(kl.shape)],
        out_specs=[key_spec] * 4,
        out_shape=(jax.ShapeDtypeStruct((PEER_HEADS, PEER_KEYS, rows), F32),
                   jax.ShapeDtypeStruct((PEER_HEADS, PEER_KEYS, rows), F32),
                   jax.ShapeDtypeStruct((PEER_HEADS, PEER_KEYS, rows), BF16),
                   jax.ShapeDtypeStruct((PEER_HEADS, PEER_KEYS, rows), BF16)),
        scratch_shapes=[pltpu.VMEM((2 * PEER_HEADS, PEER_KEYS, TM), F32),
                        pltpu.VMEM((2 * PEER_HEADS, PEER_TOPK, TM), F32)],
        compiler_params=_params(("parallel",)),
        name="peer_route",
    )(x1, n2, wq_bf, kh, kl)


PEER_CHUNK = 256


def _peer_gate_piece(i1, c, a, act_ref, wact_ref, n1_ref, e1_ref, rho_ref, e2_ref):
    ls = slice(c * PEER_CHUNK, (c + 1) * PEER_CHUNK)
    rs = slice(a * PEER_KEYS, (a + 1) * PEER_KEYS)
    w = jnp.zeros((PEER_KEYS, PEER_CHUNK), BF16)
    for h in range(PEER_HEADS):
        n1 = n1_ref[h, pl.ds(i1, 1), ls].astype(BF16)
        g1 = e1_ref[h, pl.ds(i1, 1), ls].astype(BF16)
        e2 = e2_ref[h, :, ls]
        w = w + jnp.where(rho_ref[h, :, ls] < n1, e2, jnp.zeros_like(e2)) * g1
    x = act_ref[rs, ls]
    t = jnp.exp((x * (-2.0 * GELU_C - (2.0 * GELU_C * GELU_A) * (x * x))).astype(BF16))
    wact_ref[rs, ls] = w * (x.astype(BF16) / (1.0 + t))


def _peer_half(blk_gate, u_half, vt_half, act_in, wact_out, wact_in, act_out,
               acc, h2b, n1_ref, e1_ref, rho_ref, e2_ref, stages):
    sub = act_in.shape[0] // PEER_KEYS
    for c in range(act_in.shape[1] // PEER_CHUNK):
        ls = slice(c * PEER_CHUNK, (c + 1) * PEER_CHUNK)
        if "C" in stages:
            acc[:, ls] += _dot(vt_half, wact_in[:, ls])
        if "B" in stages:
            for a in range(sub):
                _peer_gate_piece(blk_gate * sub + a, c, a, act_in, wact_out, n1_ref, e1_ref, rho_ref, e2_ref)
        if "A" in stages:
            act_out[:, ls] = _dot_nt(u_half, h2b[c * PEER_CHUNK:(c + 1) * PEER_CHUNK, :])


def _peer_kernel(x1_ref, n2_ref, n1_ref, e1_ref, rho_ref, e2_ref, u_ref, vt_ref, fn_ref, y_ref,
                 acc, h2b, wact0, wact1, act0, act1, *, n_steps):
    g = pl.program_id(1)
    EB = act0.shape[0]
    refs = (acc, h2b, n1_ref, e1_ref, rho_ref, e2_ref)

    def halves(first, second):
        _peer_half(2 * g - 1, u_ref[0:EB, :], vt_ref[0, :, 0:EB], act1, wact1, wact0, act0, *refs,
                   stages=first)
        _peer_half(2 * g, u_ref[EB:2 * EB, :], vt_ref[0, :, EB:2 * EB], act0, wact0, wact1, act1, *refs,
                   stages=second)

    @pl.when(g == 0)
    def _():
        acc[...] = jnp.zeros_like(acc)
        h2b[...] = (_rms(x1_ref[...]) * n2_ref[...]).astype(BF16)
        halves("A", "AB")

    @pl.when((g > 0) & (g < n_steps - 1))
    def _():
        halves("ABC", "ABC")

    @pl.when(g == n_steps - 1)
    def _():
        halves("BC", "C")
        out = x1_ref[...] + acc[...].T
        y_ref[...] = _rms(out) * fn_ref[...]


def _peer(x1, n2, n1, e1, rho, e2, u_bf, vt_bf, fn):
    rows = x1.shape[0]
    TM = PEER_TOK_TILE
    EB = PEER_EXP_TILE
    n_steps = PEER_EXPERTS // (2 * EB) + 1
    big = pl.BlockSpec((TM, D_MODEL), lambda i, g: (i, 0))
    key_spec = pl.BlockSpec((PEER_HEADS, PEER_KEYS, TM), lambda i, g: (0, 0, i))
    return pl.pallas_call(
        functools.partial(_peer_kernel, n_steps=n_steps),
        grid=(rows // TM, n_steps),
        in_specs=[big, _const_spec(n2.shape), key_spec, key_spec, key_spec, key_spec,
                  pl.BlockSpec((2 * EB, D_MODEL), lambda i, g: (jnp.minimum(g, n_steps - 2), 0)),
                  pl.BlockSpec((1, D_MODEL, 2 * EB), lambda i, g: (jnp.maximum(g - 1, 0), 0, 0)),
                  _const_spec(fn.shape)],
        out_specs=big,
        out_shape=jax.ShapeDtypeStruct((rows, D_MODEL), F32),
        scratch_shapes=[pltpu.VMEM((D_MODEL, TM), F32), pltpu.VMEM((TM, D_MODEL), BF16),
                        pltpu.VMEM((EB, TM), BF16), pltpu.VMEM((EB, TM), BF16),
                        pltpu.VMEM((EB, TM), F32), pltpu.VMEM((EB, TM), F32)],
        compiler_params=_params(("parallel", "arbitrary")),
        name="peer_experts",
    )(x1, n2, n1, e1, rho, e2, u_bf, vt_bf, fn)


def _block_diag(w, eye):
    n, g, a, b = w.shape
    return jnp.einsum('lgab,gh->lgahb', w, eye).reshape(n, g * a, g * b)


def _gla_constants():
    lane_head = jnp.arange(GLA_QK) // GLA_DK
    e2 =(lane_head[:, None] == (jnp.arange(GLA_WIDTH) // GLA_DV)[None, :]).astype(BF16)
    sd = SAMPLE_SEQ_BLOCK * GLA_DK
    xc = jnp.arange(GLA_HEADS * sd)
    rep = ((lane_head[:, None] == (xc // sd)[None, :])
           & ((jnp.arange(GLA_QK) % GLA_DK)[:, None] == (xc % GLA_DK)[None, :])).astype(BF16)
    return e2, rep


def kernel(x_prompt, x_sample, state_gla, state_s5_re, state_s5_im, norm1, w_in, w_a2, b_a2, gla_norm, s5_lam_re, s5_lam_im, s5_log_dt, s5_b_re, s5_b_im, s5_c_re, s5_c_im, s5_d, w_glu, b_glu, w_out, norm2, peer_wq, peer_keys, peer_u, peer_v, final_norm):
    depth = norm1.shape[0]
    assert depth == 1, "single-layer trunk"
    n_p, len_p, _ = x_prompt.shape
    n_s, len_s, _ = x_sample.shape
    l = 0

    w = w_in[l]
    w_re = jnp.concatenate([w[:, 0:1536], w[:, 1552:2064], w[:, 1536:1552],
                            jnp.zeros((D_MODEL, W_IN_COLS - 2064), F32)], axis=1)
    w_bf = w_re.astype(BF16)
    a2 = jnp.concatenate([w_a2[l], jnp.zeros((W_IN_COLS - P_LA - GLA_RANK, GLA_QK), F32)], axis=0)
    a2hi, a2lo = _split(a2)
    ba2 = b_a2[l].reshape(1, GLA_QK)
    n1 = norm1[l].reshape(1, D_MODEL)
    gn = gla_norm[l].reshape(1, GLA_WIDTH)
    e2, rep = _gla_constants()

    pwre, pwim, bbre, bbim = _s5_prep(s5_lam_re[l], s5_lam_im[l], s5_log_dt[l], s5_b_re[l], s5_b_im[l])
    eye = jnp.eye(8, dtype=F32)
    blk = lambda t: t.reshape(S5_NBLK, 8, S5_GROUP_CH, S5_STATE)
    bre = _block_diag(blk(bbre), eye).astype(BF16)
    bim = _block_diag(blk(bbim), eye).astype(BF16)
    cre = _block_diag(jnp.swapaxes(blk(s5_c_re[l]), 2, 3), eye).astype(BF16)
    cim = _block_diag(jnp.swapaxes(blk(s5_c_im[l]), 2, 3), eye).astype(BF16)
    s5w = (bre, bim, cre, cim, pwre.reshape(8, S5_LANES), pwim.reshape(8, S5_LANES),
           s5_d[l].reshape(1, S5_WIDTH), w_glu[l].astype(BF16), b_glu[l].reshape(1, S5_WIDTH))

    wt = w_out[l][:GLA_WIDTH].astype(BF16)
    wb = w_out[l][GLA_WIDTH:].astype(BF16)
    n2 = norm2[l].reshape(1, D_MODEL)
    wq_bf = peer_wq[l].astype(BF16)
    kh, kl = _split(peer_keys[l].reshape(2 * PEER_HEADS, PEER_KEYS, PEER_DQ // 2))
    u_bf = peer_u[l].astype(BF16)
    vt_bf = jnp.swapaxes(peer_v[l].astype(BF16).reshape(-1, 2 * PEER_EXP_TILE, D_MODEL), 1, 2)
    fn = final_norm.reshape(1, D_MODEL)

    def tail(x2, og, osx):
        x1 = _out_proj(x2, og, osx, wt, wb)
        n1, e1, rho, e2g = _route(x1, n2, wq_bf, kh, kl)
        return _peer(x1, n2, n1, e1, rho, e2g, u_bf, vt_bf, fn)

    xp = x_prompt.reshape(n_p * len_p, D_MODEL)
    pp = _in_proj(xp, n1, w_bf, a2hi, a2lo, ba2)
    og_p, gla_p = _gla_prompt(pp, n_p, len_p, e2, gn)
    os_p, sre_p, sim_p = _s5_prompt(pp, n_p, len_p, s5w)
    y_p = tail(xp, og_p, os_p).reshape(n_p, len_p, D_MODEL)

    xs = x_sample.reshape(n_s * len_s, D_MODEL)
    ps = _in_proj(xs, n1, w_bf, a2hi, a2lo, ba2)
    og_s, gla_s = _gla_sample(ps, state_gla[l], len_s, e2, rep, gn)
    first_row = lambda s: jnp.pad(s.reshape(n_s, 1, S5_LANES), ((0, 0), (0, len_s - 1), (0, 0))
                                  ).reshape(n_s * len_s, S5_LANES)
    os_s, hre_s, him_s = _s5_sample(ps, first_row(state_s5_re[l]), first_row(state_s5_im[l]), len_s, s5w)
    y_s = tail(xs, og_s, os_s).reshape(n_s, len_s, D_MODEL)
    last_row = lambda hs: hs.reshape(n_s, len_s, S5_GROUPS, S5_STATE)[:, len_s - 1]

    st = lambda a: a.reshape(1, n_p, S5_GROUPS, S5_STATE)
    return (y_p, y_s, gla_p[None], st(sre_p), st(sim_p),
            gla_s[None], last_row(hre_s)[None], last_row(him_s)[None])
```

```python
import functools

import jax
import jax.numpy as jnp
from jax import lax
from jax.experimental import pallas as pl
from jax.experimental.pallas import tpu as pltpu

F32 = jnp.float32
BF16 = jnp.bfloat16

D_MODEL = 1024
GLA_HEADS = 4
GLA_DK = 64
GLA_DV = 128
GLA_QK = GLA_HEADS * GLA_DK
GLA_WIDTH = GLA_HEADS * GLA_DV
GLA_RANK = 16
GLA_TAU = 16.0
GLA_CHUNK = 128
S5_WIDTH = 512
S5_GROUP_CH = 16
S5_GROUPS = 32
S5_STATE = 64
S5_LANES = S5_GROUPS * S5_STATE
S5_NBLK = 4
S5_BLK_CH = S5_WIDTH // S5_NBLK
S5_BLK_ST = S5_LANES // S5_NBLK
PEER_KEYS = 128
PEER_EXPERTS = PEER_KEYS * PEER_KEYS
PEER_HEADS = 8
PEER_DQ = 256
PEER_TOPK = 16
EPS = 1e-6
LANES = 128
GELU_C = 0.7978845608028654
GELU_A = 0.044715

P_Q, P_K, P_V, P_G, P_U, P_LA = 0, 256, 512, 1024, 1536, 2048
P_COLS = 2304
W_IN_COLS = 2176

TOK_TILE = 256
PROJ_TILE = 512
PEER_TOK_TILE = 512
PEER_EXP_TILE = 512
SAMPLE_SEQ_BLOCK = 16
VMEM_LIMIT = 56 * 1024 * 1024


def _split(x):
    hi = x.astype(BF16)
    lo = (x - hi.astype(F32)).astype(BF16)
    return hi, lo


def _split3(x):
    a = x.astype(BF16)
    r = x - a.astype(F32)
    b = r.astype(BF16)
    c = (r - b.astype(F32)).astype(BF16)
    return a, b, c


def _dot(a, b):
    return jnp.dot(a, b, preferred_element_type=F32)


def _dot_nt(a, b):
    return lax.dot_general(a, b, (((1,), (1,)), ((), ())), preferred_element_type=F32)


def _dot_tn(a, b):
    return lax.dot_general(a, b, (((0,), (0,)), ((), ())), preferred_element_type=F32)


def _dot3(a, b_hi, b_lo):
    a_hi, a_lo = _split(a)
    return _dot(a_hi, b_hi) + _dot(a_lo, b_hi) + _dot(a_hi, b_lo)


def _dot_exact01(m01, x):
    a, b, c = _split3(x)
    return _dot(m01, a) + _dot(m01, b) + _dot(m01, c)


def _rms(x):
    return x * lax.rsqrt(jnp.mean(x * x, axis=-1, keepdims=True) + EPS)


def _params(sem):
    return pltpu.CompilerParams(dimension_semantics=sem, vmem_limit_bytes=VMEM_LIMIT)


def _const_spec(shape):
    n = len(shape)
    return pl.BlockSpec(shape, lambda *_: (0,) * n)


def _s5prep_kernel(lr_ref, li_ref, ldt_ref, bret_ref, bimt_ref,
                   pwre_ref, pwim_ref, bbre_ref, bbim_ref):
    lr = lr_ref[...]
    li = li_ref[...]
    dt = jnp.exp(ldt_ref[...])
    mag = jnp.exp(lr * dt)
    abr = mag * jnp.cos(li * dt)
    abi = mag * jnp.sin(li * dt)
    den = lr * lr + li * li
    nr = abr - 1.0
    ni = abi
    fr = (nr * lr + ni * li) / den
    fi = (ni * lr - nr * li) / den
    bret = bret_ref[...]
    bimt = bimt_ref[...]
    bbre_ref[...] = fr[:, None, :] * bret - fi[:, None, :] * bimt
    bbim_ref[...] = fr[:, None, :] * bimt + fi[:, None, :] * bret
    pr, pi = abr, abi
    for i in range(8):
        pwre_ref[i] = pr
        pwim_ref[i] = pi
        pr, pi = pr * abr - pi * abi, pr * abi + pi * abr


def _s5_prep(lam_re, lam_im, log_dt, b_re, b_im):
    g, p = lam_re.shape
    ch = b_re.shape[-1]
    bret = jnp.transpose(b_re, (0, 2, 1))
    bimt = jnp.transpose(b_im, (0, 2, 1))
    out = pl.pallas_call(
        _s5prep_kernel,
        out_shape=(jax.ShapeDtypeStruct((8, g, p), F32), jax.ShapeDtypeStruct((8, g, p), F32),
                   jax.ShapeDtypeStruct((g, ch, p), F32), jax.ShapeDtypeStruct((g, ch, p), F32)),
        name="s5_prep",
    )(lam_re, lam_im, log_dt.reshape(g, 1), bret, bimt)
    return out


def _inproj_kernel(x_ref, n1_ref, w_ref, a2hi_ref, a2lo_ref, ba2_ref, p_ref):
    h = _rms(x_ref[...]) * n1_ref[...]
    p = _dot(h.astype(BF16), w_ref[...])
    alr = p[:, P_LA:W_IN_COLS]
    z = _dot3(alr, a2hi_ref[...], a2lo_ref[...]) + ba2_ref[...]
    log_sig = jnp.minimum(z, 0.0) - jnp.log1p(jnp.exp(-jnp.abs(z)))
    p_ref[:, 0:P_LA] = p[:, 0:P_LA]
    p_ref[:, P_LA:P_COLS] = log_sig * (1.0 / GLA_TAU)


def _in_proj(x2, n1, w_bf, a2hi, a2lo, ba2):
    rows = x2.shape[0]
    return pl.pallas_call(
        _inproj_kernel,
        grid=(rows // PROJ_TILE,),
        in_specs=[pl.BlockSpec((PROJ_TILE, D_MODEL), lambda i: (i, 0)),
                  _const_spec(n1.shape), _const_spec(w_bf.shape),
                  _const_spec(a2hi.shape), _const_spec(a2lo.shape), _const_spec(ba2.shape)],
        out_specs=pl.BlockSpec((PROJ_TILE, P_COLS), lambda i: (i, 0)),
        out_shape=jax.ShapeDtypeStruct((rows, P_COLS), F32),
        compiler_params=_params(("parallel",)),
        name="in_proj",
    )(x2, n1, w_bf, a2hi, a2lo, ba2)


def _cumsum_rows(mask01, la):
    return _dot_exact01(mask01.astype(BF16), la)


GLA_BAND = 4
SUBLANES = 8


def _gla_prompt_kernel(q_ref, k_ref, v_ref, g_ref, la_ref, e2_ref, gn_ref,
                       o_ref, sfin_ref, st_ref, slab, *, n_chunks):
    c = pl.program_id(1)
    C = q_ref.shape[0]
    HC = GLA_HEADS * C

    @pl.when(c == 0)
    def _():
        st_ref[...] = jnp.zeros_like(st_ref)

    la = la_ref[...]
    row = lax.broadcasted_iota(jnp.int32, (C, C), 0)
    col = lax.broadcasted_iota(jnp.int32, (C, C), 1)
    b = _cumsum_rows(col <= row, la)
    q = q_ref[...] * (GLA_DK ** -0.5)
    k = k_ref[...]
    v = v_ref[...]
    blast = b[C - 1:C, :]

    rloc = lax.broadcasted_iota(jnp.int32, (C, GLA_QK), 0) % GLA_BAND
    tiles = lambda t: t.reshape(C // SUBLANES, SUBLANES, t.shape[-1])
    k3, b3, v3 = tiles(k), tiles(b), tiles(v)
    vsh = [v]
    for d in range(GLA_BAND):
        if d == 0:
            ks_, bs_ = k, b
        else:
            ks_ = pltpu.roll(k3, d, 1).reshape(C, GLA_QK)
            bs_ = pltpu.roll(b3, d, 1).reshape(C, GLA_QK)
            vsh.append(pltpu.roll(v3, d, 1).reshape(C, GLA_WIDTH))
        m = q * ks_ * jnp.exp(jnp.minimum(b - bs_, 0.0))
        slab[d * C:(d + 1) * C, :] = jnp.where(rloc >= d, m, 0.0).astype(BF16)
    rep = _dot(slab[...], e2_ref[...])
    o = rep[0:C] * vsh[0]
    for d in range(1, GLA_BAND):
        o = o + rep[d * C:(d + 1) * C] * vsh[d]

    lane_head = lax.broadcasted_iota(jnp.int32, (HC, GLA_QK), 1) // GLA_DK
    row_head = lax.broadcasted_iota(jnp.int32, (HC, GLA_QK), 0) // C
    own_head = lane_head == row_head
    si = lax.broadcasted_iota(jnp.int32, (HC, C), 0) % C
    sj = lax.broadcasted_iota(jnp.int32, (HC, C), 1)
    scores = jnp.zeros((HC, C), F32)
    s = C // 2
    while s >= GLA_BAND:
        ref = jnp.concatenate([jnp.broadcast_to(b[p * 2 * s + s - 1:p * 2 * s + s, :], (2 * s, GLA_QK))
                               for p in range(C // (2 * s))], axis=0)
        ql = q * jnp.exp(jnp.minimum(b - ref, 0.0))
        kl = (k * jnp.exp(jnp.minimum(ref - b, 0.0))).astype(BF16)
        qs = jnp.where(own_head, jnp.concatenate([ql] * GLA_HEADS, axis=0), 0.0).astype(BF16)
        lvl = ((si // (2 * s)) == (sj // (2 * s))) & ((si // s) % 2 == 1) & ((sj // s) % 2 == 0)
        scores = scores + jnp.where(lvl, _dot_nt(qs, kl), 0.0)
        s //= 2
    scores = scores.astype(BF16)
    vb = v.astype(BF16)

    st = st_ref[...]
    o = o + _dot_nt((q * jnp.exp(b)).astype(BF16), st.astype(BF16))
    outs = []
    for h in range(GLA_HEADS):
        vs = slice(h * GLA_DV, (h + 1) * GLA_DV)
        outs.append(_rms(o[:, vs] + _dot(scores[h * C:(h + 1) * C], vb[:, vs])))
    g = g_ref[...]
    o_ref[...] = jnp.concatenate(outs, axis=-1) * gn_ref[...] * (g * jax.nn.sigmoid(g))

    kd_hi, kd_lo = _split(k * jnp.exp(blast - b))
    vt_hi, vt_lo = _split(v.T)
    upd = _dot(vt_hi, kd_hi) + _dot(vt_lo, kd_hi) + _dot(vt_hi, kd_lo)
    blk = (lax.broadcasted_iota(jnp.int32, st.shape, 0) // GLA_DV
           == lax.broadcasted_iota(jnp.int32, st.shape, 1) // GLA_DK)
    st_new = jnp.exp(blast) * st + jnp.where(blk, upd, 0.0)
    st_ref[...] = st_new

    @pl.when(c == n_chunks - 1)
    def _():
        for h in range(GLA_HEADS):
            sfin_ref[0, h] = st_new[h * GLA_DV:(h + 1) * GLA_DV, h * GLA_DK:(h + 1) * GLA_DK].T


def _gla_prompt(p2, n_seq, seq_len, e2, gn):
    C = GLA_CHUNK
    nch = seq_len // C
    rows = n_seq * seq_len

    def tok(width, colblk):
        return pl.BlockSpec((C, width), lambda b, c: (b * nch + c, colblk))

    return pl.pallas_call(
        functools.partial(_gla_prompt_kernel, n_chunks=nch),
        grid=(n_seq, nch),
        in_specs=[tok(GLA_QK, P_Q // GLA_QK), tok(GLA_QK, P_K // GLA_QK),
                  tok(GLA_WIDTH, P_V // GLA_WIDTH), tok(GLA_WIDTH, P_G // GLA_WIDTH),
                  tok(GLA_QK, P_LA // GLA_QK),
                  _const_spec(e2.shape), _const_spec(gn.shape)],
        out_specs=[pl.BlockSpec((C, GLA_WIDTH), lambda b, c: (b * nch + c, 0)),
                   pl.BlockSpec((1, GLA_HEADS, GLA_DK, GLA_DV), lambda b, c: (b, 0, 0, 0))],
        out_shape=(jax.ShapeDtypeStruct((rows, GLA_WIDTH), F32),
                   jax.ShapeDtypeStruct((n_seq, GLA_HEADS, GLA_DK, GLA_DV), F32)),
        scratch_shapes=[pltpu.VMEM((GLA_WIDTH, GLA_QK), F32),
                        pltpu.VMEM((GLA_BAND * C, GLA_QK), BF16)],
        compiler_params=_params(("parallel", "arbitrary")),
        name="gla_prompt",
    )(p2, p2, p2, p2, p2, e2, gn)


def _gla_sample_kernel(q_ref, k_ref, v_ref, g_ref, la_ref, s0_ref, e2_ref, rep_ref, gn_ref,
                       o_ref, snew_ref, *, seq_len):
    R = q_ref.shape[0]
    nseq = R // seq_len
    SD = nseq * GLA_DK
    la = la_ref[...]
    row = lax.broadcasted_iota(jnp.int32, (R, R), 0)
    col = lax.broadcasted_iota(jnp.int32, (R, R), 1)
    same = (col // seq_len) == (row // seq_len)
    b = _cumsum_rows(same & (col <= row), la)
    btot = _cumsum_rows(same, la)
    q = q_ref[...] * (GLA_DK ** -0.5)
    k = k_ref[...]
    v = v_ref[...]
    rmod = lax.broadcasted_iota(jnp.int32, (R, GLA_QK), 0) % seq_len

    o = jnp.zeros((R, GLA_WIDTH), F32)
    for d in range(seq_len):
        ks_, bs_, vs_ = (k, b, v) if d == 0 else (pltpu.roll(k, d, 0), pltpu.roll(b, d, 0),
                                                  pltpu.roll(v, d, 0))
        m = q * ks_ * jnp.exp(jnp.minimum(b - bs_, 0.0))
        m = jnp.where(rmod >= d, m, 0.0)
        o = o + _dot(m.astype(BF16), e2_ref[...]) * vs_

    xr = lax.broadcasted_iota(jnp.int32, (R, GLA_HEADS * SD), 0) // seq_len
    xc = (lax.broadcasted_iota(jnp.int32, (R, GLA_HEADS * SD), 1) % SD) // GLA_DK
    own = xr == xc
    rep = rep_ref[...]

    def expand(x_bf16):
        return jnp.where(own, _dot(x_bf16, rep), 0.0).astype(BF16)

    qx = expand((q * jnp.exp(b)).astype(BF16))
    kd_hi, kd_lo = _split(k * jnp.exp(btot - b))
    kx_hi, kx_lo = expand(kd_hi), expand(kd_lo)
    ea, eb, ec = _split3(jnp.exp(btot))
    ax = (expand(ea), expand(eb), expand(ec))
    last = (lax.broadcasted_iota(jnp.int32, (R, GLA_DV), 0) % seq_len == seq_len - 1).astype(BF16)

    outs = []
    for h in range(GLA_HEADS):
        xs = slice(h * SD, (h + 1) * SD)
        vs = slice(h * GLA_DV, (h + 1) * GLA_DV)
        s0 = s0_ref[:, h].reshape(SD, GLA_DV)
        o_h = o[:, vs] + _dot(qx[:, xs], s0.astype(BF16))
        outs.append(_rms(o_h))
        v_hi, v_lo = _split(v[:, vs])
        upd = _dot_tn(kx_hi[:, xs], v_hi) + _dot_tn(kx_lo[:, xs], v_hi) + _dot_tn(kx_hi[:, xs], v_lo)
        decay = _dot_tn(ax[0][:, xs], last) + _dot_tn(ax[1][:, xs], last) + _dot_tn(ax[2][:, xs], last)
        snew_ref[:, h] = (decay * s0 + upd).reshape(nseq, GLA_DK, GLA_DV)
    g = g_ref[...]
    o_ref[...] = jnp.concatenate(outs, axis=-1) * gn_ref[...] * (g * jax.nn.sigmoid(g))


def _gla_sample(p2, s0, seq_len, e2, rep, gn):
    n_seq = s0.shape[0]
    R = SAMPLE_SEQ_BLOCK * seq_len
    nblk = n_seq // SAMPLE_SEQ_BLOCK

    def tok(width, colblk):
        return pl.BlockSpec((R, width), lambda i: (i, colblk))

    st_spec = pl.BlockSpec((SAMPLE_SEQ_BLOCK, GLA_HEADS, GLA_DK, GLA_DV), lambda i: (i, 0, 0, 0))
    return pl.pallas_call(
        functools.partial(_gla_sample_kernel, seq_len=seq_len),
        grid=(nblk,),
        in_specs=[tok(GLA_QK, P_Q // GLA_QK), tok(GLA_QK, P_K // GLA_QK),
                  tok(GLA_WIDTH, P_V // GLA_WIDTH), tok(GLA_WIDTH, P_G // GLA_WIDTH),
                  tok(GLA_QK, P_LA // GLA_QK), st_spec,
                  _const_spec(e2.shape), _const_spec(rep.shape), _const_spec(gn.shape)],
        out_specs=[pl.BlockSpec((R, GLA_WIDTH), lambda i: (i, 0)), st_spec],
        out_shape=(jax.ShapeDtypeStruct((n_seq * seq_len, GLA_WIDTH), F32),
                   jax.ShapeDtypeStruct(s0.shape, F32)),
        compiler_params=_params(("parallel",)),
        name="gla_sample",
    )(p2, p2, p2, p2, p2, s0, e2, rep, gn)


def _s5_local_scan(bur, bui, pwre_ref, pwim_ref, lanes, group):
    rows, width = bur.shape
    xr = bur.reshape(rows // 8, 8, width)
    xi = bui.reshape(rows // 8, 8, width)
    sub = lax.broadcasted_iota(jnp.int32, (8, width), 0) % group
    s = 1
    while s < group:
        ar = jnp.where(sub >= s, pwre_ref[s - 1:s, lanes], 0.0)[None]
        ai = jnp.where(sub >= s, pwim_ref[s - 1:s, lanes], 0.0)[None]
        sr = pltpu.roll(xr, s, 1)
        si = pltpu.roll(xi, s, 1)
        xr, xi = xr + ar * sr - ai * si, xi + ar * si + ai * sr
        s *= 2
    return xr.reshape(rows, width), xi.reshape(rows, width)


def _s5_tail(ys, u, d_ref, wglu_ref, bglu_ref):
    y = jnp.concatenate(ys, axis=-1) + d_ref[...] * u
    z = jax.nn.gelu(y)
    return z * jax.nn.sigmoid(_dot(z.astype(BF16), wglu_ref[...]) + bglu_ref[...])


def _s5_prompt_kernel(u_ref, bre_ref, bim_ref, cre_ref, cim_ref,
                      pwre_ref, pwim_ref, d_ref, wglu_ref, bglu_ref,
                      o_ref, stre_ref, stim_ref, hre_s, him_s, car_re, car_im):
    t = pl.program_id(1)
    TT = u_ref.shape[0]

    @pl.when(t == 0)
    def _():
        car_re[...] = jnp.zeros_like(car_re)
        car_im[...] = jnp.zeros_like(car_im)

    u = u_ref[...]
    for l in range(S5_NBLK):
        lanes = slice(l * S5_BLK_ST, (l + 1) * S5_BLK_ST)
        ub = u[:, l * S5_BLK_CH:(l + 1) * S5_BLK_CH].astype(BF16)
        bur, bui = _s5_local_scan(_dot(ub, bre_ref[l]), _dot(ub, bim_ref[l]), pwre_ref, pwim_ref, lanes, 8)
        hre_s[:, lanes] = bur
        him_s[:, lanes] = bui
    for l in range(S5_NBLK):
        lanes = slice(l * S5_BLK_ST, (l + 1) * S5_BLK_ST)
        p8r = pwre_ref[:, lanes]
        p8i = pwim_ref[:, lanes]

        def grp(r, carry, lanes=lanes, p8r=p8r, p8i=p8i):
            cr, ci = carry
            off = pl.multiple_of(r * 8, 8)
            xr = hre_s[pl.ds(off, 8), lanes] + p8r * cr - p8i * ci
            xi = him_s[pl.ds(off, 8), lanes] + p8r * ci + p8i * cr
            hre_s[pl.ds(off, 8), lanes] = xr
            him_s[pl.ds(off, 8), lanes] = xi
            return (jnp.broadcast_to(xr[7:8], xr.shape), jnp.broadcast_to(xi[7:8], xi.shape))

        cr0 = jnp.broadcast_to(car_re[:, lanes], (8, S5_BLK_ST))
        ci0 = jnp.broadcast_to(car_im[:, lanes], (8, S5_BLK_ST))
        cr, ci = lax.fori_loop(0, TT // 8, grp, (cr0, ci0))
        car_re[:, lanes] = cr[0:1]
        car_im[:, lanes] = ci[0:1]
    ys = []
    for l in range(S5_NBLK):
        lanes = slice(l * S5_BLK_ST, (l + 1) * S5_BLK_ST)
        ys.append(_dot(hre_s[:, lanes].astype(BF16), cre_ref[l]) - _dot(him_s[:, lanes].astype(BF16), cim_ref[l]))
    o_ref[...] = _s5_tail(ys, u, d_ref, wglu_ref, bglu_ref)
    stre_ref[0] = car_re[...]
    stim_ref[0] = car_im[...]


def _s5_sample_kernel(u_ref, s0re_ref, s0im_ref, bre_ref, bim_ref, cre_ref, cim_ref,
                      pwre_ref, pwim_ref, d_ref, wglu_ref, bglu_ref,
                      o_ref, hre_ref, him_ref, *, seq_len):
    u = u_ref[...]
    ys = []
    for l in range(S5_NBLK):
        lanes = slice(l * S5_BLK_ST, (l + 1) * S5_BLK_ST)
        ul = u[:, l * S5_BLK_CH:(l + 1) * S5_BLK_CH]
        ar = pwre_ref[0:1, lanes]
        ai = pwim_ref[0:1, lanes]
        sr = s0re_ref[:, lanes]
        si = s0im_ref[:, lanes]
        ub = ul.astype(BF16)
        bur = _dot(ub, bre_ref[l]) + (ar * sr - ai * si)
        bui = _dot(ub, bim_ref[l]) + (ar * si + ai * sr)
        bur, bui = _s5_local_scan(bur, bui, pwre_ref, pwim_ref, lanes, seq_len)
        hre_ref[:, lanes] = bur
        him_ref[:, lanes] = bui
        ys.append(_dot(bur.astype(BF16), cre_ref[l]) - _dot(bui.astype(BF16), cim_ref[l]))
    o_ref[...] = _s5_tail(ys, u, d_ref, wglu_ref, bglu_ref)


def _s5_weight_specs(ws):
    return [_const_spec(w.shape) for w in ws]


def _s5_prompt(p2, n_seq, seq_len, ws):
    TT = PROJ_TILE
    nt = seq_len // TT
    rows = n_seq * seq_len
    st_spec = pl.BlockSpec((1, 1, S5_LANES), lambda b, t: (b, 0, 0))
    return pl.pallas_call(
        _s5_prompt_kernel,
        grid=(n_seq, nt),
        in_specs=[pl.BlockSpec((TT, S5_WIDTH), lambda b, t: (b * nt + t, P_U // S5_WIDTH))]
        + _s5_weight_specs(ws),
        out_specs=[pl.BlockSpec((TT, S5_WIDTH), lambda b, t: (b * nt + t, 0)), st_spec, st_spec],
        out_shape=(jax.ShapeDtypeStruct((rows, S5_WIDTH), F32),
                   jax.ShapeDtypeStruct((n_seq, 1, S5_LANES), F32),
                   jax.ShapeDtypeStruct((n_seq, 1, S5_LANES), F32)),
        scratch_shapes=[pltpu.VMEM((TT, S5_LANES), F32), pltpu.VMEM((TT, S5_LANES), F32),
                        pltpu.VMEM((1, S5_LANES), F32), pltpu.VMEM((1, S5_LANES), F32)],
        compiler_params=_params(("parallel", "arbitrary")),
        name="s5_prompt",
    )(p2, *ws)


def _s5_sample(p2, s0re_rows, s0im_rows, seq_len, ws):
    rows = s0re_rows.shape[0]
    TT = TOK_TILE
    row_spec = pl.BlockSpec((TT, S5_LANES), lambda i: (i, 0))
    return pl.pallas_call(
        functools.partial(_s5_sample_kernel, seq_len=seq_len),
        grid=(rows // TT,),
        in_specs=[pl.BlockSpec((TT, S5_WIDTH), lambda i: (i, P_U // S5_WIDTH)), row_spec, row_spec]
        + _s5_weight_specs(ws),
        out_specs=[pl.BlockSpec((TT, S5_WIDTH), lambda i: (i, 0)), row_spec, row_spec],
        out_shape=(jax.ShapeDtypeStruct((rows, S5_WIDTH), F32),
                   jax.ShapeDtypeStruct((rows, S5_LANES), F32),
                   jax.ShapeDtypeStruct((rows, S5_LANES), F32)),
        compiler_params=_params(("parallel",)),
        name="s5_sample",
    )(p2, s0re_rows, s0im_rows, *ws)


def _outproj_kernel(x_ref, og_ref, os_ref, wt_ref, wb_ref, x1_ref):
    x1 = (x_ref[...] + _dot(og_ref[...].astype(BF16), wt_ref[...])
          + _dot(os_ref[...].astype(BF16), wb_ref[...]))
    x1_ref[...] = x1


def _out_proj(x2, og, osx, wt, wb):
    rows = x2.shape[0]
    big = pl.BlockSpec((PROJ_TILE, D_MODEL), lambda i: (i, 0))
    half = pl.BlockSpec((PROJ_TILE, GLA_WIDTH), lambda i: (i, 0))
    return pl.pallas_call(
        _outproj_kernel,
        grid=(rows // PROJ_TILE,),
        in_specs=[big, half, half] + [_const_spec(w.shape) for w in (wt, wb)],
        out_specs=big,
        out_shape=jax.ShapeDtypeStruct((rows, D_MODEL), F32),
        compiler_params=_params(("parallel",)),
        name="out_proj",
    )(x2, og, osx, wt, wb)


def _first_and_nth_max(work, n):
    first = None
    thr = None
    need = None
    for r in range(n):
        m = jnp.max(work, axis=0, keepdims=True)
        hit = work == m
        if r == 0:
            first, thr = m, m
            need = float(n) - jnp.sum(jnp.where(hit, 1.0, 0.0), axis=0, keepdims=True)
        else:
            thr = jnp.where(need > 0.0, m, thr)
            if r < n - 1:
                need = need - jnp.sum(jnp.where(hit, 1.0, 0.0), axis=0, keepdims=True)
        if r < n - 1:
            work = jnp.where(hit, -jnp.inf, work)
    return first, thr


def _sorted_columns(s):
    n = s.shape[0] // 8
    cols = [s[8 * v:8 * (v + 1)] for v in range(n)]
    k = 2
    while k <= n:
        j = k // 2
        while j >= 1:
            for i in range(n):
                l = i ^ j
                if l > i:
                    hi, lo = jnp.maximum(cols[i], cols[l]), jnp.minimum(cols[i], cols[l])
                    cols[i], cols[l] = (hi, lo) if (i & k) == 0 else (lo, hi)
            j //= 2
        k *= 2
    return cols


def _top_values(s, n, store):
    cols = _sorted_columns(s)
    for r in range(n):
        m = jnp.max(cols[0], axis=0, keepdims=True)
        store(r, m)
        if r < n - 1:
            hit = cols[0] == m
            for v in range(n - 1 - r):
                nxt = cols[v + 1] if v + 1 < len(cols) else jnp.full_like(cols[v], -jnp.inf)
                cols[v] = jnp.where(hit, nxt, cols[v])


def _route_kernel(x1_ref, n2_ref, wq_ref, kh_ref, kl_ref,
                  n1_ref, e1_ref, rho_ref, e2_ref, s_s, top_s):
    TM = x1_ref.shape[0]
    h2 = _rms(x1_ref[...]) * n2_ref[...]
    qp = _dot(h2.astype(BF16), wq_ref[...])
    for hp in range(2 * PEER_HEADS):
        q_hi, q_lo = _split(qp[:, hp * PEER_KEYS:(hp + 1) * PEER_KEYS])
        s_s[hp] = _dot_nt(kh_ref[hp], q_hi) + _dot_nt(kl_ref[hp], q_hi) + _dot_nt(kh_ref[hp], q_lo)
    K = PEER_TOPK
    for lc in range(TM // LANES):
        sl = slice(lc * LANES, (lc + 1) * LANES)
        for hp in range(2 * PEER_HEADS):
            def store(r, m, hp=hp):
                top_s[hp, r:r + 1, sl] = m

            _top_values(s_s[hp, :, sl], K, store)
        for h in range(PEER_HEADS):
            v1 = top_s[2 * h, :, sl]
            v2 = top_s[2 * h + 1, :, sl]
            need = lambda b: v1[0:K // (b + 1)] + v2[b:b + 1]
            pad = jnp.full((1, v1.shape[1]), -jnp.inf, F32)
            cands = [v1 + v2[0:1], v1[0:8] + v2[1:2], v1[0:8] + v2[2:3],
                     jnp.concatenate([v1[0:6] + v2[3:4], need(7)], axis=0),
                     jnp.concatenate([need(4), need(5), need(6), pad], axis=0),
                     v2[8:K] + v1[0:1]]
            cand = jnp.concatenate(cands, axis=0)
            mx, thr = _first_and_nth_max(cand, K)
            z = jnp.sum(jnp.where(cand >= thr, jnp.exp(cand - mx), 0.0), axis=0, keepdims=True)
            cnt = jnp.zeros(v1.shape, F32)
            for b in range(K):
                cnt = cnt + jnp.where(v1 + v2[b:b + 1] >= thr, 1.0, 0.0)
            s1 = s_s[2 * h, :, sl]
            s2 = s_s[2 * h + 1, :, sl]
            n1 = jnp.where(s1 + v2[0:1] >= thr, 1.0, 0.0)
            rho = jnp.where(v1[0:1] + s2 >= thr, float(K // 2), float(K))
            for a in range(K // 2):
                n1 = jnp.where(s1 == v1[a:a + 1], cnt[a:a + 1], n1)
                rho = jnp.where(s2 == v2[a:a + 1], float(a), rho)
            n1_ref[h, :, sl] = n1
            e1_ref[h, :, sl] = jnp.exp(s1 - v1[0:1])
            rho_ref[h, :, sl] = rho.astype(BF16)
            e2_ref[h, :, sl] = (jnp.exp(s2 - v2[0:1]) * (1.0 / z)).astype(BF16)


def _route(x1, n2, wq_bf, kh, kl):
    rows = x1.shape[0]
    TM = TOK_TILE
    key_spec = pl.BlockSpec((PEER_HEADS, PEER_KEYS, TM), lambda i: (0, 0, i))
    return pl.pallas_call(
        _route_kernel,
        grid=(rows // TM,),
        in_specs=[pl.BlockSpec((TM, D_MODEL), lambda i: (i, 0)),
                  _const_spec(n2.shape), _const_spec(wq_bf.shape),
                  _const_spec(kh.shape), _const_spec(kl.shape)],
        out_specs=[key_spec] * 4,
        out_shape=(jax.ShapeDtypeStruct((PEER_HEADS, PEER_KEYS, rows), F32),
                   jax.ShapeDtypeStruct((PEER_HEADS, PEER_KEYS, rows), F32),
                   jax.ShapeDtypeStruct((PEER_HEADS, PEER_KEYS, rows), BF16),
                   jax.ShapeDtypeStruct((PEER_HEADS, PEER_KEYS, rows), BF16)),
        scratch_shapes=[pltpu.VMEM((2 * PEER_HEADS, PEER_KEYS, TM), F32),
                        pltpu.VMEM((2 * PEER_HEADS, PEER_TOPK, TM), F32)],
        compiler_params=_params(("parallel",)),
        name="peer_route",
    )(x1, n2, wq_bf, kh, kl)


PEER_CHUNK = 256


def _peer_gate_piece(i1, c, a, act_ref, wact_ref, n1_ref, e1_ref, rho_ref, e2_ref):
    ls = slice(c * PEER_CHUNK, (c + 1) * PEER_CHUNK)
    rs = slice(a * PEER_KEYS, (a + 1) * PEER_KEYS)
    w = jnp.zeros((PEER_KEYS, PEER_CHUNK), BF16)
    for h in range(PEER_HEADS):
        n1 = n1_ref[h, pl.ds(i1, 1), ls].astype(BF16)
        g1 = e1_ref[h, pl.ds(i1, 1), ls].astype(BF16)
        e2 = e2_ref[h, :, ls]
        w = w + jnp.where(rho_ref[h, :, ls] < n1, e2, jnp.zeros_like(e2)) * g1
    x = act_ref[rs, ls]
    t = jnp.exp((x * (-2.0 * GELU_C - (2.0 * GELU_C * GELU_A) * (x * x))).astype(BF16))
    wact_ref[rs, ls] = w * (x.astype(BF16) / (1.0 + t))


def _peer_half(blk_gate, u_half, vt_half, act_in, wact_out, wact_in, act_out,
               acc, h2b, n1_ref, e1_ref, rho_ref, e2_ref, stages):
    sub = act_in.shape[0] // PEER_KEYS
    for c in range(act_in.shape[1] // PEER_CHUNK):
        ls = slice(c * PEER_CHUNK, (c + 1) * PEER_CHUNK)
        if "C" in stages:
            acc[:, ls] += _dot(vt_half, wact_in[:, ls])
        if "B" in stages:
            for a in range(sub):
                _peer_gate_piece(blk_gate * sub + a, c, a, act_in, wact_out, n1_ref, e1_ref, rho_ref, e2_ref)
        if "A" in stages:
            act_out[:, ls] = _dot_nt(u_half, h2b[c * PEER_CHUNK:(c + 1) * PEER_CHUNK, :])


def _peer_kernel(x1_ref, n2_ref, n1_ref, e1_ref, rho_ref, e2_ref, u_ref, vt_ref, fn_ref, y_ref,
                 acc, h2b, wact0, wact1, act0, act1, *, n_steps):
    g = pl.program_id(1)
    EB = act0.shape[0]
    refs = (acc, h2b, n1_ref, e1_ref, rho_ref, e2_ref)

    def halves(first, second):
        _peer_half(2 * g - 1, u_ref[0:EB, :], vt_ref[0, :, 0:EB], act1, wact1, wact0, act0, *refs,
                   stages=first)
        _peer_half(2 * g, u_ref[EB:2 * EB, :], vt_ref[0, :, EB:2 * EB], act0, wact0, wact1, act1, *refs,
                   stages=second)

    @pl.when(g == 0)
    def _():
        acc[...] = jnp.zeros_like(acc)
        h2b[...] = (_rms(x1_ref[...]) * n2_ref[...]).astype(BF16)
        halves("A", "AB")

    @pl.when((g > 0) & (g < n_steps - 1))
    def _():
        halves("ABC", "ABC")

    @pl.when(g == n_steps - 1)
    def _():
        halves("BC", "C")
        out = x1_ref[...] + acc[...].T
        y_ref[...] = _rms(out) * fn_ref[...]


def _peer(x1, n2, n1, e1, rho, e2, u_bf, vt_bf, fn):
    rows = x1.shape[0]
    TM = PEER_TOK_TILE
    EB = PEER_EXP_TILE
    n_steps = PEER_EXPERTS // (2 * EB) + 1
    big = pl.BlockSpec((TM, D_MODEL), lambda i, g: (i, 0))
    key_spec = pl.BlockSpec((PEER_HEADS, PEER_KEYS, TM), lambda i, g: (0, 0, i))
    return pl.pallas_call(
        functools.partial(_peer_kernel, n_steps=n_steps),
        grid=(rows // TM, n_steps),
        in_specs=[big, _const_spec(n2.shape), key_spec, key_spec, key_spec, key_spec,
                  pl.BlockSpec((2 * EB, D_MODEL), lambda i, g: (jnp.minimum(g, n_steps - 2), 0)),
                  pl.BlockSpec((1, D_MODEL, 2 * EB), lambda i, g: (jnp.maximum(g - 1, 0), 0, 0)),
                  _const_spec(fn.shape)],
        out_specs=big,
        out_shape=jax.ShapeDtypeStruct((rows, D_MODEL), F32),
        scratch_shapes=[pltpu.VMEM((D_MODEL, TM), F32), pltpu.VMEM((TM, D_MODEL), BF16),
                        pltpu.VMEM((EB, TM), BF16), pltpu.VMEM((EB, TM), BF16),
                        pltpu.VMEM((EB, TM), F32), pltpu.VMEM((EB, TM), F32)],
        compiler_params=_params(("parallel", "arbitrary")),
        name="peer_experts",
    )(x1, n2, n1, e1, rho, e2, u_bf, vt_bf, fn)


def _block_diag(w, eye):
    n, g, a, b = w.shape
    return jnp.einsum('lgab,gh->lgahb', w, eye).reshape(n, g * a, g * b)


def _gla_constants():
    lane_head = jnp.arange(GLA_QK) // GLA_DK
    e2 =(lane_head[:, None] == (jnp.arange(GLA_WIDTH) // GLA_DV)[None, :]).astype(BF16)
    sd = SAMPLE_SEQ_BLOCK * GLA_DK
    xc = jnp.arange(GLA_HEADS * sd)
    rep = ((lane_head[:, None] == (xc // sd)[None, :])
           & ((jnp.arange(GLA_QK) % GLA_DK)[:, None] == (xc % GLA_DK)[None, :])).astype(BF16)
    return e2, rep


def kernel(x_prompt, x_sample, state_gla, state_s5_re, state_s5_im, norm1, w_in, w_a2, b_a2, gla_norm, s5_lam_re, s5_lam_im, s5_log_dt, s5_b_re, s5_b_im, s5_c_re, s5_c_im, s5_d, w_glu, b_glu, w_out, norm2, peer_wq, peer_keys, peer_u, peer_v, final_norm):
    depth = norm1.shape[0]
    assert depth == 1, "single-layer trunk"
    n_p, len_p, _ = x_prompt.shape
    n_s, len_s, _ = x_sample.shape
    l = 0

    w = w_in[l]
    w_re = jnp.concatenate([w[:, 0:1536], w[:, 1552:2064], w[:, 1536:1552],
                            jnp.zeros((D_MODEL, W_IN_COLS - 2064), F32)], axis=1)
    w_bf = w_re.astype(BF16)
    a2 = jnp.concatenate([w_a2[l], jnp.zeros((W_IN_COLS - P_LA - GLA_RANK, GLA_QK), F32)], axis=0)
    a2hi, a2lo = _split(a2)
    ba2 = b_a2[l].reshape(1, GLA_QK)
    n1 = norm1[l].reshape(1, D_MODEL)
    gn = gla_norm[l].reshape(1, GLA_WIDTH)
    e2, rep = _gla_constants()

    pwre, pwim, bbre, bbim = _s5_prep(s5_lam_re[l], s5_lam_im[l], s5_log_dt[l], s5_b_re[l], s5_b_im[l])
    eye = jnp.eye(8, dtype=F32)
    blk = lambda t: t.reshape(S5_NBLK, 8, S5_GROUP_CH, S5_STATE)
    bre = _block_diag(blk(bbre), eye).astype(BF16)
    bim = _block_diag(blk(bbim), eye).astype(BF16)
    cre = _block_diag(jnp.swapaxes(blk(s5_c_re[l]), 2, 3), eye).astype(BF16)
    cim = _block_diag(jnp.swapaxes(blk(s5_c_im[l]), 2, 3), eye).astype(BF16)
    s5w = (bre, bim, cre, cim, pwre.reshape(8, S5_LANES), pwim.reshape(8, S5_LANES),
           s5_d[l].reshape(1, S5_WIDTH), w_glu[l].astype(BF16), b_glu[l].reshape(1, S5_WIDTH))

    wt = w_out[l][:GLA_WIDTH].astype(BF16)
    wb = w_out[l][GLA_WIDTH:].astype(BF16)
    n2 = norm2[l].reshape(1, D_MODEL)
    wq_bf = peer_wq[l].astype(BF16)
    kh, kl = _split(peer_keys[l].reshape(2 * PEER_HEADS, PEER_KEYS, PEER_DQ // 2))
    u_bf = peer_u[l].astype(BF16)
    vt_bf = jnp.swapaxes(peer_v[l].astype(BF16).reshape(-1, 2 * PEER_EXP_TILE, D_MODEL), 1, 2)
    fn = final_norm.reshape(1, D_MODEL)

    def tail(x2, og, osx):
        x1 = _out_proj(x2, og, osx, wt, wb)
        n1, e1, rho, e2g = _route(x1, n2, wq_bf, kh, kl)
        return _peer(x1, n2, n1, e1, rho, e2g, u_bf, vt_bf, fn)

    xp = x_prompt.reshape(n_p * len_p, D_MODEL)
    pp = _in_proj(xp, n1, w_bf, a2hi, a2lo, ba2)
    og_p, gla_p = _gla_prompt(pp, n_p, len_p, e2, gn)
    os_p, sre_p, sim_p = _s5_prompt(pp, n_p, len_p, s5w)
    y_p = tail(xp, og_p, os_p).reshape(n_p, len_p, D_MODEL)

    xs = x_sample.reshape(n_s * len_s, D_MODEL)
    ps = _in_proj(xs, n1, w_bf, a2hi, a2lo, ba2)
    og_s, gla_s = _gla_sample(ps, state_gla[l], len_s, e2, rep, gn)
    first_row = lambda s: jnp.pad(s.reshape(n_s, 1, S5_LANES), ((0, 0), (0, len_s - 1), (0, 0))
                                  ).reshape(n_s * len_s, S5_LANES)
    os_s, hre_s, him_s = _s5_sample(ps, first_row(state_s5_re[l]), first_row(state_s5_im[l]), len_s, s5w)
    y_s = tail(xs, og_s, os_s).reshape(n_s, len_s, D_MODEL)
    last_row = lambda hs: hs.reshape(n_s, len_s, S5_GROUPS, S5_STATE)[:, len_s - 1]

    st = lambda a: a.reshape(1, n_p, S5_GROUPS, S5_STATE)
    return (y_p, y_s, gla_p[None], st(sre_p), st(sim_p),
            gla_s[None], last_row(hre_s)[None], last_row(him_s)[None])
```

```python
import functools

import jax
import jax.numpy as jnp
from jax import lax
from jax.experimental import pallas as pl
from jax.experimental.pallas import tpu as pltpu

F32 = jnp.float32
BF16 = jnp.bfloat16

D_MODEL = 1024
GLA_HEADS = 4
GLA_DK = 64
GLA_DV = 128
GLA_QK = GLA_HEADS * GLA_DK
GLA_WIDTH = GLA_HEADS * GLA_DV
GLA_RANK = 16
GLA_TAU = 16.0
GLA_CHUNK = 128
S5_WIDTH = 512
S5_GROUP_CH = 16
S5_GROUPS = 32
S5_STATE = 64
S5_LANES = S5_GROUPS * S5_STATE
S5_NBLK = 4
S5_BLK_CH = S5_WIDTH // S5_NBLK
S5_BLK_ST = S5_LANES // S5_NBLK
PEER_KEYS = 128
PEER_EXPERTS = PEER_KEYS * PEER_KEYS
PEER_HEADS = 8
PEER_DQ = 256
PEER_TOPK = 16
EPS = 1e-6
LANES = 128
GELU_C = 0.7978845608028654
GELU_A = 0.044715

P_Q, P_K, P_V, P_G, P_U, P_LA = 0, 256, 512, 1024, 1536, 2048
P_COLS = 2304
W_IN_COLS = 2176

TOK_TILE = 256
PROJ_TILE = 512
PEER_TOK_TILE = 512
PEER_EXP_TILE = 512
SAMPLE_SEQ_BLOCK = 16
VMEM_LIMIT = 56 * 1024 * 1024


def _split(x):
    hi = x.astype(BF16)
    lo = (x - hi.astype(F32)).astype(BF16)
    return hi, lo


def _split3(x):
    a = x.astype(BF16)
    r = x - a.astype(F32)
    b = r.astype(BF16)
    c = (r - b.astype(F32)).astype(BF16)
    return a, b, c


def _dot(a, b):
    return jnp.dot(a, b, preferred_element_type=F32)


def _dot_nt(a, b):
    return lax.dot_general(a, b, (((1,), (1,)), ((), ())), preferred_element_type=F32)


def _dot_tn(a, b):
    return lax.dot_general(a, b, (((0,), (0,)), ((), ())), preferred_element_type=F32)


def _dot3(a, b_hi, b_lo):
    a_hi, a_lo = _split(a)
    return _dot(a_hi, b_hi) + _dot(a_lo, b_hi) + _dot(a_hi, b_lo)


def _dot_exact01(m01, x):
    a, b, c = _split3(x)
    return _dot(m01, a) + _dot(m01, b) + _dot(m01, c)


def _rms(x):
    return x * lax.rsqrt(jnp.mean(x * x, axis=-1, keepdims=True) + EPS)


def _params(sem):
    return pltpu.CompilerParams(dimension_semantics=sem, vmem_limit_bytes=VMEM_LIMIT)


def _const_spec(shape):
    n = len(shape)
    return pl.BlockSpec(shape, lambda *_: (0,) * n)


def _s5prep_kernel(lr_ref, li_ref, ldt_ref, bret_ref, bimt_ref,
                   pwre_ref, pwim_ref, bbre_ref, bbim_ref):
    lr = lr_ref[...]
    li = li_ref[...]
    dt = jnp.exp(ldt_ref[...])
    mag = jnp.exp(lr * dt)
    abr = mag * jnp.cos(li * dt)
    abi = mag * jnp.sin(li * dt)
    den = lr * lr + li * li
    nr = abr - 1.0
    ni = abi
    fr = (nr * lr + ni * li) / den
    fi = (ni * lr - nr * li) / den
    bret = bret_ref[...]
    bimt = bimt_ref[...]
    bbre_ref[...] = fr[:, None, :] * bret - fi[:, None, :] * bimt
    bbim_ref[...] = fr[:, None, :] * bimt + fi[:, None, :] * bret
    pr, pi = abr, abi
    for i in range(8):
        pwre_ref[i] = pr
        pwim_ref[i] = pi
        pr, pi = pr * abr - pi * abi, pr * abi + pi * abr


def _s5_prep(lam_re, lam_im, log_dt, b_re, b_im):
    g, p = lam_re.shape
    ch = b_re.shape[-1]
    bret = jnp.transpose(b_re, (0, 2, 1))
    bimt = jnp.transpose(b_im, (0, 2, 1))
    out = pl.pallas_call(
        _s5prep_kernel,
        out_shape=(jax.ShapeDtypeStruct((8, g, p), F32), jax.ShapeDtypeStruct((8, g, p), F32),
                   jax.ShapeDtypeStruct((g, ch, p), F32), jax.ShapeDtypeStruct((g, ch, p), F32)),
        name="s5_prep",
    )(lam_re, lam_im, log_dt.reshape(g, 1), bret, bimt)
    return out


def _inproj_kernel(x_ref, n1_ref, w_ref, a2hi_ref, a2lo_ref, ba2_ref, p_ref):
    h = _rms(x_ref[...]) * n1_ref[...]
    p = _dot(h.astype(BF16), w_ref[...])
    alr = p[:, P_LA:W_IN_COLS]
    z = _dot3(alr, a2hi_ref[...], a2lo_ref[...]) + ba2_ref[...]
    log_sig = jnp.minimum(z, 0.0) - jnp.log1p(jnp.exp(-jnp.abs(z)))
    p_ref[:, 0:P_LA] = p[:, 0:P_LA]
    p_ref[:, P_LA:P_COLS] = log_sig * (1.0 / GLA_TAU)


def _in_proj(x2, n1, w_bf, a2hi, a2lo, ba2):
    rows = x2.shape[0]
    return pl.pallas_call(
        _inproj_kernel,
        grid=(rows // PROJ_TILE,),
        in_specs=[pl.BlockSpec((PROJ_TILE, D_MODEL), lambda i: (i, 0)),
                  _const_spec(n1.shape), _const_spec(w_bf.shape),
                  _const_spec(a2hi.shape), _const_spec(a2lo.shape), _const_spec(ba2.shape)],
        out_specs=pl.BlockSpec((PROJ_TILE, P_COLS), lambda i: (i, 0)),
        out_shape=jax.ShapeDtypeStruct((rows, P_COLS), F32),
        compiler_params=_params(("parallel",)),
        name="in_proj",
    )(x2, n1, w_bf, a2hi, a2lo, ba2)


def _cumsum_rows(mask01, la):
    return _dot_exact01(mask01.astype(BF16), la)


GLA_BAND = 4
SUBLANES = 8


def _gla_prompt_kernel(q_ref, k_ref, v_ref, g_ref, la_ref, e2_ref, gn_ref,
                       o_ref, sfin_ref, st_ref, slab, *, n_chunks):
    c = pl.program_id(1)
    C = q_ref.shape[0]
    HC = GLA_HEADS * C

    @pl.when(c == 0)
    def _():
        st_ref[...] = jnp.zeros_like(st_ref)

    la = la_ref[...]
    row = lax.broadcasted_iota(jnp.int32, (C, C), 0)
    col = lax.broadcasted_iota(jnp.int32, (C, C), 1)
    b = _cumsum_rows(col <= row, la)
    q = q_ref[...] * (GLA_DK ** -0.5)
    k = k_ref[...]
    v = v_ref[...]
    blast = b[C - 1:C, :]

    rloc = lax.broadcasted_iota(jnp.int32, (C, GLA_QK), 0) % GLA_BAND
    tiles = lambda t: t.reshape(C // SUBLANES, SUBLANES, t.shape[-1])
    k3, b3, v3 = tiles(k), tiles(b), tiles(v)
    vsh = [v]
    for d in range(GLA_BAND):
        if d == 0:
            ks_, bs_ = k, b
        else:
            ks_ = pltpu.roll(k3, d, 1).reshape(C, GLA_QK)
            bs_ = pltpu.roll(b3, d, 1).reshape(C, GLA_QK)
            vsh.append(pltpu.roll(v3, d, 1).reshape(C, GLA_WIDTH))
        m = q * ks_ * jnp.exp(jnp.minimum(b - bs_, 0.0))
        slab[d * C:(d + 1) * C, :] = jnp.where(rloc >= d, m, 0.0).astype(BF16)
    rep = _dot(slab[...], e2_ref[...])
    o = rep[0:C] * vsh[0]
    for d in range(1, GLA_BAND):
        o = o + rep[d * C:(d + 1) * C] * vsh[d]

    lane_head = lax.broadcasted_iota(jnp.int32, (HC, GLA_QK), 1) // GLA_DK
    row_head = lax.broadcasted_iota(jnp.int32, (HC, GLA_QK), 0) // C
    own_head = lane_head == row_head
    si = lax.broadcasted_iota(jnp.int32, (HC, C), 0) % C
    sj = lax.broadcasted_iota(jnp.int32, (HC, C), 1)
    scores = jnp.zeros((HC, C), F32)
    s = C // 2
    while s >= GLA_BAND:
        ref = jnp.concatenate([jnp.broadcast_to(b[p * 2 * s + s - 1:p * 2 * s + s, :], (2 * s, GLA_QK))
                               for p in range(C // (2 * s))], axis=0)
        ql = q * jnp.exp(jnp.minimum(b - ref, 0.0))
        kl = (k * jnp.exp(jnp.minimum(ref - b, 0.0))).astype(BF16)
        qs = jnp.where(own_head, jnp.concatenate([ql] * GLA_HEADS, axis=0), 0.0).astype(BF16)
        lvl = ((si // (2 * s)) == (sj // (2 * s))) & ((si // s) % 2 == 1) & ((sj // s) % 2 == 0)
        scores = scores + jnp.where(lvl, _dot_nt(qs, kl), 0.0)
        s //= 2
    scores = scores.astype(BF16)
    vb = v.astype(BF16)

    st = st_ref[...]
    o = o + _dot_nt((q * jnp.exp(b)).astype(BF16), st.astype(BF16))
    outs = []
    for h in range(GLA_HEADS):
        vs = slice(h * GLA_DV, (h + 1) * GLA_DV)
        outs.append(_rms(o[:, vs] + _dot(scores[h * C:(h + 1) * C], vb[:, vs])))
    g = g_ref[...]
    o_ref[...] = jnp.concatenate(outs, axis=-1) * gn_ref[...] * (g * jax.nn.sigmoid(g))

    kd_hi, kd_lo = _split(k * jnp.exp(blast - b))
    vt_hi, vt_lo = _split(v.T)
    upd = _dot(vt_hi, kd_hi) + _dot(vt_lo, kd_hi) + _dot(vt_hi, kd_lo)
    blk = (lax.broadcasted_iota(jnp.int32, st.shape, 0) // GLA_DV
           == lax.broadcasted_iota(jnp.int32, st.shape, 1) // GLA_DK)
    st_new = jnp.exp(blast) * st + jnp.where(blk, upd, 0.0)
    st_ref[...] = st_new

    @pl.when(c == n_chunks - 1)
    def _():
        for h in range(GLA_HEADS):
            sfin_ref[0, h] = st_new[h * GLA_DV:(h + 1) * GLA_DV, h * GLA_DK:(h + 1) * GLA_DK].T


def _gla_prompt(p2, n_seq, seq_len, e2, gn):
    C = GLA_CHUNK
    nch = seq_len // C
    rows = n_seq * seq_len

    def tok(width, colblk):
        return pl.BlockSpec((C, width), lambda b, c: (b * nch + c, colblk))

    return pl.pallas_call(
        functools.partial(_gla_prompt_kernel, n_chunks=nch),
        grid=(n_seq, nch),
        in_specs=[tok(GLA_QK, P_Q // GLA_QK), tok(GLA_QK, P_K // GLA_QK),
                  tok(GLA_WIDTH, P_V // GLA_WIDTH), tok(GLA_WIDTH, P_G // GLA_WIDTH),
                  tok(GLA_QK, P_LA // GLA_QK),
                  _const_spec(e2.shape), _const_spec(gn.shape)],
        out_specs=[pl.BlockSpec((C, GLA_WIDTH), lambda b, c: (b * nch + c, 0)),
                   pl.BlockSpec((1, GLA_HEADS, GLA_DK, GLA_DV), lambda b, c: (b, 0, 0, 0))],
        out_shape=(jax.ShapeDtypeStruct((rows, GLA_WIDTH), F32),
                   jax.ShapeDtypeStruct((n_seq, GLA_HEADS, GLA_DK, GLA_DV), F32)),
        scratch_shapes=[pltpu.VMEM((GLA_WIDTH, GLA_QK), F32),
                        pltpu.VMEM((GLA_BAND * C, GLA_QK), BF16)],
        compiler_params=_params(("parallel", "arbitrary")),
        name="gla_prompt",
    )(p2, p2, p2, p2, p2, e2, gn)


def _gla_sample_kernel(q_ref, k_ref, v_ref, g_ref, la_ref, s0_ref, e2_ref, rep_ref, gn_ref,
                       o_ref, snew_ref, *, seq_len):
    R = q_ref.shape[0]
    nseq = R // seq_len
    SD = nseq * GLA_DK
    la = la_ref[...]
    row = lax.broadcasted_iota(jnp.int32, (R, R), 0)
    col = lax.broadcasted_iota(jnp.int32, (R, R), 1)
    same = (col // seq_len) == (row // seq_len)
    b = _cumsum_rows(same & (col <= row), la)
    btot = _cumsum_rows(same, la)
    q = q_ref[...] * (GLA_DK ** -0.5)
    k = k_ref[...]
    v = v_ref[...]
    rmod = lax.broadcasted_iota(jnp.int32, (R, GLA_QK), 0) % seq_len

    o = jnp.zeros((R, GLA_WIDTH), F32)
    for d in range(seq_len):
        ks_, bs_, vs_ = (k, b, v) if d == 0 else (pltpu.roll(k, d, 0), pltpu.roll(b, d, 0),
                                                  pltpu.roll(v, d, 0))
        m = q * ks_ * jnp.exp(jnp.minimum(b - bs_, 0.0))
        m = jnp.where(rmod >= d, m, 0.0)
        o = o + _dot(m.astype(BF16), e2_ref[...]) * vs_

    xr = lax.broadcasted_iota(jnp.int32, (R, GLA_HEADS * SD), 0) // seq_len
    xc = (lax.broadcasted_iota(jnp.int32, (R, GLA_HEADS * SD), 1) % SD) // GLA_DK
    own = xr == xc
    rep = rep_ref[...]

    def expand(x_bf16):
        return jnp.where(own, _dot(x_bf16, rep), 0.0).astype(BF16)

    qx = expand((q * jnp.exp(b)).astype(BF16))
    kd_hi, kd_lo = _split(k * jnp.exp(btot - b))
    kx_hi, kx_lo = expand(kd_hi), expand(kd_lo)
    ea, eb, ec = _split3(jnp.exp(btot))
    ax = (expand(ea), expand(eb), expand(ec))
    last = (lax.broadcasted_iota(jnp.int32, (R, GLA_DV), 0) % seq_len == seq_len - 1).astype(BF16)

    outs = []
    for h in range(GLA_HEADS):
        xs = slice(h * SD, (h + 1) * SD)
        vs = slice(h * GLA_DV, (h + 1) * GLA_DV)
        s0 = s0_ref[:, h].reshape(SD, GLA_DV)
        o_h = o[:, vs] + _dot(qx[:, xs], s0.astype(BF16))
        outs.append(_rms(o_h))
        v_hi, v_lo = _split(v[:, vs])
        upd = _dot_tn(kx_hi[:, xs], v_hi) + _dot_tn(kx_lo[:, xs], v_hi) + _dot_tn(kx_hi[:, xs], v_lo)
        decay = _dot_tn(ax[0][:, xs], last) + _dot_tn(ax[1][:, xs], last) + _dot_tn(ax[2][:, xs], last)
        snew_ref[:, h] = (decay * s0 + upd).reshape(nseq, GLA_DK, GLA_DV)
    g = g_ref[...]
    o_ref[...] = jnp.concatenate(outs, axis=-1) * gn_ref[...] * (g * jax.nn.sigmoid(g))


def _gla_sample(p2, s0, seq_len, e2, rep, gn):
    n_seq = s0.shape[0]
    R = SAMPLE_SEQ_BLOCK * seq_len
    nblk = n_seq // SAMPLE_SEQ_BLOCK

    def tok(width, colblk):
        return pl.BlockSpec((R, width), lambda i: (i, colblk))

    st_spec = pl.BlockSpec((SAMPLE_SEQ_BLOCK, GLA_HEADS, GLA_DK, GLA_DV), lambda i: (i, 0, 0, 0))
    return pl.pallas_call(
        functools.partial(_gla_sample_kernel, seq_len=seq_len),
        grid=(nblk,),
        in_specs=[tok(GLA_QK, P_Q // GLA_QK), tok(GLA_QK, P_K // GLA_QK),
                  tok(GLA_WIDTH, P_V // GLA_WIDTH), tok(GLA_WIDTH, P_G // GLA_WIDTH),
                  tok(GLA_QK, P_LA // GLA_QK), st_spec,
                  _const_spec(e2.shape), _const_spec(rep.shape), _const_spec(gn.shape)],
        out_specs=[pl.BlockSpec((R, GLA_WIDTH), lambda i: (i, 0)), st_spec],
        out_shape=(jax.ShapeDtypeStruct((n_seq * seq_len, GLA_WIDTH), F32),
                   jax.ShapeDtypeStruct(s0.shape, F32)),
        compiler_params=_params(("parallel",)),
        name="gla_sample",
    )(p2, p2, p2, p2, p2, s0, e2, rep, gn)


def _s5_local_scan(bur, bui, pwre_ref, pwim_ref, lanes, group):
    rows, width = bur.shape
    xr = bur.reshape(rows // 8, 8, width)
    xi = bui.reshape(rows // 8, 8, width)
    sub = lax.broadcasted_iota(jnp.int32, (8, width), 0) % group
    s = 1
    while s < group:
        ar = jnp.where(sub >= s, pwre_ref[s - 1:s, lanes], 0.0)[None]
        ai = jnp.where(sub >= s, pwim_ref[s - 1:s, lanes], 0.0)[None]
        sr = pltpu.roll(xr, s, 1)
        si = pltpu.roll(xi, s, 1)
        xr, xi = xr + ar * sr - ai * si, xi + ar * si + ai * sr
        s *= 2
    return xr.reshape(rows, width), xi.reshape(rows, width)


def _s5_tail(ys, u, d_ref, wglu_ref, bglu_ref):
    y = jnp.concatenate(ys, axis=-1) + d_ref[...] * u
    z = jax.nn.gelu(y)
    return z * jax.nn.sigmoid(_dot(z.astype(BF16), wglu_ref[...]) + bglu_ref[...])


def _s5_prompt_kernel(u_ref, bre_ref, bim_ref, cre_ref, cim_ref,
                      pwre_ref, pwim_ref, d_ref, wglu_ref, bglu_ref,
                      o_ref, stre_ref, stim_ref, hre_s, him_s, car_re, car_im):
    t = pl.program_id(1)
    TT = u_ref.shape[0]

    @pl.when(t == 0)
    def _():
        car_re[...] = jnp.zeros_like(car_re)
        car_im[...] = jnp.zeros_like(car_im)

    u = u_ref[...]
    for l in range(S5_NBLK):
        lanes = slice(l * S5_BLK_ST, (l + 1) * S5_BLK_ST)
        ub = u[:, l * S5_BLK_CH:(l + 1) * S5_BLK_CH].astype(BF16)
        bur, bui = _s5_local_scan(_dot(ub, bre_ref[l]), _dot(ub, bim_ref[l]), pwre_ref, pwim_ref, lanes, 8)
        hre_s[:, lanes] = bur
        him_s[:, lanes] = bui
    for l in range(S5_NBLK):
        lanes = slice(l * S5_BLK_ST, (l + 1) * S5_BLK_ST)
        p8r = pwre_ref[:, lanes]
        p8i = pwim_ref[:, lanes]

        def grp(r, carry, lanes=lanes, p8r=p8r, p8i=p8i):
            cr, ci = carry
            off = pl.multiple_of(r * 8, 8)
            xr = hre_s[pl.ds(off, 8), lanes] + p8r * cr - p8i * ci
            xi = him_s[pl.ds(off, 8), lanes] + p8r * ci + p8i * cr
            hre_s[pl.ds(off, 8), lanes] = xr
            him_s[pl.ds(off, 8), lanes] = xi
            return (jnp.broadcast_to(xr[7:8], xr.shape), jnp.broadcast_to(xi[7:8], xi.shape))

        cr0 = jnp.broadcast_to(car_re[:, lanes], (8, S5_BLK_ST))
        ci0 = jnp.broadcast_to(car_im[:, lanes], (8, S5_BLK_ST))
        cr, ci = lax.fori_loop(0, TT // 8, grp, (cr0, ci0))
        car_re[:, lanes] = cr[0:1]
        car_im[:, lanes] = ci[0:1]
    ys = []
    for l in range(S5_NBLK):
        lanes = slice(l * S5_BLK_ST, (l + 1) * S5_BLK_ST)
        ys.append(_dot(hre_s[:, lanes].astype(BF16), cre_ref[l]) - _dot(him_s[:, lanes].astype(BF16), cim_ref[l]))
    o_ref[...] = _s5_tail(ys, u, d_ref, wglu_ref, bglu_ref)
    stre_ref[0] = car_re[...]
    stim_ref[0] = car_im[...]


def _s5_sample_kernel(u_ref, s0re_ref, s0im_ref, bre_ref, bim_ref, cre_ref, cim_ref,
                      pwre_ref, pwim_ref, d_ref, wglu_ref, bglu_ref,
                      o_ref, hre_ref, him_ref, *, seq_len):
    u = u_ref[...]
    ys = []
    for l in range(S5_NBLK):
        lanes = slice(l * S5_BLK_ST, (l + 1) * S5_BLK_ST)
        ul = u[:, l * S5_BLK_CH:(l + 1) * S5_BLK_CH]
        ar = pwre_ref[0:1, lanes]
        ai = pwim_ref[0:1, lanes]
        sr = s0re_ref[:, lanes]
        si = s0im_ref[:, lanes]
        ub = ul.astype(BF16)
        bur = _dot(ub, bre_ref[l]) + (ar * sr - ai * si)
        bui = _dot(ub, bim_ref[l]) + (ar * si + ai * sr)
        bur, bui = _s5_local_scan(bur, bui, pwre_ref, pwim_ref, lanes, seq_len)
        hre_ref[:, lanes] = bur
        him_ref[:, lanes] = bui
        ys.append(_dot(bur.astype(BF16), cre_ref[l]) - _dot(bui.astype(BF16), cim_ref[l]))
    o_ref[...] = _s5_tail(ys, u, d_ref, wglu_ref, bglu_ref)


def _s5_weight_specs(ws):
    return [_const_spec(w.shape) for w in ws]


def _s5_prompt(p2, n_seq, seq_len, ws):
    TT = PROJ_TILE
    nt = seq_len // TT
    rows = n_seq * seq_len
    st_spec = pl.BlockSpec((1, 1, S5_LANES), lambda b, t: (b, 0, 0))
    return pl.pallas_call(
        _s5_prompt_kernel,
        grid=(n_seq, nt),
        in_specs=[pl.BlockSpec((TT, S5_WIDTH), lambda b, t: (b * nt + t, P_U // S5_WIDTH))]
        + _s5_weight_specs(ws),
        out_specs=[pl.BlockSpec((TT, S5_WIDTH), lambda b, t: (b * nt + t, 0)), st_spec, st_spec],
        out_shape=(jax.ShapeDtypeStruct((rows, S5_WIDTH), F32),
                   jax.ShapeDtypeStruct((n_seq, 1, S5_LANES), F32),
                   jax.ShapeDtypeStruct((n_seq, 1, S5_LANES), F32)),
        scratch_shapes=[pltpu.VMEM((TT, S5_LANES), F32), pltpu.VMEM((TT, S5_LANES), F32),
                        pltpu.VMEM((1, S5_LANES), F32), pltpu.VMEM((1, S5_LANES), F32)],
        compiler_params=_params(("parallel", "arbitrary")),
        name="s5_prompt",
    )(p2, *ws)


def _s5_sample(p2, s0re_rows, s0im_rows, seq_len, ws):
    rows = s0re_rows.shape[0]
    TT = TOK_TILE
    row_spec = pl.BlockSpec((TT, S5_LANES), lambda i: (i, 0))
    return pl.pallas_call(
        functools.partial(_s5_sample_kernel, seq_len=seq_len),
        grid=(rows // TT,),
        in_specs=[pl.BlockSpec((TT, S5_WIDTH), lambda i: (i, P_U // S5_WIDTH)), row_spec, row_spec]
        + _s5_weight_specs(ws),
        out_specs=[pl.BlockSpec((TT, S5_WIDTH), lambda i: (i, 0)), row_spec, row_spec],
        out_shape=(jax.ShapeDtypeStruct((rows, S5_WIDTH), F32),
                   jax.ShapeDtypeStruct((rows, S5_LANES), F32),
                   jax.ShapeDtypeStruct((rows, S5_LANES), F32)),
        compiler_params=_params(("parallel",)),
        name="s5_sample",
    )(p2, s0re_rows, s0im_rows, *ws)


def _first_and_nth_max(work, n):
    first = None
    thr = None
    need = None
    for r in range(n):
        m = jnp.max(work, axis=0, keepdims=True)
        hit = work == m
        if r == 0:
            first, thr = m, m
            need = float(n) - jnp.sum(jnp.where(hit, 1.0, 0.0), axis=0, keepdims=True)
        else:
            thr = jnp.where(need > 0.0, m, thr)
            if r < n - 1:
                need = need - jnp.sum(jnp.where(hit, 1.0, 0.0), axis=0, keepdims=True)
        if r < n - 1:
            work = jnp.where(hit, -jnp.inf, work)
    return first, thr


def _sorted_columns(s):
    n = s.shape[0] // 8
    cols = [s[8 * v:8 * (v + 1)] for v in range(n)]
    k = 2
    while k <= n:
        j = k // 2
        while j >= 1:
            for i in range(n):
                l = i ^ j
                if l > i:
                    hi, lo = jnp.maximum(cols[i], cols[l]), jnp.minimum(cols[i], cols[l])
                    cols[i], cols[l] = (hi, lo) if (i & k) == 0 else (lo, hi)
            j //= 2
        k *= 2
    return cols


def _top_values(s, n, store):
    cols = _sorted_columns(s)
    for r in range(n):
        m = jnp.max(cols[0], axis=0, keepdims=True)
        store(r, m)
        if r < n - 1:
            hit = cols[0] == m
            for v in range(n - 1 - r):
                nxt = cols[v + 1] if v + 1 < len(cols) else jnp.full_like(cols[v], -jnp.inf)
                cols[v] = jnp.where(hit, nxt, cols[v])


def _route_kernel(x1_ref, n2_ref, wq_ref, kh_ref, kl_ref,
                  n1_ref, e1_ref, rho_ref, e2_ref, s_s, top_s):
    TM = x1_ref.shape[0]
    h2 = _rms(x1_ref[...]) * n2_ref[...]
    qp = _dot(h2.astype(BF16), wq_ref[...])
    for hp in range(2 * PEER_HEADS):
        q_hi, q_lo = _split(qp[:, hp * PEER_KEYS:(hp + 1) * PEER_KEYS])
        s_s[hp] = _dot_nt(kh_ref[hp], q_hi) + _dot_nt(kl_ref[hp], q_hi) + _dot_nt(kh_ref[hp], q_lo)
    K = PEER_TOPK
    for lc in range(TM // LANES):
        sl = slice(lc * LANES, (lc + 1) * LANES)
        for hp in range(2 * PEER_HEADS):
            def store(r, m, hp=hp):
                top_s[hp, r:r + 1, sl] = m

            _top_values(s_s[hp, :, sl], K, store)
        for h in range(PEER_HEADS):
            v1 = top_s[2 * h, :, sl]
            v2 = top_s[2 * h + 1, :, sl]
            need = lambda b: v1[0:K // (b + 1)] + v2[b:b + 1]
            pad = jnp.full((1, v1.shape[1]), -jnp.inf, F32)
            cands = [v1 + v2[0:1], v1[0:8] + v2[1:2], v1[0:8] + v2[2:3],
                     jnp.concatenate([v1[0:6] + v2[3:4], need(7)], axis=0),
                     jnp.concatenate([need(4), need(5), need(6), pad], axis=0),
                     v2[8:K] + v1[0:1]]
            cand = jnp.concatenate(cands, axis=0)
            mx, thr = _first_and_nth_max(cand, K)
            z = jnp.sum(jnp.where(cand >= thr, jnp.exp(cand - mx), 0.0), axis=0, keepdims=True)
            cnt = jnp.zeros(v1.shape, F32)
            for b in range(K):
                cnt = cnt + jnp.where(v1 + v2[b:b + 1] >= thr, 1.0, 0.0)
            s1 = s_s[2 * h, :, sl]
            s2 = s_s[2 * h + 1, :, sl]
            n1 = jnp.where(s1 + v2[0:1] >= thr, 1.0, 0.0)
            rho = jnp.where(v1[0:1] + s2 >= thr, float(K // 2), float(K))
            for a in range(K // 2):
                n1 = jnp.where(s1 == v1[a:a + 1], cnt[a:a + 1], n1)
                rho = jnp.where(s2 == v2[a:a + 1], float(a), rho)
            n1_ref[h, :, sl] = n1
            e1_ref[h, :, sl] = jnp.exp(s1 - v1[0:1])
            rho_ref[h, :, sl] = rho.astype(BF16)
            e2_ref[h, :, sl] = (jnp.exp(s2 - v2[0:1]) * (1.0 / z)).astype(BF16)


def _mix_route_kernel(x_ref, og_ref, os_ref, wt_ref, wb_ref, n2_ref, wq_ref, kh_ref, kl_ref,
                      x1_ref, n1_ref, e1_ref, rho_ref, e2_ref, s_s, top_s):
    x1_ref[...] = (x_ref[...] + _dot(og_ref[...].astype(BF16), wt_ref[...])
                   + _dot(os_ref[...].astype(BF16), wb_ref[...]))
    _route_kernel(x1_ref, n2_ref, wq_ref, kh_ref, kl_ref, n1_ref, e1_ref, rho_ref, e2_ref, s_s, top_s)


def _mix_route(x2, og, osx, wt, wb, n2, wq_bf, kh, kl):
    rows = x2.shape[0]
    TM = TOK_TILE
    key_spec = pl.BlockSpec((PEER_HEADS, PEER_KEYS, TM), lambda i: (0, 0, i))
    big = pl.BlockSpec((TM, D_MODEL), lambda i: (i, 0))
    half = pl.BlockSpec((TM, GLA_WIDTH), lambda i: (i, 0))
    return pl.pallas_call(
        _mix_route_kernel,
        grid=(rows // TM,),
        in_specs=[big, half, half] + [_const_spec(w.shape) for w in (wt, wb, n2, wq_bf, kh, kl)],
        out_specs=[big] + [key_spec] * 4,
        out_shape=(jax.ShapeDtypeStruct((rows, D_MODEL), F32),
                   jax.ShapeDtypeStruct((PEER_HEADS, PEER_KEYS, rows), F32),
                   jax.ShapeDtypeStruct((PEER_HEADS, PEER_KEYS, rows), F32),
                   jax.ShapeDtypeStruct((PEER_HEADS, PEER_KEYS, rows), BF16),
                   jax.ShapeDtypeStruct((PEER_HEADS, PEER_KEYS, rows), BF16)),
        scratch_shapes=[pltpu.VMEM((2 * PEER_HEADS, PEER_KEYS, TM), F32),
                        pltpu.VMEM((2 * PEER_HEADS, PEER_TOPK, TM), F32)],
        compiler_params=_params(("parallel",)),
        name="mix_route",
    )(x2, og, osx, wt, wb, n2, wq_bf, kh, kl)


PEER_CHUNK = 256


def _peer_gate_piece(i1, c, a, act_ref, wact_ref, n1_ref, e1_ref, rho_ref, e2_ref):
    ls = slice(c * PEER_CHUNK, (c + 1) * PEER_CHUNK)
    rs = slice(a * PEER_KEYS, (a + 1) * PEER_KEYS)
    w = jnp.zeros((PEER_KEYS, PEER_CHUNK), BF16)
    for h in range(PEER_HEADS):
        n1 = n1_ref[h, pl.ds(i1, 1), ls].astype(BF16)
        g1 = e1_ref[h, pl.ds(i1, 1), ls].astype(BF16)
        e2 = e2_ref[h, :, ls]
        w = w + jnp.where(rho_ref[h, :, ls] < n1, e2, jnp.zeros_like(e2)) * g1
    x = act_ref[rs, ls]
    t = jnp.exp((x * (-2.0 * GELU_C - (2.0 * GELU_C * GELU_A) * (x * x))).astype(BF16))
    wact_ref[rs, ls] = w * (x.astype(BF16) / (1.0 + t))


def _peer_half(blk_gate, u_half, vt_half, act_in, wact_out, wact_in, act_out,
               acc, h2b, n1_ref, e1_ref, rho_ref, e2_ref, stages):
    sub = act_in.shape[0] // PEER_KEYS
    for c in range(act_in.shape[1] // PEER_CHUNK):
        ls = slice(c * PEER_CHUNK, (c + 1) * PEER_CHUNK)
        if "C" in stages:
            acc[:, ls] += _dot(vt_half, wact_in[:, ls])
        if "B" in stages:
            for a in range(sub):
                _peer_gate_piece(blk_gate * sub + a, c, a, act_in, wact_out, n1_ref, e1_ref, rho_ref, e2_ref)
        if "A" in stages:
            act_out[:, ls] = _dot_nt(u_half, h2b[c * PEER_CHUNK:(c + 1) * PEER_CHUNK, :])


def _peer_kernel(x1_ref, n2_ref, n1_ref, e1_ref, rho_ref, e2_ref, u_ref, vt_ref, fn_ref, y_ref,
                 acc, h2b, wact0, wact1, act0, act1, *, n_steps):
    g = pl.program_id(1)
    EB = act0.shape[0]
    refs = (acc, h2b, n1_ref, e1_ref, rho_ref, e2_ref)

    def halves(first, second):
        _peer_half(2 * g - 1, u_ref[0:EB, :], vt_ref[0, :, 0:EB], act1, wact1, wact0, act0, *refs,
                   stages=first)
        _peer_half(2 * g, u_ref[EB:2 * EB, :], vt_ref[0, :, EB:2 * EB], act0, wact0, wact1, act1, *refs,
                   stages=second)

    @pl.when(g == 0)
    def _():
        acc[...] = jnp.zeros_like(acc)
        h2b[...] = (_rms(x1_ref[...]) * n2_ref[...]).astype(BF16)
        halves("A", "AB")

    @pl.when((g > 0) & (g < n_steps - 1))
    def _():
        halves("ABC", "ABC")

    @pl.when(g == n_steps - 1)
    def _():
        halves("BC", "C")
        out = x1_ref[...] + acc[...].T
        y_ref[...] = _rms(out) * fn_ref[...]


def _peer(x1, n2, n1, e1, rho, e2, u_bf, vt_bf, fn):
    rows = x1.shape[0]
    TM = PEER_TOK_TILE
    EB = PEER_EXP_TILE
    n_steps = PEER_EXPERTS // (2 * EB) + 1
    big = pl.BlockSpec((TM, D_MODEL), lambda i, g: (i, 0))
    key_spec = pl.BlockSpec((PEER_HEADS, PEER_KEYS, TM), lambda i, g: (0, 0, i))
    return pl.pallas_call(
        functools.partial(_peer_kernel, n_steps=n_steps),
        grid=(rows // TM, n_steps),
        in_specs=[big, _const_spec(n2.shape), key_spec, key_spec, key_spec, key_spec,
                  pl.BlockSpec((2 * EB, D_MODEL), lambda i, g: (jnp.minimum(g, n_steps - 2), 0)),
                  pl.BlockSpec((1, D_MODEL, 2 * EB), lambda i, g: (jnp.maximum(g - 1, 0), 0, 0)),
                  _const_spec(fn.shape)],
        out_specs=big,
        out_shape=jax.ShapeDtypeStruct((rows, D_MODEL), F32),
        scratch_shapes=[pltpu.VMEM((D_MODEL, TM), F32), pltpu.VMEM((TM, D_MODEL), BF16),
                        pltpu.VMEM((EB, TM), BF16), pltpu.VMEM((EB, TM), BF16),
                        pltpu.VMEM((EB, TM), F32), pltpu.VMEM((EB, TM), F32)],
        compiler_params=_params(("parallel", "arbitrary")),
        name="peer_experts",
    )(x1, n2, n1, e1, rho, e2, u_bf, vt_bf, fn)


def _block_diag(w, eye):
    n, g, a, b = w.shape
    return jnp.einsum('lgab,gh->lgahb', w, eye).reshape(n, g * a, g * b)


def _gla_constants():
    lane_head = jnp.arange(GLA_QK) // GLA_DK
    e2 =(lane_head[:, None] == (jnp.arange(GLA_WIDTH) // GLA_DV)[None, :]).astype(BF16)
    sd = SAMPLE_SEQ_BLOCK * GLA_DK
    xc = jnp.arange(GLA_HEADS * sd)
    rep = ((lane_head[:, None] == (xc // sd)[None, :])
           & ((jnp.arange(GLA_QK) % GLA_DK)[:, None] == (xc % GLA_DK)[None, :])).astype(BF16)
    return e2, rep


def kernel(x_prompt, x_sample, state_gla, state_s5_re, state_s5_im, norm1, w_in, w_a2, b_a2, gla_norm, s5_lam_re, s5_lam_im, s5_log_dt, s5_b_re, s5_b_im, s5_c_re, s5_c_im, s5_d, w_glu, b_glu, w_out, norm2, peer_wq, peer_keys, peer_u, peer_v, final_norm):
    depth = norm1.shape[0]
    assert depth == 1, "single-layer trunk"
    n_p, len_p, _ = x_prompt.shape
    n_s, len_s, _ = x_sample.shape
    l = 0

    w = w_in[l]
    w_re = jnp.concatenate([w[:, 0:1536], w[:, 1552:2064], w[:, 1536:1552],
                            jnp.zeros((D_MODEL, W_IN_COLS - 2064), F32)], axis=1)
    w_bf = w_re.astype(BF16)
    a2 = jnp.concatenate([w_a2[l], jnp.zeros((W_IN_COLS - P_LA - GLA_RANK, GLA_QK), F32)], axis=0)
    a2hi, a2lo = _split(a2)
    ba2 = b_a2[l].reshape(1, GLA_QK)
    n1 = norm1[l].reshape(1, D_MODEL)
    gn = gla_norm[l].reshape(1, GLA_WIDTH)
    e2, rep = _gla_constants()

    pwre, pwim, bbre, bbim = _s5_prep(s5_lam_re[l], s5_lam_im[l], s5_log_dt[l], s5_b_re[l], s5_b_im[l])
    eye = jnp.eye(8, dtype=F32)
    blk = lambda t: t.reshape(S5_NBLK, 8, S5_GROUP_CH, S5_STATE)
    bre = _block_diag(blk(bbre), eye).astype(BF16)
    bim = _block_diag(blk(bbim), eye).astype(BF16)
    cre = _block_diag(jnp.swapaxes(blk(s5_c_re[l]), 2, 3), eye).astype(BF16)
    cim = _block_diag(jnp.swapaxes(blk(s5_c_im[l]), 2, 3), eye).astype(BF16)
    s5w = (bre, bim, cre, cim, pwre.reshape(8, S5_LANES), pwim.reshape(8, S5_LANES),
           s5_d[l].reshape(1, S5_WIDTH), w_glu[l].astype(BF16), b_glu[l].reshape(1, S5_WIDTH))

    wt = w_out[l][:GLA_WIDTH].astype(BF16)
    wb = w_out[l][GLA_WIDTH:].astype(BF16)
    n2 = norm2[l].reshape(1, D_MODEL)
    wq_bf = peer_wq[l].astype(BF16)
    kh, kl = _split(peer_keys[l].reshape(2 * PEER_HEADS, PEER_KEYS, PEER_DQ // 2))
    u_bf = peer_u[l].astype(BF16)
    vt_bf = jnp.swapaxes(peer_v[l].astype(BF16).reshape(-1, 2 * PEER_EXP_TILE, D_MODEL), 1, 2)
    fn = final_norm.reshape(1, D_MODEL)

    def tail(x2, og, osx):
        x1, n1, e1, rho, e2g = _mix_route(x2, og, osx, wt, wb, n2, wq_bf, kh, kl)
        return _peer(x1, n2, n1, e1, rho, e2g, u_bf, vt_bf, fn)

    xp = x_prompt.reshape(n_p * len_p, D_MODEL)
    pp = _in_proj(xp, n1, w_bf, a2hi, a2lo, ba2)
    og_p, gla_p = _gla_prompt(pp, n_p, len_p, e2, gn)
    os_p, sre_p, sim_p = _s5_prompt(pp, n_p, len_p, s5w)
    y_p = tail(xp, og_p, os_p).reshape(n_p, len_p, D_MODEL)

    xs = x_sample.reshape(n_s * len_s, D_MODEL)
    ps = _in_proj(xs, n1, w_bf, a2hi, a2lo, ba2)
    og_s, gla_s = _gla_sample(ps, state_gla[l], len_s, e2, rep, gn)
    first_row = lambda s: jnp.pad(s.reshape(n_s, 1, S5_LANES), ((0, 0), (0, len_s - 1), (0, 0))
                                  ).reshape(n_s * len_s, S5_LANES)
    os_s, hre_s, him_s = _s5_sample(ps, first_row(state_s5_re[l]), first_row(state_s5_im[l]), len_s, s5w)
    y_s = tail(xs, og_s, os_s).reshape(n_s, len_s, D_MODEL)
    last_row = lambda hs: hs.reshape(n_s, len_s, S5_GROUPS, S5_STATE)[:, len_s - 1]

    st = lambda a: a.reshape(1, n_p, S5_GROUPS, S5_STATE)
    return (y_p, y_s, gla_p[None], st(sre_p), st(sim_p),
            gla_s[None], last_row(hre_s)[None], last_row(him_s)[None])
```
